```python
import math
import jax, jax.numpy as jnp
from jax import lax
import numpy as np

D_MODEL = 2048
BATCH = 4
SEQ = 2048
DEPTH = 4
DEC_BATCH = 128
DEC_SEQ = 8
PAST_LEN = 16384
PAGE_SIZE = 128

N_BRANCH = 4
MIX_W = D_MODEL // 4
CONV_W = 4
LRU_W = MIX_W
LRU_BLOCKS = 8
LRU_BD = LRU_W // LRU_BLOCKS
LRU_C = 8.0
GM_W = MIX_W
GM_CHUNK = 128
GM_GROUPS = 4
GM_GD = GM_W // GM_GROUPS
RW_W = MIX_W
RW_HD = 64
RW_H = RW_W // RW_HD
RW_DECAY_R = 32
RW_A_R = 32
RW_G_R = 96
RW_P = 3 * RW_W + RW_DECAY_R + RW_A_R + RW_G_R
RW_LN_EPS = 64e-5
ML_W = MIX_W
ML_H = 4
ML_HD = ML_W // ML_H
ML_CHUNK = 64
D_FF = ((8 * D_MODEL + 3 * 256 - 1) // (3 * 256)) * 256
PROJ_SIZES = (LRU_W, LRU_W, GM_W, GM_W, RW_P, ML_W, ML_W, ML_W, ML_H, ML_H, N_BRANCH * D_MODEL)
P_TOTAL = sum(PROJ_SIZES)
N_STATE = 8

LAYER_PARAM_NAMES = (
    'norm_pre_mix', 'norm_post_mix', 'norm_pre_ffn', 'norm_post_ffn', 'w_in',
    'lru_conv_w', 'lru_conv_b', 'lru_wa', 'lru_ba', 'lru_wx', 'lru_bx', 'lru_lambda',
    'gmlp_ln_g', 'gmlp_ln_b', 'gmlp_ws', 'gmlp_bs',
    'rwkv_mu', 'rwkv_w0', 'rwkv_w2', 'rwkv_a0', 'rwkv_a2', 'rwkv_g2', 'rwkv_kk', 'rwkv_ka',
    'rwkv_rk', 'rwkv_ln_g', 'rwkv_ln_b',
    'mlstm_conv_w', 'mlstm_conv_b', 'mlstm_wq', 'mlstm_wk', 'mlstm_bi', 'mlstm_bf',
    'mlstm_norm_g', 'mlstm_skip',
    'w_branch', 'w_out', 'w_ff_in', 'w_ff_out')

kernel_name = 'gated_parallel_recurrent_hybrid_step'

F32 = jnp.float32


def _rmsnorm(x, g, eps=1e-6):
    xf = x.astype(F32)
    y = xf * lax.rsqrt(jnp.mean(xf * xf, axis=-1, keepdims=True) + eps)
    return (y * g.astype(F32)).astype(x.dtype)


def _layernorm(x, g, b, eps):
    xf = x.astype(F32)
    xc = xf - jnp.mean(xf, axis=-1, keepdims=True)
    y = xc * lax.rsqrt(jnp.mean(xc * xc, axis=-1, keepdims=True) + eps) * g.astype(F32)
    if b is not None:
        y = y + b.astype(F32)
    return y


def _split(z, sizes):
    idx = [int(i) for i in np.cumsum(sizes)[:-1]]
    return jnp.split(z, idx, axis=-1)


def _causal_conv(x, buf, w, b):
    T = x.shape[1]
    xp = jnp.concatenate([buf.astype(x.dtype), x], axis=1)
    y = b + w[0] * xp[:, 0:T]
    for j in range(1, CONV_W):
        y = y + w[j] * xp[:, j:j + T]
    return y, xp[:, T:]


def _lin_combine(e1, e2):
    a1, b1 = e1
    a2, b2 = e2
    return a1 * a2, a2 * b1 + b2


def _rglru_branch(z_x, z_g, conv_buf, h0, p, is_start):
    xc, conv_buf = _causal_conv(z_x, conv_buf, p['lru_conv_w'], p['lru_conv_b'])
    Bx, T, _ = xc.shape
    xb = xc.reshape(Bx, T, LRU_BLOCKS, LRU_BD)
    r = jax.nn.sigmoid((jnp.einsum('btni,nij->btnj', xb, p['lru_wa']).reshape(Bx, T, LRU_W) + p['lru_ba']).astype(F32))
    i = jax.nn.sigmoid((jnp.einsum('btni,nij->btnj', xb, p['lru_wx']).reshape(Bx, T, LRU_W) + p['lru_bx']).astype(F32))
    log_a = LRU_C * r * jax.nn.log_sigmoid(p['lru_lambda'].astype(F32))
    a = jnp.exp(log_a)
    mult = jnp.sqrt(-jnp.expm1(2.0 * log_a))
    if is_start:
        mult = mult.at[:, 0].set(1.0)
    b = mult * i * xc.astype(F32)
    b = b.at[:, 0].add(a[:, 0] * h0.astype(F32))
    _, h = lax.associative_scan(_lin_combine, (a, b), axis=1)
    y = h * jax.nn.gelu(z_g.astype(F32))
    return y, conv_buf, h[:, -1]


def _gmlp_branch(z_u, z_v, p):
    u = jax.nn.gelu(z_u.astype(F32))
    v = _layernorm(jax.nn.gelu(z_v.astype(F32)), p['gmlp_ln_g'], p['gmlp_ln_b'], 1e-5)
    Bx, T, _ = v.shape
    L = min(GM_CHUNK, T)
    nc = T // L
    vc = v.reshape(Bx, nc, L, GM_GROUPS, GM_GD)
    ws = jnp.tril(p['gmlp_ws'][:, :L, :L].astype(F32))
    bias = p['gmlp_bs'][:, :L].astype(F32).T
    s = jnp.einsum('gps,bcsgd->bcpgd', ws, vc) + bias[None, None, :, :, None]
    y = u * s.reshape(Bx, T, GM_W)
    return y, v


def _rwkv7_branch(z, shift_buf, S0, p):
    zf = z.astype(F32)
    Bx, T, _ = zf.shape
    prev = jnp.concatenate([shift_buf.astype(F32)[:, None], zf[:, :-1]], axis=1)
    zs = zf + (prev - zf) * p['rwkv_mu'].astype(F32)
    r, k, v, wd, ad, gd = _split(zs, (RW_W, RW_W, RW_W, RW_DECAY_R, RW_A_R, RW_G_R))
    w = -jax.nn.softplus(-(p['rwkv_w0'] + jnp.tanh(wd) @ p['rwkv_w2'])) - 0.5
    decay = jnp.exp(-jnp.exp(w))
    a = jax.nn.sigmoid(p['rwkv_a0'] + ad @ p['rwkv_a2'])
    g = jax.nn.sigmoid(gd) @ p['rwkv_g2']

    def hs(t):
        return t.reshape(Bx, T, RW_H, RW_HD)

    kk = hs(k * p['rwkv_kk'])
    kk = kk / jnp.maximum(jnp.linalg.norm(kk, axis=-1, keepdims=True), 1e-12)
    k = k * (1.0 + (a - 1.0) * p['rwkv_ka'])
    r, k, v, decay, a = hs(r), hs(k), hs(v), hs(decay), hs(a)

    def step(S, inp):
        r_t, w_t, k_t, v_t, kk_t, a_t = inp
        sa = jnp.einsum('bhvk,bhk->bhv', S, -kk_t)
        S = (S * w_t[:, :, None, :] + sa[..., None] * (kk_t * a_t)[:, :, None, :]
             + v_t[..., None] * k_t[:, :, None, :])
        return S, jnp.einsum('bhvk,bhk->bhv', S, r_t)

    def tm(t):
        return jnp.swapaxes(t, 0, 1)

    S, o = lax.scan(step, S0.astype(F32), (tm(r), tm(decay), tm(k), tm(v), tm(kk), tm(a)))
    o = tm(o)
    o = _layernorm(o, p['rwkv_ln_g'].reshape(RW_H, RW_HD), p['rwkv_ln_b'].reshape(RW_H, RW_HD), RW_LN_EPS)
    o = o + jnp.sum(r * k * p['rwkv_rk'].astype(F32), axis=-1, keepdims=True) * v
    y = o.reshape(Bx, T, RW_W) * g
    return y, zf[:, -1], S


def _mlstm_chunk(carry, inp):
    C, n, m = carry
    q, k, v, logi, logf = inp
    L = q.shape[2]
    b = jnp.cumsum(logf, axis=-1)
    mask = jnp.tril(jnp.ones((L, L), dtype=bool))
    logD = jnp.where(mask, b[..., :, None] - b[..., None, :] + logi[..., None, :], -jnp.inf)
    inter = b + m[..., None]
    m_t = jnp.maximum(jnp.max(logD, axis=-1), inter)
    s = jnp.einsum('bhtd,bhsd->bhts', q, k) * jnp.exp(logD - m_t[..., None])
    sc = jnp.exp(inter - m_t)
    num = jnp.einsum('bhts,bhsd->bhtd', s, v) + sc[..., None] * jnp.einsum('bhvk,bhtk->bhtv', C, q)
    den = jnp.sum(s, axis=-1) + sc * jnp.einsum('bhk,bhtk->bht', n, q)
    h = num / jnp.maximum(jnp.abs(den), jnp.exp(-m_t))[..., None]
    m_new = m_t[..., -1]
    wj = jnp.exp(b[..., -1:] - b + logi - m_new[..., None])
    dec = jnp.exp(b[..., -1] + m - m_new)
    C_new = dec[..., None, None] * C + jnp.einsum('bhs,bhsv,bhsk->bhvk', wj, v, k)
    n_new = dec[..., None] * n + jnp.einsum('bhs,bhsk->bhk', wj, k)
    return (C_new, n_new, m_new), h


def _mlstm_branch(z_x, z_v, z_o, z_i, z_f, conv_buf, C0, n0, m0, p):
    c, conv_buf = _causal_conv(z_x, conv_buf, p['mlstm_conv_w'], p['mlstm_conv_b'])
    c = jax.nn.silu(c.astype(F32))
    Bx, T, _ = c.shape
    ch = c.reshape(Bx, T, ML_H, ML_HD)
    q = jnp.einsum('bthi,hij->bhtj', ch, p['mlstm_wq'].astype(F32))
    k = jnp.einsum('bthi,hij->bhtj', ch, p['mlstm_wk'].astype(F32)) * (ML_HD ** -0.5)
    v = z_v.astype(F32).reshape(Bx, T, ML_H, ML_HD).transpose(0, 2, 1, 3)
    logi = (z_i.astype(F32) + p['mlstm_bi']).transpose(0, 2, 1)
    logf = jax.nn.log_sigmoid(z_f.astype(F32) + p['mlstm_bf']).transpose(0, 2, 1)
    L = math.gcd(T, ML_CHUNK)
    nc = T // L

    def chunk(t):
        return jnp.moveaxis(t.reshape((Bx, ML_H, nc, L) + t.shape[3:]), 2, 0)

    (C, n, m), h = lax.scan(_mlstm_chunk, (C0.astype(F32), n0.astype(F32), m0.astype(F32)),
                            (chunk(q), chunk(k), chunk(v), chunk(logi), chunk(logf)))
    h = jnp.moveaxis(h, 0, 2).reshape(Bx, ML_H, T, ML_HD).transpose(0, 2, 1, 3)
    h = _layernorm(h, p['mlstm_norm_g'].reshape(ML_H, ML_HD), None, 1e-6)
    y = jax.nn.sigmoid(z_o.astype(F32)) * (h.reshape(Bx, T, ML_W) + p['mlstm_skip'] * c)
    return y, conv_buf, C, n, m


def _layer(x, lru_buf, lru_h, rw_shift, rw_S, ml_buf, ml_C, ml_n, ml_m, p, is_start):
    dt = x.dtype
    Bx, T, _ = x.shape
    hn = _rmsnorm(x, p['norm_pre_mix'])
    z = jnp.einsum('btd,dp->btp', hn, p['w_in'])
    (z_lx, z_lg, z_gu, z_gv, z_rw, z_mx, z_mv, z_mo, z_mi, z_mf, z_gate) = _split(z, PROJ_SIZES)
    y_a, lru_buf, lru_h = _rglru_branch(z_lx, z_lg, lru_buf, lru_h, p, is_start)
    y_b, gm_v = _gmlp_branch(z_gu, z_gv, p)
    y_c, rw_shift, rw_S = _rwkv7_branch(z_rw, rw_shift, rw_S, p)
    y_d, ml_buf, ml_C, ml_n, ml_m = _mlstm_branch(z_mx, z_mv, z_mo, z_mi, z_mf, ml_buf, ml_C, ml_n, ml_m, p)
    br = jnp.stack([y_a, y_b, y_c, y_d], axis=2).astype(dt)
    br = jnp.einsum('btnc,ncd->btnd', br, p['w_branch'])
    gates = jax.nn.sigmoid(z_gate.reshape(Bx, T, N_BRANCH, D_MODEL))
    merged = jnp.sum(gates * br, axis=2)
    mix = jnp.einsum('btd,de->bte', merged, p['w_out'])
    x = x + _rmsnorm(mix, p['norm_post_mix'])
    hf = _rmsnorm(x, p['norm_pre_ffn'])
    gu = jnp.einsum('btd,df->btf', hf, p['w_ff_in'])
    g, u = jnp.split(gu, 2, axis=-1)
    ff = jnp.einsum('btf,fd->btd', jax.nn.silu(g) * u, p['w_ff_out'])
    x = x + _rmsnorm(ff, p['norm_post_ffn'])
    return x, (lru_buf, lru_h, rw_shift, rw_S, ml_buf, ml_C, ml_n, ml_m), gm_v


def _zero_states(bsz):
    def z(*s):
        return jnp.zeros((bsz,) + s, F32)
    return (z(CONV_W - 1, LRU_W), z(LRU_W), z(RW_P), z(RW_H, RW_HD, RW_HD),
            z(CONV_W - 1, ML_W), z(ML_H, ML_HD, ML_HD), z(ML_H, ML_HD), z(ML_H))


def setup_inputs(seed: int = 0) -> dict:
    key = jax.random.key(seed)
    ks = iter(jax.random.split(key, 64))

    def nrm(shape, scale=1.0):
        return jax.random.normal(next(ks), shape, jnp.float32) * scale

    def unif(shape, lo, hi):
        return jax.random.uniform(next(ks), shape, jnp.float32, lo, hi)

    def gain(shape):
        return 1.0 + 0.05 * nrm(shape)

    Ly = DEPTH
    a8 = unif((Ly, LRU_W), 0.9, 0.999) ** (1.0 / LRU_C)
    return {
        'x_prompt': nrm((BATCH, SEQ, D_MODEL)),
        'x_sample': nrm((DEC_BATCH, DEC_SEQ, D_MODEL)),
        'state_lru_conv': nrm((Ly, DEC_BATCH, CONV_W - 1, LRU_W)),
        'state_lru_h': nrm((Ly, DEC_BATCH, LRU_W), 0.5),
        'state_rwkv_shift': nrm((Ly, DEC_BATCH, RW_P)),
        'state_rwkv_wkv': nrm((Ly, DEC_BATCH, RW_H, RW_HD, RW_HD), 0.5),
        'state_mlstm_conv': nrm((Ly, DEC_BATCH, CONV_W - 1, ML_W)),
        'state_mlstm_C': nrm((Ly, DEC_BATCH, ML_H, ML_HD, ML_HD), 0.1),
        'state_mlstm_n': nrm((Ly, DEC_BATCH, ML_H, ML_HD), 0.1),
        'state_mlstm_m': nrm((Ly, DEC_BATCH, ML_H), 0.5),
        'norm_pre_mix': gain((Ly, D_MODEL)),
        'norm_post_mix': gain((Ly, D_MODEL)),
        'norm_pre_ffn': gain((Ly, D_MODEL)),
        'norm_post_ffn': gain((Ly, D_MODEL)),
        'w_in': nrm((Ly, D_MODEL, P_TOTAL), D_MODEL ** -0.5),
        'lru_conv_w': nrm((Ly, CONV_W, LRU_W), CONV_W ** -0.5),
        'lru_conv_b': nrm((Ly, LRU_W), 0.01),
        'lru_wa': nrm((Ly, LRU_BLOCKS, LRU_BD, LRU_BD), LRU_BD ** -0.5),
        'lru_ba': nrm((Ly, LRU_W), 0.01),
        'lru_wx': nrm((Ly, LRU_BLOCKS, LRU_BD, LRU_BD), LRU_BD ** -0.5),
        'lru_bx': nrm((Ly, LRU_W), 0.01),
        'lru_lambda': jnp.log(a8) - jnp.log1p(-a8),
        'gmlp_ln_g': gain((Ly, GM_W)),
        'gmlp_ln_b': nrm((Ly, GM_W), 0.01),
        'gmlp_ws': nrm((Ly, GM_GROUPS, GM_CHUNK, GM_CHUNK), GM_CHUNK ** -0.5),
        'gmlp_bs': gain((Ly, GM_GROUPS, GM_CHUNK)),
        'rwkv_mu': unif((Ly, RW_P), 0.0, 1.0),
        'rwkv_w0': unif((Ly, RW_W), -6.0, -1.0),
        'rwkv_w2': nrm((Ly, RW_DECAY_R, RW_W), 0.1),
        'rwkv_a0': nrm((Ly, RW_W), 0.1),
        'rwkv_a2': nrm((Ly, RW_A_R, RW_W), 0.1),
        'rwkv_g2': nrm((Ly, RW_G_R, RW_W), RW_G_R ** -0.5),
        'rwkv_kk': 0.85 + 0.05 * nrm((Ly, RW_W)),
        'rwkv_ka': gain((Ly, RW_W)),
        'rwkv_rk': nrm((Ly, RW_H, RW_HD), 0.1),
        'rwkv_ln_g': gain((Ly, RW_W)),
        'rwkv_ln_b': nrm((Ly, RW_W), 0.01),
        'mlstm_conv_w': nrm((Ly, CONV_W, ML_W), CONV_W ** -0.5),
        'mlstm_conv_b': nrm((Ly, ML_W), 0.01),
        'mlstm_wq': nrm((Ly, ML_H, ML_HD, ML_HD), ML_HD ** -0.5),
        'mlstm_wk': nrm((Ly, ML_H, ML_HD, ML_HD), ML_HD ** -0.5),
        'mlstm_bi': nrm((Ly, ML_H), 0.1),
        'mlstm_bf': jnp.linspace(3.0, 6.0, ML_H, dtype=jnp.float32)[None, :] + nrm((Ly, ML_H), 0.05),
        'mlstm_norm_g': gain((Ly, ML_W)),
        'mlstm_skip': gain((Ly, ML_W)),
        'w_branch': nrm((Ly, N_BRANCH, MIX_W, D_MODEL), MIX_W ** -0.5),
        'w_out': nrm((Ly, D_MODEL, D_MODEL), D_MODEL ** -0.5),
        'w_ff_in': nrm((Ly, D_MODEL, 2 * D_FF), D_MODEL ** -0.5),
        'w_ff_out': nrm((Ly, D_FF, D_MODEL), D_FF ** -0.5),
    }


def reference(x_prompt, x_sample, state_lru_conv, state_lru_h, state_rwkv_shift, state_rwkv_wkv,
              state_mlstm_conv, state_mlstm_C, state_mlstm_n, state_mlstm_m,
              norm_pre_mix, norm_post_mix, norm_pre_ffn, norm_post_ffn, w_in,
              lru_conv_w, lru_conv_b, lru_wa, lru_ba, lru_wx, lru_bx, lru_lambda,
              gmlp_ln_g, gmlp_ln_b, gmlp_ws, gmlp_bs,
              rwkv_mu, rwkv_w0, rwkv_w2, rwkv_a0, rwkv_a2, rwkv_g2, rwkv_kk, rwkv_ka,
              rwkv_rk, rwkv_ln_g, rwkv_ln_b,
              mlstm_conv_w, mlstm_conv_b, mlstm_wq, mlstm_wk, mlstm_bi, mlstm_bf,
              mlstm_norm_g, mlstm_skip,
              w_branch, w_out, w_ff_in, w_ff_out):
    weights = (norm_pre_mix, norm_post_mix, norm_pre_ffn, norm_post_ffn, w_in,
               lru_conv_w, lru_conv_b, lru_wa, lru_ba, lru_wx, lru_bx, lru_lambda,
               gmlp_ln_g, gmlp_ln_b, gmlp_ws, gmlp_bs,
               rwkv_mu, rwkv_w0, rwkv_w2, rwkv_a0, rwkv_a2, rwkv_g2, rwkv_kk, rwkv_ka,
               rwkv_rk, rwkv_ln_g, rwkv_ln_b,
               mlstm_conv_w, mlstm_conv_b, mlstm_wq, mlstm_wk, mlstm_bi, mlstm_bf,
               mlstm_norm_g, mlstm_skip,
               w_branch, w_out, w_ff_in, w_ff_out)
    sample_states = (state_lru_conv, state_lru_h, state_rwkv_shift, state_rwkv_wkv,
                     state_mlstm_conv, state_mlstm_C, state_mlstm_n, state_mlstm_m)
    bp = x_prompt.shape[0]
    yp, ys = x_prompt, x_sample
    new_p = [[] for _ in range(N_STATE)]
    new_s = [[] for _ in range(N_STATE)]
    gm_v = []
    for l in range(DEPTH):
        p = dict(zip(LAYER_PARAM_NAMES, [w[l] for w in weights]))
        yp, st_p, _ = _layer(yp, *_zero_states(bp), p, True)
        ys, st_s, v_s = _layer(ys, *[s[l] for s in sample_states], p, False)
        for j in range(N_STATE):
            new_p[j].append(st_p[j].astype(x_prompt.dtype))
            new_s[j].append(st_s[j].astype(x_sample.dtype))
        gm_v.append(v_s.astype(x_sample.dtype))
    (p_lru_conv, p_lru_h, p_rwkv_shift, p_rwkv_wkv,
     p_mlstm_conv, p_mlstm_C, p_mlstm_n, p_mlstm_m) = [jnp.stack(t, axis=0) for t in new_p]
    (s_lru_conv, s_lru_h, s_rwkv_shift, s_rwkv_wkv,
     s_mlstm_conv, s_mlstm_C, s_mlstm_n, s_mlstm_m) = [jnp.stack(t, axis=0) for t in new_s]
    s_gmlp_v = jnp.stack(gm_v, axis=0)
    return (yp, ys,
            p_lru_conv, p_lru_h, p_rwkv_shift, p_rwkv_wkv, p_mlstm_conv, p_mlstm_C, p_mlstm_n, p_mlstm_m,
            s_lru_conv, s_lru_h, s_rwkv_shift, s_rwkv_wkv, s_mlstm_conv, s_mlstm_C, s_mlstm_n, s_mlstm_m,
            s_gmlp_v)
```

```python
import functools

import numpy as np
import jax
import jax.numpy as jnp
from jax import lax
from jax.experimental import pallas as pl
from jax.experimental.pallas import tpu as pltpu

F32 = jnp.float32
BF16 = jnp.bfloat16

D_MODEL = 2048
DEPTH = 4
MIX_W = 512
CONV_W = 4
LRU_BLOCKS = 8
LRU_C = 8.0
GM_CHUNK = 128
GM_GROUPS = 4
RW_HD = 64
RW_H = 8
RW_RKV = 3 * MIX_W
RW_TAIL = 160
RW_TAIL_PAD = 256
RW_LN_EPS = 64e-5
ML_H = 4
ML_HD = 128
D_FF = 5632
N_GATE = 4 * D_MODEL
P_SRC = 5288
Z_COLS = 5632

ZB_LX, ZB_LG, ZB_GU, ZB_GV, ZB_R, ZB_K, ZB_V, ZB_MX, ZB_MV, ZB_MO = range(10)
Z_TAIL = 5120
Z_MIF = 5376

RW_CHUNK = 64
ML_CHUNK_P = 128
VMEM_LIMIT = 56 * 1024 * 1024
HI = lax.Precision.HIGHEST


def _cparams(sem):
    return pltpu.CompilerParams(dimension_semantics=sem, vmem_limit_bytes=VMEM_LIMIT)


def _softplus(x):
    return jnp.maximum(x, 0.0) + jnp.log1p(jnp.exp(-jnp.abs(x)))


def _log_sigmoid(x):
    return -_softplus(-x)


def _rms(x, g):
    return x * lax.rsqrt(jnp.mean(x * x, axis=-1, keepdims=True) + 1e-6) * g


def _dot(a, b):
    return jnp.dot(a.astype(BF16), b.astype(BF16), preferred_element_type=F32)


def _dot_nt(a, b):
    return lax.dot_general(a.astype(BF16), b.astype(BF16), (((1,), (1,)), ((), ())),
                           preferred_element_type=F32)


def _dot_tn(a, b):
    return lax.dot_general(a.astype(BF16), b.astype(BF16), (((0,), (0,)), ((), ())),
                           preferred_element_type=F32)


def _dot_hi(a, b):
    return jnp.dot(a, b, preferred_element_type=F32, precision=HI)


def _inproj_kernel(x_ref, g_ref, w_ref, z_ref, hn_ref):
    @pl.when(pl.program_id(1) == 0)
    def _():
        hn_ref[...] = _rms(x_ref[...], g_ref[...]).astype(BF16)

    z_ref[...] = jnp.dot(hn_ref[...], w_ref[...], preferred_element_type=F32)


def _inproj(x, g, w, l, tm, tn=512):
    m = x.shape[0]
    return pl.pallas_call(
        _inproj_kernel,
        grid=(m // tm, Z_COLS // tn),
        in_specs=[pl.BlockSpec((tm, D_MODEL), lambda i, j: (i, 0)),
                  pl.BlockSpec((None, 1, D_MODEL), lambda i, j: (l, 0, 0)),
                  pl.BlockSpec((None, D_MODEL, tn), lambda i, j: (l, 0, j))],
        out_specs=[pl.BlockSpec((tm, tn), lambda i, j: (i, j)),
                   pl.BlockSpec((tm, D_MODEL), lambda i, j: (i, 0))],
        out_shape=[jax.ShapeDtypeStruct((m, Z_COLS), F32),
                   jax.ShapeDtypeStruct((m, D_MODEL), BF16)],
        compiler_params=_cparams(("parallel", "arbitrary")),
        name="inproj",
    )(x, g, w)


def _merge_kernel(hn_ref, ya_ref, yb_ref, yc_ref, yd_ref, g0_ref, g1_ref, g2_ref, g3_ref, wb_ref, o_ref):
    hn = hn_ref[...]
    acc = None
    for b, (y_ref, wg_ref) in enumerate(((ya_ref, g0_ref), (yb_ref, g1_ref), (yc_ref, g2_ref), (yd_ref, g3_ref))):
        zg = jnp.dot(hn, wg_ref[...], preferred_element_type=F32)
        br = jnp.dot(y_ref[...], wb_ref[b], preferred_element_type=F32)
        term = jax.nn.sigmoid(zg) * br
        acc = term if acc is None else acc + term
    o_ref[...] = acc.astype(BF16)


def _merge(hn, ys, wgate, wbranch, l, tm, tn=256):
    m = hn.shape[0]
    nb = D_MODEL // tn
    y_spec = pl.BlockSpec((tm, MIX_W), lambda j, i: (i, 0))
    g_specs = [pl.BlockSpec((None, D_MODEL, tn), lambda j, i, b=b: (l, 0, b * nb + j)) for b in range(4)]
    return pl.pallas_call(
        _merge_kernel,
        grid=(nb, m // tm),
        in_specs=[pl.BlockSpec((tm, D_MODEL), lambda j, i: (i, 0)), y_spec, y_spec, y_spec, y_spec,
                  *g_specs,
                  pl.BlockSpec((None, 4, MIX_W, tn), lambda j, i: (l, 0, 0, j))],
        out_specs=pl.BlockSpec((tm, tn), lambda j, i: (i, j)),
        out_shape=jax.ShapeDtypeStruct((m, D_MODEL), BF16),
        compiler_params=_cparams(("parallel", "arbitrary")),
        name="merge",
    )(hn, *ys, wgate, wgate, wgate, wgate, wbranch)


def _outproj_kernel(mg_ref, w_ref, x_ref, gpost_ref, gpre_ref, x1_ref, hf_ref):
    mix = jnp.dot(mg_ref[...], w_ref[...], preferred_element_type=F32)
    x1 = x_ref[...] + _rms(mix, gpost_ref[...])
    x1_ref[...] = x1
    hf_ref[...] = _rms(x1, gpre_ref[...]).astype(BF16)


def _outproj(merged, w_out, x, g_post, g_pre_ffn, l, tm):
    m = x.shape[0]
    g_spec = pl.BlockSpec((None, 1, D_MODEL), lambda i: (l, 0, 0))
    row_spec = pl.BlockSpec((tm, D_MODEL), lambda i: (i, 0))
    return pl.pallas_call(
        _outproj_kernel,
        grid=(m // tm,),
        in_specs=[row_spec, pl.BlockSpec((None, D_MODEL, D_MODEL), lambda i: (l, 0, 0)), row_spec, g_spec, g_spec],
        out_specs=[row_spec, row_spec],
        out_shape=[jax.ShapeDtypeStruct((m, D_MODEL), F32), jax.ShapeDtypeStruct((m, D_MODEL), BF16)],
        compiler_params=_cparams(("parallel",)),
        name="outproj",
    )(merged, w_out, x, g_post, g_pre_ffn)


def _ffup_kernel(hf_ref, wg_ref, wu_ref, o_ref):
    hf = hf_ref[...]
    g = jnp.dot(hf, wg_ref[...], preferred_element_type=F32)
    u = jnp.dot(hf, wu_ref[...], preferred_element_type=F32)
    o_ref[...] = (jax.nn.silu(g) * u).astype(BF16)


def _ffup(hf, w_ff_in, l, tm, tn=512):
    m = hf.shape[0]
    nb = D_FF // tn
    return pl.pallas_call(
        _ffup_kernel,
        grid=(nb, m // tm),
        in_specs=[pl.BlockSpec((tm, D_MODEL), lambda j, i: (i, 0)),
                  pl.BlockSpec((None, D_MODEL, tn), lambda j, i: (l, 0, j)),
                  pl.BlockSpec((None, D_MODEL, tn), lambda j, i: (l, 0, nb + j))],
        out_specs=pl.BlockSpec((tm, tn), lambda j, i: (i, j)),
        out_shape=jax.ShapeDtypeStruct((m, D_FF), BF16),
        compiler_params=_cparams(("parallel", "arbitrary")),
        name="ffup",
    )(hf, w_ff_in, w_ff_in)


def _ffdown_kernel(h_ref, w_ref, x1_ref, g_ref, o_ref, acc_ref):
    k = pl.program_id(1)

    @pl.when(k == 0)
    def _():
        acc_ref[...] = jnp.zeros_like(acc_ref)

    acc_ref[...] += jnp.dot(h_ref[...], w_ref[...], preferred_element_type=F32)

    @pl.when(k == pl.num_programs(1) - 1)
    def _():
        o_ref[...] = x1_ref[...] + _rms(acc_ref[...], g_ref[...])


def _ffdown(h, w_ff_out, x1, g_post, l, tm, tk=512):
    m = x1.shape[0]
    row_spec = pl.BlockSpec((tm, D_MODEL), lambda i, k: (i, 0))
    return pl.pallas_call(
        _ffdown_kernel,
        grid=(m // tm, D_FF // tk),
        in_specs=[pl.BlockSpec((tm, tk), lambda i, k: (i, k)),
                  pl.BlockSpec((None, tk, D_MODEL), lambda i, k: (l, k, 0)),
                  row_spec,
                  pl.BlockSpec((None, 1, D_MODEL), lambda i, k: (l, 0, 0))],
        out_specs=row_spec,
        out_shape=jax.ShapeDtypeStruct((m, D_MODEL), F32),
        scratch_shapes=[pltpu.VMEM((tm, D_MODEL), F32)],
        compiler_params=_cparams(("parallel", "arbitrary")),
        name="ffdown",
    )(h, w_ff_out, x1, g_post)


def _lru_kernel(zx_ref, zg_ref, buf_ref, h0_ref, cw_ref, cb_ref, wa_ref, ba_ref, wx_ref, bx_ref, lam_ref,
                y_ref, hout_ref, xs_ref, a_ref, b_ref, h_ref, *, is_start, bb, tt):
    t = pl.program_id(1)
    c = MIX_W

    @pl.when(t == 0)
    def _():
        xs_ref[:, 5:8, :] = buf_ref[...]
        h_ref[...] = jnp.broadcast_to(h0_ref[...], (bb, 8, c))

    @pl.when(t > 0)
    def _():
        xs_ref[:, 5:8, :] = xs_ref[:, tt + 5:tt + 8, :]

    xs_ref[:, 8:8 + tt, :] = zx_ref[...]
    xc = cb_ref[...] + cw_ref[0:1, :] * xs_ref[:, 5:5 + tt, :]
    for j in range(1, CONV_W):
        xc = xc + cw_ref[j:j + 1, :] * xs_ref[:, 5 + j:5 + j + tt, :]
    xc2 = xc.reshape(bb * tt, c)
    r = jax.nn.sigmoid(_dot(xc2, wa_ref[...]) + ba_ref[...])
    i = jax.nn.sigmoid(_dot(xc2, wx_ref[...]) + bx_ref[...])
    log_a = LRU_C * r * _log_sigmoid(lam_ref[...])
    a = jnp.exp(log_a)
    mult = jnp.sqrt(1.0 - jnp.exp(2.0 * log_a))
    if is_start:
        tpos = lax.broadcasted_iota(jnp.int32, (bb, tt, c), 1).reshape(bb * tt, c) + t * tt
        mult = jnp.where(tpos == 0, 1.0, mult)
    a_ref[...] = a.reshape(bb, tt, c)
    b_ref[...] = (mult * i * xc2).reshape(bb, tt, c)

    row = lax.broadcasted_iota(jnp.int32, (bb, 8, c), 1).reshape(bb * 8, c)

    def group(gi, carry):
        off = pl.multiple_of(gi * 8, 8)
        av = a_ref[:, pl.ds(off, 8), :].reshape(bb * 8, c)
        bv = b_ref[:, pl.ds(off, 8), :].reshape(bb * 8, c)
        for s in (1, 2, 4):
            keep = row >= s
            a_sh = pltpu.roll(av, s, 0)
            b_sh = pltpu.roll(bv, s, 0)
            bv = jnp.where(keep, av * b_sh + bv, bv)
            av = jnp.where(keep, av * a_sh, av)
        hh = (av * h_ref[...].reshape(bb * 8, c) + bv).reshape(bb, 8, c)
        b_ref[:, pl.ds(off, 8), :] = hh
        h_ref[...] = jnp.broadcast_to(hh[:, 7:8, :], (bb, 8, c))
        return carry

    lax.fori_loop(0, tt // 8, group, 0)
    y_ref[...] = (b_ref[...] * jax.nn.gelu(zg_ref[...])).astype(BF16)

    @pl.when(t == pl.num_programs(1) - 1)
    def _():
        hout_ref[...] = h_ref[:, 7:8, :]


def _lru(z3, buf, h0, p, l, is_start, bb, tt):
    bsz, tlen, _ = z3.shape
    c = MIX_W
    vec = lambda: pl.BlockSpec((None, 1, c), lambda b, t: (l, 0, 0))
    mat = lambda: pl.BlockSpec((None, c, c), lambda b, t: (l, 0, 0))
    kern = functools.partial(_lru_kernel, is_start=is_start, bb=bb, tt=tt)
    return pl.pallas_call(
        kern,
        grid=(bsz // bb, tlen // tt),
        in_specs=[pl.BlockSpec((bb, tt, c), lambda b, t: (b, t, ZB_LX)),
                  pl.BlockSpec((bb, tt, c), lambda b, t: (b, t, ZB_LG)),
                  pl.BlockSpec((bb, CONV_W - 1, c), lambda b, t: (b, 0, 0)),
                  pl.BlockSpec((bb, 1, c), lambda b, t: (b, 0, 0)),
                  pl.BlockSpec((None, CONV_W, c), lambda b, t: (l, 0, 0)),
                  vec(), mat(), vec(), mat(), vec(), vec()],
        out_specs=[pl.BlockSpec((bb, tt, c), lambda b, t: (b, t, 0)),
                   pl.BlockSpec((bb, 1, c), lambda b, t: (b, 0, 0))],
        out_shape=[jax.ShapeDtypeStruct((bsz, tlen, c), BF16),
                   jax.ShapeDtypeStruct((bsz, 1, c), F32)],
        scratch_shapes=[pltpu.VMEM((bb, 8 + tt, c), F32), pltpu.VMEM((bb, tt, c), F32),
                        pltpu.VMEM((bb, tt, c), F32), pltpu.VMEM((bb, 8, c), F32)],
        compiler_params=_cparams(("parallel", "arbitrary")),
        name="rglru",
    )(z3, z3, buf, h0, p["lru_conv_w"], p["lru_conv_b"], p["lru_wa"], p["lru_ba"], p["lru_wx"], p["lru_bx"],
      p["lru_lambda"])


def _gmlp_kernel(zu_ref, zv_ref, lng_ref, lnb_ref, ws_ref, bias_ref, y_ref, v_ref):
    u = jax.nn.gelu(zu_ref[...])
    gv = jax.nn.gelu(zv_ref[...])
    vc = gv - jnp.mean(gv, axis=-1, keepdims=True)
    v = vc * lax.rsqrt(jnp.mean(vc * vc, axis=-1, keepdims=True) + 1e-5) * lng_ref[...] + lnb_ref[...]
    v_ref[...] = v
    gd = MIX_W // GM_GROUPS
    for g in range(GM_GROUPS):
        sl = slice(g * gd, (g + 1) * gd)
        s = _dot(ws_ref[g], v[:, sl]) + bias_ref[:, sl]
        y_ref[:, sl] = (u[:, sl] * s).astype(BF16)


def _gmlp(z2, p, l, ws_mix, bias_tile):
    m = z2.shape[0]
    c = MIX_W
    rows = GM_CHUNK
    vec = lambda: pl.BlockSpec((None, 1, c), lambda i: (l, 0, 0))
    return pl.pallas_call(
        _gmlp_kernel,
        grid=(m // rows,),
        in_specs=[pl.BlockSpec((rows, c), lambda i: (i, ZB_GU)),
                  pl.BlockSpec((rows, c), lambda i: (i, ZB_GV)),
                  vec(), vec(),
                  pl.BlockSpec((None, GM_GROUPS, rows, rows), lambda i: (l, 0, 0, 0)),
                  pl.BlockSpec((None, rows, c), lambda i: (l, 0, 0))],
        out_specs=[pl.BlockSpec((rows, c), lambda i: (i, 0)),
                   pl.BlockSpec((rows, c), lambda i: (i, 0))],
        out_shape=[jax.ShapeDtypeStruct((m, c), BF16), jax.ShapeDtypeStruct((m, c), F32)],
        compiler_params=_cparams(("parallel",)),
        name="gmlp",
    )(z2, z2, p["gmlp_ln_g"], p["gmlp_ln_b"], ws_mix, bias_tile)


def _rwkv_kernel(zr_ref, zk_ref, zv_ref, zt_ref, shr_ref, sht_ref, s0_ref,
                 mur_ref, mut_ref, w0_ref, a0_ref, kkp_ref, kap_ref, rk_ref, lng_ref, lnb_ref,
                 w2_ref, a2_ref, g2_ref, tril_ref, lvl_ref,
                 y_ref, sout_ref, prev_ref, s_ref, *, L, nlev):
    ci = pl.program_id(1)
    c = MIX_W

    @pl.when(ci == 0)
    def _():
        prev_ref[:, 0:RW_RKV] = shr_ref[0]
        prev_ref[:, RW_RKV:RW_RKV + RW_TAIL_PAD] = sht_ref[0]
        s_ref[...] = s0_ref[0]

    row = lax.broadcasted_iota(jnp.int32, (L, 1), 0)

    def shift(z, lo, hi, mu):
        zp = jnp.where(row == 0, prev_ref[:, lo:hi], pltpu.roll(z, 1, 0))
        prev_ref[:, lo:hi] = z[L - 1:L, :]
        return z + (zp - z) * mu

    r = shift(zr_ref[...], 0, c, mur_ref[:, 0:c])
    k = shift(zk_ref[...], c, 2 * c, mur_ref[:, c:2 * c])
    v = shift(zv_ref[...], 2 * c, 3 * c, mur_ref[:, 2 * c:3 * c])
    tl = shift(zt_ref[...], RW_RKV, RW_RKV + RW_TAIL_PAD, mut_ref[...])

    wlin = w0_ref[...] + _dot(jnp.tanh(tl), w2_ref[...])
    logw = -jnp.exp(-_softplus(-wlin) - 0.5)
    a = jax.nn.sigmoid(a0_ref[...] + _dot(tl, a2_ref[...]))
    g = _dot(jax.nn.sigmoid(tl), g2_ref[...])
    kkf = k * kkp_ref[...]
    kmod = k * (1.0 + (a - 1.0) * kap_ref[...])

    cum = _dot_hi(tril_ref[...], logw)
    c_last = cum[L - 1:L, :]
    e_c = jnp.exp(cum)
    e_cm1 = jnp.exp(cum - logw)
    e_mc = jnp.exp(-cum)
    e_rest = jnp.exp(c_last - cum)
    e_last = jnp.exp(c_last)

    ri = lax.broadcasted_iota(jnp.int32, (L, L), 0)
    cj = lax.broadcasted_iota(jnp.int32, (L, L), 1)
    strict = cj < ri
    incl = cj <= ri
    eye = (cj == ri).astype(F32)

    for h in range(RW_H):
        sl = slice(h * RW_HD, (h + 1) * RW_HD)
        kk = kkf[:, sl]
        kk = kk / jnp.maximum(jnp.sqrt(jnp.sum(kk * kk, axis=-1, keepdims=True)), 1e-12)
        beta = kk * a[:, sl]
        r_h, k_h, v_h = r[:, sl], kmod[:, sl], v[:, sl]
        a_t = -kk * e_cm1[:, sl]
        r_t = r_h * e_c[:, sl]
        b_t = beta * e_mc[:, sl]
        k_t = k_h * e_mc[:, sl]
        b_hat = beta * e_rest[:, sl]
        k_hat = k_h * e_rest[:, sl]

        m_ba = jnp.where(strict, _dot_nt(a_t, b_t), 0.0)
        m_ka = jnp.where(strict, _dot_nt(a_t, k_t), 0.0)
        m_br = jnp.where(incl, _dot_nt(r_t, b_t), 0.0)
        m_kr = jnp.where(incl, _dot_nt(r_t, k_t), 0.0)

        inv = eye
        for lv in range(nlev):
            inv = inv + _dot_hi(_dot_hi(inv, m_ba * lvl_ref[lv]), inv)

        s0 = s_ref[h]
        u = _dot_hi(inv, _dot_nt(a_t, s0) + _dot(m_ka, v_h))
        o = _dot_nt(r_t, s0) + _dot(m_br, u) + _dot(m_kr, v_h)
        s_ref[h] = s0 * e_last[:, sl] + _dot_tn(u, b_hat) + _dot_tn(v_h, k_hat)

        oc = o - jnp.mean(o, axis=-1, keepdims=True)
        on = oc * lax.rsqrt(jnp.mean(oc * oc, axis=-1, keepdims=True) + RW_LN_EPS) * lng_ref[:, sl] + lnb_ref[:, sl]
        bonus = jnp.sum(r_h * k_h * rk_ref[:, sl], axis=-1, keepdims=True) * v_h
        y_ref[:, sl] = ((on + bonus) * g[:, sl]).astype(BF16)

    @pl.when(ci == pl.num_programs(1) - 1)
    def _():
        sout_ref[0] = s_ref[...]


def _level_masks(L):
    idx = np.arange(L)
    masks = []
    s = 1
    while s < L:
        same2 = (idx[:, None] // (2 * s)) == (idx[None, :] // (2 * s))
        diff1 = (idx[:, None] // s) != (idx[None, :] // s)
        lower = idx[:, None] > idx[None, :]
        masks.append((same2 & diff1 & lower).astype(np.float32))
        s *= 2
    return np.stack(masks)


def _rwkv(z2, sh_rkv, sh_tail, s0, p, l, bsz, tlen, L):
    c = MIX_W
    nc = tlen // L
    lvl = jnp.asarray(_level_masks(L))
    nlev = lvl.shape[0]
    tril = jnp.asarray(np.tril(np.ones((L, L), np.float32)))
    vec = lambda w=c: pl.BlockSpec((None, 1, w), lambda b, i: (l, 0, 0))
    lora = lambda: pl.BlockSpec((None, RW_TAIL_PAD, c), lambda b, i: (l, 0, 0))
    zspec = lambda blk: pl.BlockSpec((L, c), lambda b, i, blk=blk: (b * nc + i, blk))
    kern = functools.partial(_rwkv_kernel, L=L, nlev=nlev)
    return pl.pallas_call(
        kern,
        grid=(bsz, nc),
        in_specs=[zspec(ZB_R), zspec(ZB_K), zspec(ZB_V),
                  pl.BlockSpec((L, RW_TAIL_PAD), lambda b, i: (b * nc + i, Z_TAIL // RW_TAIL_PAD)),
                  pl.BlockSpec((1, 1, RW_RKV), lambda b, i: (b, 0, 0)),
                  pl.BlockSpec((1, 1, RW_TAIL_PAD), lambda b, i: (b, 0, 0)),
                  pl.BlockSpec((1, RW_H, RW_HD, RW_HD), lambda b, i: (b, 0, 0, 0)),
                  vec(RW_RKV), vec(RW_TAIL_PAD), vec(), vec(), vec(), vec(), vec(), vec(), vec(),
                  lora(), lora(), lora(),
                  pl.BlockSpec((L, L), lambda b, i: (0, 0)),
                  pl.BlockSpec((nlev, L, L), lambda b, i: (0, 0, 0))],
        out_specs=[pl.BlockSpec((L, c), lambda b, i: (b * nc + i, 0)),
                   pl.BlockSpec((1, RW_H, RW_HD, RW_HD), lambda b, i: (b, 0, 0, 0))],
        out_shape=[jax.ShapeDtypeStruct((bsz * tlen, c), BF16),
                   jax.ShapeDtypeStruct((bsz, RW_H, RW_HD, RW_HD), F32)],
        scratch_shapes=[pltpu.VMEM((1, RW_RKV + RW_TAIL_PAD), F32), pltpu.VMEM((RW_H, RW_HD, RW_HD), F32)],
        compiler_params=_cparams(("parallel", "arbitrary")),
        name="rwkv7",
    )(z2, z2, z2, z2, sh_rkv, sh_tail, s0,
      p["rwkv_mu_rkv"], p["rwkv_mu_tail"], p["rwkv_w0"], p["rwkv_a0"], p["rwkv_kk"], p["rwkv_ka"], p["rwkv_rk"],
      p["rwkv_ln_g"], p["rwkv_ln_b"], p["rwkv_w2"], p["rwkv_a2"], p["rwkv_g2"], tril, lvl)


def _mlstm_kernel(zx_ref, zv_ref, zo_ref, zif_ref, zift_ref, buf_ref, c0_ref, n0_ref, m0_ref,
                  cw_ref, cb_ref, wq_ref, wk_ref, brow_ref, bcol_ref, ng_ref, skip_ref, tril_ref, triu_ref,
                  y_ref, cout_ref, nout_ref, mout_ref, xs_ref, c_ref, n_ref, m_ref, *, L):
    ci = pl.program_id(1)

    @pl.when(ci == 0)
    def _():
        xs_ref[5:8, :] = buf_ref[0]
        c_ref[...] = c0_ref[0]
        n_ref[...] = n0_ref[0]
        m_ref[...] = m0_ref[0]

    @pl.when(ci > 0)
    def _():
        xs_ref[5:8, :] = xs_ref[L + 5:L + 8, :]

    xs_ref[8:8 + L, :] = zx_ref[...]
    conv = cb_ref[...] + cw_ref[0:1, :] * xs_ref[5:5 + L, :]
    for j in range(1, CONV_W):
        conv = conv + cw_ref[j:j + 1, :] * xs_ref[5 + j:5 + j + L, :]
    cc = jax.nn.silu(conv)

    gate_col = zif_ref[...] + brow_ref[...]
    bcum_col = _dot_hi(tril_ref[...], _log_sigmoid(gate_col))
    gate_row = zift_ref[...] + bcol_ref[...]
    bcum_row = _dot_hi(_log_sigmoid(gate_row), triu_ref[...])

    ri = lax.broadcasted_iota(jnp.int32, (L, L), 0)
    cj = lax.broadcasted_iota(jnp.int32, (L, L), 1)
    causal = cj <= ri

    for h in range(ML_H):
        sl = slice(h * ML_HD, (h + 1) * ML_HD)
        ch = cc[:, sl]
        q = _dot(ch, wq_ref[h])
        k = _dot(ch, wk_ref[h]) * (ML_HD ** -0.5)
        v = zv_ref[:, sl]
        b_col = bcum_col[:, ML_H + h:ML_H + h + 1]
        li_col = gate_col[:, h:h + 1]
        b_row = bcum_row[ML_H + h:ML_H + h + 1, :]
        li_row = gate_row[h:h + 1, :]
        m_prev = m_ref[h:h + 1, 0:1]

        log_d = jnp.where(causal, b_col - b_row + li_row, -jnp.inf)
        inter = b_col + m_prev
        m_t = jnp.maximum(jnp.max(log_d, axis=-1, keepdims=True), inter)
        s = _dot_nt(q, k) * jnp.exp(log_d - m_t)
        sc = jnp.exp(inter - m_t)
        cmat = c_ref[h]
        nvec = n_ref[h:h + 1, :]
        num = _dot(s, v) + sc * _dot_nt(q, cmat)
        den = jnp.sum(s, axis=-1, keepdims=True) + sc * jnp.sum(q * nvec, axis=-1, keepdims=True)
        hh = num / jnp.maximum(jnp.abs(den), jnp.exp(-m_t))

        m_new = m_t[L - 1:L, :]
        b_last = b_col[L - 1:L, :]
        wj = jnp.exp(b_last - b_col + li_col - m_new)
        dec = jnp.exp(b_last + m_prev - m_new)
        c_ref[h] = dec * cmat + _dot_tn(wj * v, k)
        n_ref[h:h + 1, :] = dec * nvec + jnp.sum(wj * k, axis=0, keepdims=True)
        m_ref[h:h + 1, :] = jnp.broadcast_to(m_new, (1, ML_HD))

        hc = hh - jnp.mean(hh, axis=-1, keepdims=True)
        hn = hc * lax.rsqrt(jnp.mean(hc * hc, axis=-1, keepdims=True) + 1e-6) * ng_ref[:, sl]
        y_ref[:, sl] = (jax.nn.sigmoid(zo_ref[:, sl]) * (hn + skip_ref[:, sl] * ch)).astype(BF16)

    @pl.when(ci == pl.num_programs(1) - 1)
    def _():
        cout_ref[0] = c_ref[...]
        nout_ref[0] = n_ref[...]
        mout_ref[0] = m_ref[...]


def _mlstm(z2, zift, buf, c0, n0, m0, p, l, bsz, tlen, L):
    c = MIX_W
    nc = tlen // L
    tril = jnp.asarray(np.tril(np.ones((L, L), np.float32)))
    triu = jnp.asarray(np.triu(np.ones((L, L), np.float32)))
    bcol = jnp.broadcast_to(p["mlstm_bif"][l][:, None], (8, L))
    vec = lambda: pl.BlockSpec((None, 1, c), lambda b, i: (l, 0, 0))
    zspec = lambda blk: pl.BlockSpec((L, c), lambda b, i, blk=blk: (b * nc + i, blk))
    hmat = lambda: pl.BlockSpec((None, ML_H, ML_HD, ML_HD), lambda b, i: (l, 0, 0, 0))
    sq = lambda: pl.BlockSpec((L, L), lambda b, i: (0, 0))
    kern = functools.partial(_mlstm_kernel, L=L)
    return pl.pallas_call(
        kern,
        grid=(bsz, nc),
        in_specs=[zspec(ZB_MX), zspec(ZB_MV), zspec(ZB_MO),
                  pl.BlockSpec((L, 128), lambda b, i: (b * nc + i, Z_MIF // 128)),
                  pl.BlockSpec((None, 8, L), lambda b, i: (b * nc + i, 0, 0)),
                  pl.BlockSpec((1, CONV_W - 1, c), lambda b, i: (b, 0, 0)),
                  pl.BlockSpec((1, ML_H, ML_HD, ML_HD), lambda b, i: (b, 0, 0, 0)),
                  pl.BlockSpec((1, ML_H, ML_HD), lambda b, i: (b, 0, 0)),
                  pl.BlockSpec((1, ML_H, ML_HD), lambda b, i: (b, 0, 0)),
                  pl.BlockSpec((None, CONV_W, c), lambda b, i: (l, 0, 0)),
                  vec(), hmat(), hmat(),
                  pl.BlockSpec((None, 1, 128), lambda b, i: (l, 0, 0)),
                  pl.BlockSpec((8, L), lambda b, i: (0, 0)),
                  vec(), vec(), sq(), sq()],
        out_specs=[pl.BlockSpec((L, c), lambda b, i: (b * nc + i, 0)),
                   pl.BlockSpec((1, ML_H, ML_HD, ML_HD), lambda b, i: (b, 0, 0, 0)),
                   pl.BlockSpec((1, ML_H, ML_HD), lambda b, i: (b, 0, 0)),
                   pl.BlockSpec((1, ML_H, ML_HD), lambda b, i: (b, 0, 0))],
        out_shape=[jax.ShapeDtypeStruct((bsz * tlen, c), BF16),
                   jax.ShapeDtypeStruct((bsz, ML_H, ML_HD, ML_HD), F32),
                   jax.ShapeDtypeStruct((bsz, ML_H, ML_HD), F32),
                   jax.ShapeDtypeStruct((bsz, ML_H, ML_HD), F32)],
        scratch_shapes=[pltpu.VMEM((8 + L, c), F32), pltpu.VMEM((ML_H, ML_HD, ML_HD), F32),
                        pltpu.VMEM((ML_H, ML_HD), F32), pltpu.VMEM((ML_H, ML_HD), F32)],
        compiler_params=_cparams(("parallel", "arbitrary")),
        name="mlstm",
    )(z2, z2, z2, z2, zift, buf, c0, n0, m0,
      p["mlstm_conv_w"], p["mlstm_conv_b"], p["mlstm_wq"], p["mlstm_wk"], p["mlstm_brow"], bcol,
      p["mlstm_norm_g"], p["mlstm_skip"], tril, triu)


def _block_diag(w):
    dp, nb, d, _ = w.shape
    eye = jnp.eye(nb, dtype=w.dtype)
    return jnp.einsum("lnij,nm->lnimj", w, eye).reshape(dp, nb * d, nb * d)


def _prepare(raw):
    p = {}
    w_in = raw["w_in"]
    zeros = lambda n: jnp.zeros(w_in.shape[:2] + (n,), w_in.dtype)
    p["w_mix"] = jnp.concatenate(
        [w_in[..., 0:3584], w_in[..., 3744:5280], w_in[..., 3584:3744], zeros(RW_TAIL_PAD - RW_TAIL),
         w_in[..., 5280:5288], zeros(Z_COLS - Z_MIF - 8)], axis=-1).astype(BF16)
    p["w_gate"] = w_in[..., P_SRC:].astype(BF16)
    for name in ("w_branch", "w_out", "w_ff_in", "w_ff_out"):
        p[name] = raw[name].astype(BF16)
    row = lambda a: a.reshape(DEPTH, 1, -1)
    for name in ("norm_pre_mix", "norm_post_mix", "norm_pre_ffn", "norm_post_ffn",
                 "lru_conv_b", "lru_ba", "lru_bx", "lru_lambda", "gmlp_ln_g", "gmlp_ln_b",
                 "rwkv_w0", "rwkv_a0", "rwkv_kk", "rwkv_ka", "rwkv_rk", "rwkv_ln_g", "rwkv_ln_b",
                 "mlstm_conv_b", "mlstm_norm_g", "mlstm_skip"):
        p[name] = row(raw[name])
    p["lru_conv_w"] = raw["lru_conv_w"]
    p["mlstm_conv_w"] = raw["mlstm_conv_w"]
    p["lru_wa"] = _block_diag(raw["lru_wa"]).astype(BF16)
    p["lru_wx"] = _block_diag(raw["lru_wx"]).astype(BF16)
    mu = raw["rwkv_mu"]
    p["rwkv_mu_rkv"] = row(mu[:, :RW_RKV])
    p["rwkv_mu_tail"] = row(jnp.pad(mu[:, RW_RKV:], ((0, 0), (0, RW_TAIL_PAD - RW_TAIL))))

    def lora(w, lo):
        return jnp.pad(w, ((0, 0), (lo, RW_TAIL_PAD - lo - w.shape[1]), (0, 0))).astype(BF16)

    p["rwkv_w2"] = lora(raw["rwkv_w2"], 0)
    p["rwkv_a2"] = lora(raw["rwkv_a2"], 32)
    p["rwkv_g2"] = lora(raw["rwkv_g2"], 64)
    p["mlstm_wq"] = raw["mlstm_wq"].astype(BF16)
    p["mlstm_wk"] = raw["mlstm_wk"].astype(BF16)
    bif = jnp.concatenate([raw["mlstm_bi"], raw["mlstm_bf"]], axis=-1)
    p["mlstm_bif"] = bif
    p["mlstm_brow"] = row(jnp.pad(bif, ((0, 0), (0, 128 - 8))))
    p["gmlp_ws"] = raw["gmlp_ws"]
    p["gmlp_bs"] = raw["gmlp_bs"]
    return p


def _gmlp_mix_weights(p, tlen):
    L = min(GM_CHUNK, tlen)
    rep = GM_CHUNK // L
    ws = jnp.tril(p["gmlp_ws"][:, :, :L, :L])
    eye = jnp.eye(rep, dtype=ws.dtype)
    ws_mix = jnp.einsum("lgps,ab->lgapbs", ws, eye).reshape(DEPTH, GM_GROUPS, GM_CHUNK, GM_CHUNK)
    bias = jnp.swapaxes(p["gmlp_bs"][:, :, :L], 1, 2)
    bias = jnp.repeat(bias, MIX_W // GM_GROUPS, axis=2)
    bias = jnp.tile(bias, (1, rep, 1))
    return ws_mix, bias


def _group_forward(x3, states, p, is_start, depth=DEPTH):
    bsz, tlen, _ = x3.shape
    m = bsz * tlen
    x = x3.reshape(m, D_MODEL)
    lru_buf, lru_h, rw_shift, rw_s, ml_buf, ml_c, ml_n, ml_m = states
    tm = 512
    if is_start:
        lru_bb, lru_tt, rw_l, ml_l = 1, 512, RW_CHUNK, ML_CHUNK_P
    else:
        lru_bb, lru_tt, rw_l, ml_l = 16, tlen, tlen, tlen
    ws_mix, gm_bias = _gmlp_mix_weights(p, tlen)
    new_states = [[] for _ in range(8)]
    gm_vs = []
    for l in range(depth):
        z, hn = _inproj(x, p["norm_pre_mix"], p["w_mix"], l, tm)
        z3 = z.reshape(bsz, tlen, Z_COLS)

        y_a, h_new = _lru(z3, lru_buf[l], lru_h[l].reshape(bsz, 1, MIX_W), p, l, is_start, lru_bb, lru_tt)
        y_b, gm_v = _gmlp(z, p, l, ws_mix, gm_bias)
        sh = rw_shift[l]
        sh_rkv = sh[:, :RW_RKV].reshape(bsz, 1, RW_RKV)
        sh_tail = jnp.pad(sh[:, RW_RKV:], ((0, 0), (0, RW_TAIL_PAD - RW_TAIL))).reshape(bsz, 1, RW_TAIL_PAD)
        y_c, s_new = _rwkv(z, sh_rkv, sh_tail, rw_s[l], p, l, bsz, tlen, rw_l)
        zift = jnp.swapaxes(z[:, Z_MIF:Z_MIF + 8].reshape(m // ml_l, ml_l, 8), 1, 2)
        m0 = jnp.broadcast_to(ml_m[l][:, :, None], (bsz, ML_H, ML_HD))
        y_d, c_new, n_new, m_new = _mlstm(z, zift, ml_buf[l], ml_c[l], ml_n[l], m0, p, l, bsz, tlen, ml_l)

        merged = _merge(hn, (y_a.reshape(m, MIX_W), y_b, y_c, y_d), p["w_gate"], p["w_branch"], l, tm)
        x1, hf = _outproj(merged, p["w_out"], x, p["norm_post_mix"], p["norm_pre_ffn"], l, 256)
        hmid = _ffup(hf, p["w_ff_in"], l, tm)
        x = _ffdown(hmid, p["w_ff_out"], x1, p["norm_post_ffn"], l, tm)

        new_states[0].append(z3[:, tlen - (CONV_W - 1):, 0:MIX_W])
        new_states[1].append(h_new.reshape(bsz, MIX_W))
        new_states[2].append(jnp.concatenate(
            [z3[:, tlen - 1, ZB_R * MIX_W:ZB_R * MIX_W + RW_RKV], z3[:, tlen - 1, Z_TAIL:Z_TAIL + RW_TAIL]], axis=-1))
        new_states[3].append(s_new)
        new_states[4].append(z3[:, tlen - (CONV_W - 1):, ZB_MX * MIX_W:(ZB_MX + 1) * MIX_W])
        new_states[5].append(c_new)
        new_states[6].append(n_new)
        new_states[7].append(m_new[:, :, 0])
        gm_vs.append(gm_v.reshape(bsz, tlen, MIX_W))
    return x.reshape(bsz, tlen, D_MODEL), [jnp.stack(s, axis=0) for s in new_states], jnp.stack(gm_vs, axis=0)


def kernel(x_prompt, x_sample, state_lru_conv, state_lru_h, state_rwkv_shift, state_rwkv_wkv, state_mlstm_conv, state_mlstm_C, state_mlstm_n, state_mlstm_m, norm_pre_mix, norm_post_mix, norm_pre_ffn, norm_post_ffn, w_in, lru_conv_w, lru_conv_b, lru_wa, lru_ba, lru_wx, lru_bx, lru_lambda, gmlp_ln_g, gmlp_ln_b, gmlp_ws, gmlp_bs, rwkv_mu, rwkv_w0, rwkv_w2, rwkv_a0, rwkv_a2, rwkv_g2, rwkv_kk, rwkv_ka, rwkv_rk, rwkv_ln_g, rwkv_ln_b, mlstm_conv_w, mlstm_conv_b, mlstm_wq, mlstm_wk, mlstm_bi, mlstm_bf, mlstm_norm_g, mlstm_skip, w_branch, w_out, w_ff_in, w_ff_out):
    raw = dict(norm_pre_mix=norm_pre_mix, norm_post_mix=norm_post_mix, norm_pre_ffn=norm_pre_ffn,
               norm_post_ffn=norm_post_ffn, w_in=w_in, lru_conv_w=lru_conv_w, lru_conv_b=lru_conv_b,
               lru_wa=lru_wa, lru_ba=lru_ba, lru_wx=lru_wx, lru_bx=lru_bx, lru_lambda=lru_lambda,
               gmlp_ln_g=gmlp_ln_g, gmlp_ln_b=gmlp_ln_b, gmlp_ws=gmlp_ws, gmlp_bs=gmlp_bs,
               rwkv_mu=rwkv_mu, rwkv_w0=rwkv_w0, rwkv_w2=rwkv_w2, rwkv_a0=rwkv_a0, rwkv_a2=rwkv_a2,
               rwkv_g2=rwkv_g2, rwkv_kk=rwkv_kk, rwkv_ka=rwkv_ka, rwkv_rk=rwkv_rk, rwkv_ln_g=rwkv_ln_g,
               rwkv_ln_b=rwkv_ln_b, mlstm_conv_w=mlstm_conv_w, mlstm_conv_b=mlstm_conv_b, mlstm_wq=mlstm_wq,
               mlstm_wk=mlstm_wk, mlstm_bi=mlstm_bi, mlstm_bf=mlstm_bf, mlstm_norm_g=mlstm_norm_g,
               mlstm_skip=mlstm_skip, w_branch=w_branch, w_out=w_out, w_ff_in=w_ff_in, w_ff_out=w_ff_out)
    p = _prepare(raw)
    bp = x_prompt.shape[0]
    zero = lambda *s: jnp.zeros((DEPTH, bp) + s, F32)
    prompt_states = (zero(CONV_W - 1, MIX_W), zero(MIX_W), zero(RW_RKV + RW_TAIL), zero(RW_H, RW_HD, RW_HD),
                     zero(CONV_W - 1, MIX_W), zero(ML_H, ML_HD, ML_HD), zero(ML_H, ML_HD), zero(ML_H))
    sample_states = (state_lru_conv, state_lru_h, state_rwkv_shift, state_rwkv_wkv,
                     state_mlstm_conv, state_mlstm_C, state_mlstm_n, state_mlstm_m)
    yp, st_p, _ = _group_forward(x_prompt, prompt_states, p, True)
    ys, st_s, gm_v = _group_forward(x_sample, sample_states, p, False)
    return (yp, ys, *st_p, *st_s, gm_v)
```

```python
import functools

import numpy as np
import jax
import jax.numpy as jnp
from jax import lax
from jax.experimental import pallas as pl
from jax.experimental.pallas import tpu as pltpu

F32 = jnp.float32
BF16 = jnp.bfloat16

D_MODEL = 2048
DEPTH = 4
MIX_W = 512
CONV_W = 4
LRU_BLOCKS = 8
LRU_C = 8.0
GM_CHUNK = 128
GM_GROUPS = 4
RW_HD = 64
RW_H = 8
RW_RKV = 3 * MIX_W
RW_TAIL = 160
RW_TAIL_PAD = 256
RW_LN_EPS = 64e-5
ML_H = 4
ML_HD = 128
D_FF = 5632
N_GATE = 4 * D_MODEL
P_SRC = 5288
Z_COLS = 5632

ZB_LX, ZB_LG, ZB_GU, ZB_GV, ZB_R, ZB_K, ZB_V, ZB_MX, ZB_MV, ZB_MO = range(10)
Z_TAIL = 5120
Z_MIF = 5376

RW_CHUNK = 64
ML_CHUNK_P = 128
VMEM_LIMIT = 56 * 1024 * 1024
HI = lax.Precision.HIGHEST


def _cparams(sem):
    return pltpu.CompilerParams(dimension_semantics=sem, vmem_limit_bytes=VMEM_LIMIT)


def _softplus(x):
    return jnp.maximum(x, 0.0) + jnp.log1p(jnp.exp(-jnp.abs(x)))


def _log_sigmoid(x):
    return -_softplus(-x)


def _rms(x, g):
    return x * lax.rsqrt(jnp.mean(x * x, axis=-1, keepdims=True) + 1e-6) * g


def _dot(a, b):
    return jnp.dot(a.astype(BF16), b.astype(BF16), preferred_element_type=F32)


def _dot_nt(a, b):
    return lax.dot_general(a.astype(BF16), b.astype(BF16), (((1,), (1,)), ((), ())),
                           preferred_element_type=F32)


def _dot_tn(a, b):
    return lax.dot_general(a.astype(BF16), b.astype(BF16), (((0,), (0,)), ((), ())),
                           preferred_element_type=F32)


def _dot_hi(a, b):
    return jnp.dot(a, b, preferred_element_type=F32, precision=HI)


def _inproj_kernel(x_ref, g_ref, w_ref, z_ref, hn_ref):
    @pl.when(pl.program_id(1) == 0)
    def _():
        hn_ref[...] = _rms(x_ref[...], g_ref[...]).astype(BF16)

    z_ref[...] = jnp.dot(hn_ref[...], w_ref[...], preferred_element_type=F32)


def _inproj(x, g, w, l, tm, tn=512):
    m = x.shape[0]
    return pl.pallas_call(
        _inproj_kernel,
        grid=(m // tm, Z_COLS // tn),
        in_specs=[pl.BlockSpec((tm, D_MODEL), lambda i, j: (i, 0)),
                  pl.BlockSpec((None, 1, D_MODEL), lambda i, j: (l, 0, 0)),
                  pl.BlockSpec((None, D_MODEL, tn), lambda i, j: (l, 0, j))],
        out_specs=[pl.BlockSpec((tm, tn), lambda i, j: (i, j)),
                   pl.BlockSpec((tm, D_MODEL), lambda i, j: (i, 0))],
        out_shape=[jax.ShapeDtypeStruct((m, Z_COLS), F32),
                   jax.ShapeDtypeStruct((m, D_MODEL), BF16)],
        compiler_params=_cparams(("parallel", "arbitrary")),
        name="inproj",
    )(x, g, w)


def _merge_kernel(hn_ref, ya_ref, yb_ref, yc_ref, yd_ref, g0_ref, g1_ref, g2_ref, g3_ref, wb_ref, o_ref):
    hn = hn_ref[...]
    acc = None
    for b, (y_ref, wg_ref) in enumerate(((ya_ref, g0_ref), (yb_ref, g1_ref), (yc_ref, g2_ref), (yd_ref, g3_ref))):
        zg = jnp.dot(hn, wg_ref[...], preferred_element_type=F32)
        br = jnp.dot(y_ref[...], wb_ref[b], preferred_element_type=F32)
        term = jax.nn.sigmoid(zg) * br
        acc = term if acc is None else acc + term
    o_ref[...] = acc.astype(BF16)


def _merge(hn, ys, wgate, wbranch, l, tm, tn=256):
    m = hn.shape[0]
    nb = D_MODEL // tn
    y_spec = pl.BlockSpec((tm, MIX_W), lambda j, i: (i, 0))
    g_specs = [pl.BlockSpec((None, D_MODEL, tn), lambda j, i, b=b: (l, 0, b * nb + j)) for b in range(4)]
    return pl.pallas_call(
        _merge_kernel,
        grid=(nb, m // tm),
        in_specs=[pl.BlockSpec((tm, D_MODEL), lambda j, i: (i, 0)), y_spec, y_spec, y_spec, y_spec,
                  *g_specs,
                  pl.BlockSpec((None, 4, MIX_W, tn), lambda j, i: (l, 0, 0, j))],
        out_specs=pl.BlockSpec((tm, tn), lambda j, i: (i, j)),
        out_shape=jax.ShapeDtypeStruct((m, D_MODEL), BF16),
        compiler_params=_cparams(("parallel", "arbitrary")),
        name="merge",
    )(hn, *ys, wgate, wgate, wgate, wgate, wbranch)


def _outproj_kernel(mg_ref, w_ref, x_ref, gpost_ref, gpre_ref, x1_ref, hf_ref):
    mix = jnp.dot(mg_ref[...], w_ref[...], preferred_element_type=F32)
    x1 = x_ref[...] + _rms(mix, gpost_ref[...])
    x1_ref[...] = x1
    hf_ref[...] = _rms(x1, gpre_ref[...]).astype(BF16)


def _outproj(merged, w_out, x, g_post, g_pre_ffn, l, tm):
    m = x.shape[0]
    g_spec = pl.BlockSpec((None, 1, D_MODEL), lambda i: (l, 0, 0))
    row_spec = pl.BlockSpec((tm, D_MODEL), lambda i: (i, 0))
    return pl.pallas_call(
        _outproj_kernel,
        grid=(m // tm,),
        in_specs=[row_spec, pl.BlockSpec((None, D_MODEL, D_MODEL), lambda i: (l, 0, 0)), row_spec, g_spec, g_spec],
        out_specs=[row_spec, row_spec],
        out_shape=[jax.ShapeDtypeStruct((m, D_MODEL), F32), jax.ShapeDtypeStruct((m, D_MODEL), BF16)],
        compiler_params=_cparams(("parallel",)),
        name="outproj",
    )(merged, w_out, x, g_post, g_pre_ffn)


def _ffup_kernel(hf_ref, wg_ref, wu_ref, o_ref):
    hf = hf_ref[...]
    g = jnp.dot(hf, wg_ref[...], preferred_element_type=F32)
    u = jnp.dot(hf, wu_ref[...], preferred_element_type=F32)
    o_ref[...] = (jax.nn.silu(g) * u).astype(BF16)


def _ffup(hf, w_ff_in, l, tm, tn=512):
    m = hf.shape[0]
    nb = D_FF // tn
    return pl.pallas_call(
        _ffup_kernel,
        grid=(nb, m // tm),
        in_specs=[pl.BlockSpec((tm, D_MODEL), lambda j, i: (i, 0)),
                  pl.BlockSpec((None, D_MODEL, tn), lambda j, i: (l, 0, j)),
                  pl.BlockSpec((None, D_MODEL, tn), lambda j, i: (l, 0, nb + j))],
        out_specs=pl.BlockSpec((tm, tn), lambda j, i: (i, j)),
        out_shape=jax.ShapeDtypeStruct((m, D_FF), BF16),
        compiler_params=_cparams(("parallel", "arbitrary")),
        name="ffup",
    )(hf, w_ff_in, w_ff_in)


def _ffdown_kernel(h_ref, w_ref, x1_ref, g_ref, o_ref, acc_ref):
    k = pl.program_id(1)

    @pl.when(k == 0)
    def _():
        acc_ref[...] = jnp.zeros_like(acc_ref)

    acc_ref[...] += jnp.dot(h_ref[...], w_ref[...], preferred_element_type=F32)

    @pl.when(k == pl.num_programs(1) - 1)
    def _():
        o_ref[...] = x1_ref[...] + _rms(acc_ref[...], g_ref[...])


def _ffdown(h, w_ff_out, x1, g_post, l, tm, tk=512):
    m = x1.shape[0]
    row_spec = pl.BlockSpec((tm, D_MODEL), lambda i, k: (i, 0))
    return pl.pallas_call(
        _ffdown_kernel,
        grid=(m // tm, D_FF // tk),
        in_specs=[pl.BlockSpec((tm, tk), lambda i, k: (i, k)),
                  pl.BlockSpec((None, tk, D_MODEL), lambda i, k: (l, k, 0)),
                  row_spec,
                  pl.BlockSpec((None, 1, D_MODEL), lambda i, k: (l, 0, 0))],
        out_specs=row_spec,
        out_shape=jax.ShapeDtypeStruct((m, D_MODEL), F32),
        scratch_shapes=[pltpu.VMEM((tm, D_MODEL), F32)],
        compiler_params=_cparams(("parallel", "arbitrary")),
        name="ffdown",
    )(h, w_ff_out, x1, g_post)


def _lru_kernel(zx_ref, zg_ref, buf_ref, h0_ref, cw_ref, cb_ref, wa_ref, ba_ref, wx_ref, bx_ref, lam_ref,
                y_ref, hout_ref, xs_ref, a_ref, b_ref, h_ref, *, is_start, bb, tt):
    t = pl.program_id(1)
    c = MIX_W

    @pl.when(t == 0)
    def _():
        xs_ref[:, 5:8, :] = buf_ref[...]
        h_ref[...] = jnp.broadcast_to(h0_ref[...], (bb, 8, c))

    @pl.when(t > 0)
    def _():
        xs_ref[:, 5:8, :] = xs_ref[:, tt + 5:tt + 8, :]

    xs_ref[:, 8:8 + tt, :] = zx_ref[...]
    xc = cb_ref[...] + cw_ref[0:1, :] * xs_ref[:, 5:5 + tt, :]
    for j in range(1, CONV_W):
        xc = xc + cw_ref[j:j + 1, :] * xs_ref[:, 5 + j:5 + j + tt, :]
    xc2 = xc.reshape(bb * tt, c)
    r = jax.nn.sigmoid(_dot(xc2, wa_ref[...]) + ba_ref[...])
    i = jax.nn.sigmoid(_dot(xc2, wx_ref[...]) + bx_ref[...])
    log_a = LRU_C * r * _log_sigmoid(lam_ref[...])
    a = jnp.exp(log_a)
    mult = jnp.sqrt(1.0 - jnp.exp(2.0 * log_a))
    if is_start:
        tpos = lax.broadcasted_iota(jnp.int32, (bb, tt, c), 1).reshape(bb * tt, c) + t * tt
        mult = jnp.where(tpos == 0, 1.0, mult)
    a_ref[...] = a.reshape(bb, tt, c)
    b_ref[...] = (mult * i * xc2).reshape(bb, tt, c)

    row = lax.broadcasted_iota(jnp.int32, (bb, 8, c), 1).reshape(bb * 8, c)

    def group(gi, carry):
        off = pl.multiple_of(gi * 8, 8)
        av = a_ref[:, pl.ds(off, 8), :].reshape(bb * 8, c)
        bv = b_ref[:, pl.ds(off, 8), :].reshape(bb * 8, c)
        for s in (1, 2, 4):
            keep = row >= s
            a_sh = pltpu.roll(av, s, 0)
            b_sh = pltpu.roll(bv, s, 0)
            bv = jnp.where(keep, av * b_sh + bv, bv)
            av = jnp.where(keep, av * a_sh, av)
        hh = (av * h_ref[...].reshape(bb * 8, c) + bv).reshape(bb, 8, c)
        b_ref[:, pl.ds(off, 8), :] = hh
        h_ref[...] = jnp.broadcast_to(hh[:, 7:8, :], (bb, 8, c))
        return carry

    lax.fori_loop(0, tt // 8, group, 0)
    y_ref[...] = (b_ref[...] * jax.nn.gelu(zg_ref[...])).astype(BF16)

    @pl.when(t == pl.num_programs(1) - 1)
    def _():
        hout_ref[...] = h_ref[:, 7:8, :]


def _lru(z3, buf, h_all, p, l, is_start, bb, tt):
    bsz, tlen, _ = z3.shape
    c = MIX_W
    vec = lambda: pl.BlockSpec((None, 1, c), lambda b, t: (l, 0, 0))
    mat = lambda: pl.BlockSpec((None, c, c), lambda b, t: (l, 0, 0))
    hspec = lambda: pl.BlockSpec((None, bb, 1, c), lambda b, t: (l, b, 0, 0))
    kern = functools.partial(_lru_kernel, is_start=is_start, bb=bb, tt=tt)
    return pl.pallas_call(
        kern,
        grid=(bsz // bb, tlen // tt),
        in_specs=[pl.BlockSpec((bb, tt, c), lambda b, t: (b, t, ZB_LX)),
                  pl.BlockSpec((bb, tt, c), lambda b, t: (b, t, ZB_LG)),
                  pl.BlockSpec((None, bb, CONV_W - 1, c), lambda b, t: (l, b, 0, 0)),
                  hspec(),
                  pl.BlockSpec((None, CONV_W, c), lambda b, t: (l, 0, 0)),
                  vec(), mat(), vec(), mat(), vec(), vec()],
        out_specs=[pl.BlockSpec((bb, tt, c), lambda b, t: (b, t, 0)), hspec()],
        out_shape=[jax.ShapeDtypeStruct((bsz, tlen, c), BF16),
                   jax.ShapeDtypeStruct(h_all.shape, F32)],
        input_output_aliases={3: 1},
        scratch_shapes=[pltpu.VMEM((bb, 8 + tt, c), F32), pltpu.VMEM((bb, tt, c), F32),
                        pltpu.VMEM((bb, tt, c), F32), pltpu.VMEM((bb, 8, c), F32)],
        compiler_params=_cparams(("parallel", "arbitrary")),
        name="rglru",
    )(z3, z3, buf, h_all, p["lru_conv_w"], p["lru_conv_b"], p["lru_wa"], p["lru_ba"], p["lru_wx"], p["lru_bx"],
      p["lru_lambda"])


def _gmlp_kernel(zu_ref, zv_ref, lng_ref, lnb_ref, ws_ref, bias_ref, y_ref, v_ref):
    u = jax.nn.gelu(zu_ref[...])
    gv = jax.nn.gelu(zv_ref[...])
    vc = gv - jnp.mean(gv, axis=-1, keepdims=True)
    v = vc * lax.rsqrt(jnp.mean(vc * vc, axis=-1, keepdims=True) + 1e-5) * lng_ref[...] + lnb_ref[...]
    v_ref[...] = v
    gd = MIX_W // GM_GROUPS
    for g in range(GM_GROUPS):
        sl = slice(g * gd, (g + 1) * gd)
        s = _dot(ws_ref[g], v[:, sl]) + bias_ref[:, sl]
        y_ref[:, sl] = (u[:, sl] * s).astype(BF16)


def _gmlp(z2, p, l, ws_mix, bias_tile):
    m = z2.shape[0]
    c = MIX_W
    rows = GM_CHUNK
    vec = lambda: pl.BlockSpec((None, 1, c), lambda i: (l, 0, 0))
    return pl.pallas_call(
        _gmlp_kernel,
        grid=(m // rows,),
        in_specs=[pl.BlockSpec((rows, c), lambda i: (i, ZB_GU)),
                  pl.BlockSpec((rows, c), lambda i: (i, ZB_GV)),
                  vec(), vec(),
                  pl.BlockSpec((None, GM_GROUPS, rows, rows), lambda i: (l, 0, 0, 0)),
                  pl.BlockSpec((None, rows, c), lambda i: (l, 0, 0))],
        out_specs=[pl.BlockSpec((rows, c), lambda i: (i, 0)),
                   pl.BlockSpec((rows, c), lambda i: (i, 0))],
        out_shape=[jax.ShapeDtypeStruct((m, c), BF16), jax.ShapeDtypeStruct((m, c), F32)],
        compiler_params=_cparams(("parallel",)),
        name="gmlp",
    )(z2, z2, p["gmlp_ln_g"], p["gmlp_ln_b"], ws_mix, bias_tile)


RW_ROWS = 64


def _split_bf16(x):
    hi = x.astype(BF16)
    lo = (x - hi.astype(F32)).astype(BF16)
    return hi, lo


def _dot_inv(a, b):
    a_hi, a_lo = _split_bf16(a)
    b_hi, b_lo = _split_bf16(b)
    d = lambda x, y: jnp.dot(x, y, preferred_element_type=F32)
    return d(a_hi, b_hi) + (d(a_hi, b_lo) + d(a_lo, b_hi))


def _dot_ones(x, ones, pieces, ones_on_left=False):
    acc = None
    rem = x
    for i in range(pieces):
        part = rem.astype(BF16)
        if i + 1 < pieces:
            rem = rem - part.astype(F32)
        term = (jnp.dot(ones, part, preferred_element_type=F32) if ones_on_left
                else jnp.dot(part, ones, preferred_element_type=F32))
        acc = term if acc is None else acc + term
    return acc


def _head_sums(x, ones):
    half = MIX_W // 2
    rows = x.shape[0]
    stacked = jnp.concatenate([x[:, :half], x[:, half:]], axis=0)
    s = _dot_ones(stacked, ones, 2)
    return jnp.concatenate([s[:rows], s[rows:]], axis=1)


def _rwkv_kernel(zr_ref, zk_ref, zv_ref, zt_ref, shr_ref, sht_ref, s0_ref,
                 mur_ref, mut_ref, w0_ref, a0_ref, kkp_ref, kap_ref, rk_ref, lng_ref, lnb_ref,
                 w2_ref, a2_ref, g2_ref, hsum_ref,
                 y_ref, sout_ref, prev_ref, s_ref, *, nb, L, carry):
    ci = pl.program_id(1)
    c = MIX_W
    rows = nb * L
    bb = RW_ROWS // L
    ngrp = rows // RW_ROWS
    log_l = L.bit_length() - 1

    @pl.when(ci == 0)
    def _():
        prev_ref[:, :, 0:RW_RKV] = shr_ref[...]
        prev_ref[:, :, RW_RKV:RW_RKV + RW_TAIL_PAD] = sht_ref[...]
        if carry:
            s_ref[...] = s0_ref[...]

    first = (lax.broadcasted_iota(jnp.int32, (rows, 1), 0) & (L - 1)) == 0

    def shift(z_ref, lo, hi, mu):
        w = hi - lo
        z3 = z_ref[...]
        z = z3.reshape(rows, w)
        prev = jnp.broadcast_to(prev_ref[:, :, lo:hi], (nb, L, w)).reshape(rows, w)
        zp = jnp.where(first, prev, pltpu.roll(z, 1, 0))
        prev_ref[:, :, lo:hi] = z3[:, L - 1:L, :]
        return z + (zp - z) * mu

    r = shift(zr_ref, 0, c, mur_ref[:, 0:c])
    k = shift(zk_ref, c, 2 * c, mur_ref[:, c:2 * c])
    v = shift(zv_ref, 2 * c, 3 * c, mur_ref[:, 2 * c:3 * c])
    tl = shift(zt_ref, RW_RKV, RW_RKV + RW_TAIL_PAD, mut_ref[...])

    wlin = w0_ref[...] + _dot(jnp.tanh(tl), w2_ref[...])
    logw = -jnp.exp(-_softplus(-wlin) - 0.5)
    a = jax.nn.sigmoid(a0_ref[...] + _dot(tl, a2_ref[...]))
    gate = _dot(jax.nn.sigmoid(tl), g2_ref[...])
    kkf = k * kkp_ref[...]
    kmod = k * (1.0 + (a - 1.0) * kap_ref[...])
    hsum = hsum_ref[...]
    kkn = kkf / jnp.maximum(jnp.sqrt(_head_sums(kkf * kkf, hsum)), 1e-12)
    beta = kkn * a
    bonus = _head_sums(r * kmod * rk_ref[...], hsum) * v

    ri_f = lax.broadcasted_iota(jnp.int32, (rows, rows), 0)
    cj_f = lax.broadcasted_iota(jnp.int32, (rows, rows), 1)
    cum_op = (((ri_f >> log_l) == (cj_f >> log_l)) & (cj_f <= ri_f)).astype(BF16)
    cum = _dot_ones(logw, cum_op, 3, ones_on_left=True)
    c_last = cum.reshape(nb, L, c)[:, L - 1:L, :]
    e_last = jnp.exp(c_last)
    e_rest = jnp.exp(jnp.broadcast_to(c_last, (nb, L, c)).reshape(rows, c) - cum)
    e_mc = jnp.exp(-cum)
    a_t = -kkn * jnp.exp(cum - logw)
    r_t = r * jnp.exp(cum)
    b_t = beta * e_mc
    k_t = kmod * e_mc
    b_hat = beta * e_rest
    k_hat = kmod * e_rest

    n = RW_ROWS
    ri = lax.broadcasted_iota(jnp.int32, (n, n), 0)
    cj = lax.broadcasted_iota(jnp.int32, (n, n), 1)
    same = (ri >> log_l) == (cj >> log_l)
    strict = same & (cj < ri)
    incl = same & (cj <= ri)
    eye = (cj == ri).astype(F32)
    lvl = [((ri >> (s + 1)) == (cj >> (s + 1))) & ((ri >> s) != (cj >> s)) & (cj < ri) for s in range(log_l)]

    probs = [(g, h) for g in range(ngrp) for h in range(RW_H)]

    def cut(x, g, h):
        return x[g * n:(g + 1) * n, h * RW_HD:(h + 1) * RW_HD]

    state_in = s_ref if carry else s0_ref
    s0 = {(g, h, j): state_in[g * bb + j, h] for (g, h) in probs for j in range(bb)}
    ar, m_ba, m_ka, m_br, m_kr = {}, {}, {}, {}, {}
    for p in probs:
        ar[p] = jnp.concatenate([cut(a_t, *p), cut(r_t, *p)], axis=0)
        mb = _dot_nt(ar[p], cut(b_t, *p))
        mk = _dot_nt(ar[p], cut(k_t, *p))
        m_ba[p] = jnp.where(strict, mb[0:n], 0.0)
        m_br[p] = jnp.where(incl, mb[n:2 * n], 0.0)
        m_ka[p] = jnp.where(strict, mk[0:n], 0.0)
        m_kr[p] = jnp.where(incl, mk[n:2 * n], 0.0)

    inv = {p: eye + jnp.where(lvl[0], m_ba[p], 0.0) for p in probs}
    for s in range(1, log_l):
        half = {p: _dot_inv(inv[p], jnp.where(lvl[s], m_ba[p], 0.0)) for p in probs}
        inv = {p: inv[p] + _dot_inv(half[p], inv[p]) for p in probs}

    xa, xr = {}, {}
    for (g, h) in probs:
        pa, pr = [], []
        for j in range(bb):
            arj = ar[(g, h)] if bb == 1 else jnp.concatenate(
                [cut(a_t, g, h)[j * L:(j + 1) * L], cut(r_t, g, h)[j * L:(j + 1) * L]], axis=0)
            as0 = _dot_nt(arj, s0[(g, h, j)])
            pa.append(as0[0:L])
            pr.append(as0[L:2 * L])
        xa[(g, h)] = pa[0] if bb == 1 else jnp.concatenate(pa, axis=0)
        xr[(g, h)] = pr[0] if bb == 1 else jnp.concatenate(pr, axis=0)

    vv = {p: cut(v, *p) for p in probs}
    u = {p: _dot_inv(inv[p], xa[p] + _dot(m_ka[p], vv[p])) for p in probs}
    o = {p: xr[p] + _dot(m_br[p], u[p]) + _dot(m_kr[p], vv[p]) for p in probs}

    state_out = s_ref if carry else sout_ref
    for (g, h) in probs:
        bh, kh = cut(b_hat, g, h), cut(k_hat, g, h)
        for j in range(bb):
            js = slice(j * L, (j + 1) * L)
            uv = jnp.concatenate([u[(g, h)][js], vv[(g, h)][js]], axis=0)
            bk = jnp.concatenate([bh[js], kh[js]], axis=0)
            seq = g * bb + j
            state_out[seq, h] = s0[(g, h, j)] * e_last[seq][:, h * RW_HD:(h + 1) * RW_HD] + _dot_tn(uv, bk)

    o_rows = [jnp.concatenate([o[(g, h)] for h in range(RW_H)], axis=1) for g in range(ngrp)]
    o_all = o_rows[0] if ngrp == 1 else jnp.concatenate(o_rows, axis=0)
    oc = o_all - _head_sums(o_all, hsum) * (1.0 / RW_HD)
    on = oc * lax.rsqrt(_head_sums(oc * oc, hsum) * (1.0 / RW_HD) + RW_LN_EPS)
    y = (on * lng_ref[...] + lnb_ref[...] + bonus) * gate
    y_ref[...] = y.reshape(nb, L, c).astype(BF16)

    if carry:
        @pl.when(ci == pl.num_programs(1) - 1)
        def _():
            sout_ref[...] = s_ref[...]


def _rwkv(z3, sh_rkv, sh_tail, s_all, p, l, nb, L):
    bsz, tlen, _ = z3.shape
    c = MIX_W
    nc = tlen // L
    carry = nc > 1
    half = c // 2
    hsum = jnp.asarray(np.kron(np.eye(half // RW_HD, dtype=np.float32), np.ones((RW_HD, RW_HD), np.float32)), BF16)
    vec = lambda w=c: pl.BlockSpec((None, 1, w), lambda b, i: (l, 0, 0))
    lora = lambda: pl.BlockSpec((None, RW_TAIL_PAD, c), lambda b, i: (l, 0, 0))
    zspec = lambda blk: pl.BlockSpec((nb, L, c), lambda b, i, blk=blk: (b, i, blk))
    sspec = lambda: pl.BlockSpec((None, nb, RW_H, RW_HD, RW_HD), lambda b, i: (l, b, 0, 0, 0))
    kern = functools.partial(_rwkv_kernel, nb=nb, L=L, carry=carry)
    return pl.pallas_call(
        kern,
        grid=(bsz // nb, nc),
        in_specs=[zspec(ZB_R), zspec(ZB_K), zspec(ZB_V),
                  pl.BlockSpec((nb, L, RW_TAIL_PAD), lambda b, i: (b, i, Z_TAIL // RW_TAIL_PAD)),
                  pl.BlockSpec((None, nb, 1, RW_RKV), lambda b, i: (l, b, 0, 0)),
                  pl.BlockSpec((None, nb, 1, RW_TAIL_PAD), lambda b, i: (l, b, 0, 0)),
                  sspec(),
                  vec(RW_RKV), vec(RW_TAIL_PAD), vec(), vec(), vec(), vec(), vec(), vec(), vec(),
                  lora(), lora(), lora(),
                  pl.BlockSpec((half, half), lambda b, i: (0, 0))],
        out_specs=[pl.BlockSpec((nb, L, c), lambda b, i: (b, i, 0)), sspec()],
        out_shape=[jax.ShapeDtypeStruct((bsz, tlen, c), BF16),
                   jax.ShapeDtypeStruct(s_all.shape, F32)],
        input_output_aliases={6: 1},
        scratch_shapes=[pltpu.VMEM((nb, 1, RW_RKV + RW_TAIL_PAD), F32),
                        pltpu.VMEM((nb, RW_H, RW_HD, RW_HD) if carry else (1, 1, 8, 128), F32)],
        compiler_params=_cparams(("parallel", "arbitrary")),
        name="rwkv7",
    )(z3, z3, z3, z3, sh_rkv, sh_tail, s_all,
      p["rwkv_mu_rkv"], p["rwkv_mu_tail"], p["rwkv_w0"], p["rwkv_a0"], p["rwkv_kk"], p["rwkv_ka"], p["rwkv_rk"],
      p["rwkv_ln_g"], p["rwkv_ln_b"], p["rwkv_w2"], p["rwkv_a2"], p["rwkv_g2"], hsum)


ML_ROWS = 128


def _mlstm_kernel(zx_ref, zv_ref, zo_ref, zif_ref, zift_ref, buf_ref, c0_ref, n0_ref, m0_ref,
                  cw_ref, cb_ref, wq_ref, wk_ref, brow_ref, bcol_ref, ng_ref, skip_ref,
                  y_ref, cout_ref, nout_ref, mout_ref, xs_ref, c_ref, n_ref, m_ref, *, nb, L, carry):
    ci = pl.program_id(1)
    c = MIX_W
    rows = nb * L
    log_l = L.bit_length() - 1

    @pl.when(ci == 0)
    def _():
        xs_ref[:, 5:8, :] = buf_ref[...]
        if carry:
            c_ref[...] = c0_ref[...]
            n_ref[...] = n0_ref[...]
            m_ref[...] = m0_ref[...]

    if carry:
        @pl.when(ci > 0)
        def _():
            xs_ref[:, 5:8, :] = xs_ref[:, L + 5:L + 8, :]

    xs_ref[:, 8:8 + L, :] = zx_ref[...]
    conv = cb_ref[...] + cw_ref[0:1, :] * xs_ref[:, 5:5 + L, :]
    for j in range(1, CONV_W):
        conv = conv + cw_ref[j:j + 1, :] * xs_ref[:, 5 + j:5 + j + L, :]
    cc = jax.nn.silu(conv).reshape(rows, c)
    zv = zv_ref[...].reshape(rows, c)
    zo = zo_ref[...].reshape(rows, c)

    ri = lax.broadcasted_iota(jnp.int32, (rows, rows), 0)
    cj = lax.broadcasted_iota(jnp.int32, (rows, rows), 1)
    same = (ri >> log_l) == (cj >> log_l)
    causal = same & (cj <= ri)

    gate_col = zif_ref[...].reshape(rows, 128) + brow_ref[...]
    bcum_col = _dot_ones(_log_sigmoid(gate_col), causal.astype(BF16), 3, ones_on_left=True)
    gate_row = zift_ref[...] + bcol_ref[...]
    bcum_row = _dot_ones(_log_sigmoid(gate_row), (same & (ri <= cj)).astype(BF16), 3)

    def per_row(x3):
        return jnp.broadcast_to(x3, (nb, L, x3.shape[-1])).reshape(rows, x3.shape[-1])

    def last(x):
        return x.reshape(nb, L, x.shape[-1])[:, L - 1:L, :]

    c_in, n_in, m_in = (c_ref, n_ref, m_ref) if carry else (c0_ref, n0_ref, m0_ref)
    c_out, n_out, m_out = (c_ref, n_ref, m_ref) if carry else (cout_ref, nout_ref, mout_ref)
    heads = range(ML_H)
    sls = [slice(h * ML_HD, (h + 1) * ML_HD) for h in heads]
    cmat = {(h, j): c_in[j, h] for h in heads for j in range(nb)}
    nvec = [n_in[:, h:h + 1, :] for h in heads]
    m_prev = [m_in[:, h:h + 1, 0:1] for h in heads]

    ch = [cc[:, sls[h]] for h in heads]
    q = [_dot(ch[h], wq_ref[h]) for h in heads]
    k = [_dot(ch[h], wk_ref[h]) * (ML_HD ** -0.5) for h in heads]
    v = [zv[:, sls[h]] for h in heads]
    qk = [_dot_nt(q[h], k[h]) for h in heads]
    qc = []
    for h in heads:
        parts = [_dot_nt(q[h][j * L:(j + 1) * L], cmat[(h, j)]) for j in range(nb)]
        qc.append(parts[0] if nb == 1 else jnp.concatenate(parts, axis=0))

    b_col = [bcum_col[:, ML_H + h:ML_H + h + 1] for h in heads]
    li_col = [gate_col[:, h:h + 1] for h in heads]
    m_t, s, sc = [], [], []
    for h in heads:
        log_d = jnp.where(causal, b_col[h] - bcum_row[ML_H + h:ML_H + h + 1, :] + gate_row[h:h + 1, :], -jnp.inf)
        inter = b_col[h] + per_row(m_prev[h])
        mt = jnp.maximum(jnp.max(log_d, axis=-1, keepdims=True), inter)
        m_t.append(mt)
        s.append(qk[h] * jnp.exp(log_d - mt))
        sc.append(jnp.exp(inter - mt))

    for h in heads:
        num = _dot(s[h], v[h]) + sc[h] * qc[h]
        den = (jnp.sum(s[h], axis=-1, keepdims=True)
               + sc[h] * jnp.sum(q[h] * per_row(nvec[h]), axis=-1, keepdims=True))
        hh = num / jnp.maximum(jnp.abs(den), jnp.exp(-m_t[h]))
        hc = hh - jnp.mean(hh, axis=-1, keepdims=True)
        hn = hc * lax.rsqrt(jnp.mean(hc * hc, axis=-1, keepdims=True) + 1e-6) * ng_ref[:, sls[h]]
        y = jax.nn.sigmoid(zo[:, sls[h]]) * (hn + skip_ref[:, sls[h]] * ch[h])
        y_ref[:, :, sls[h]] = y.reshape(nb, L, ML_HD).astype(BF16)

    for h in heads:
        m_new = last(m_t[h])
        b_last = last(b_col[h])
        wj = jnp.exp(per_row(b_last - m_new) - b_col[h] + li_col[h])
        dec = jnp.exp(b_last + m_prev[h] - m_new)
        wv = wj * v[h]
        for j in range(nb):
            js = slice(j * L, (j + 1) * L)
            c_out[j, h] = dec[j] * cmat[(h, j)] + _dot_tn(wv[js], k[h][js])
        n_out[:, h:h + 1, :] = dec * nvec[h] + jnp.sum((wj * k[h]).reshape(nb, L, ML_HD), axis=1, keepdims=True)
        m_out[:, h:h + 1, :] = jnp.broadcast_to(m_new, (nb, 1, ML_HD))

    if carry:
        @pl.when(ci == pl.num_programs(1) - 1)
        def _():
            cout_ref[...] = c_ref[...]
            nout_ref[...] = n_ref[...]
            mout_ref[...] = m_ref[...]


def _mlstm(z3, zift, buf, c_all, n_all, m_all, p, l, nb, L):
    bsz, tlen, _ = z3.shape
    c = MIX_W
    nc = tlen // L
    rows = nb * L
    carry = nc > 1
    bcol = jnp.broadcast_to(p["mlstm_bif"][l][:, None], (8, rows))
    vec = lambda: pl.BlockSpec((None, 1, c), lambda b, i: (l, 0, 0))
    zspec = lambda blk: pl.BlockSpec((nb, L, c), lambda b, i, blk=blk: (b, i, blk))
    hmat = lambda: pl.BlockSpec((None, ML_H, ML_HD, ML_HD), lambda b, i: (l, 0, 0, 0))
    cspec = lambda: pl.BlockSpec((None, nb, ML_H, ML_HD, ML_HD), lambda b, i: (l, b, 0, 0, 0))
    nspec = lambda: pl.BlockSpec((None, nb, ML_H, ML_HD), lambda b, i: (l, b, 0, 0))
    kern = functools.partial(_mlstm_kernel, nb=nb, L=L, carry=carry)
    tiny = (1, 8, 128)
    return pl.pallas_call(
        kern,
        grid=(bsz // nb, nc),
        in_specs=[zspec(ZB_MX), zspec(ZB_MV), zspec(ZB_MO),
                  pl.BlockSpec((nb, L, 128), lambda b, i: (b, i, Z_MIF // 128)),
                  pl.BlockSpec((None, 8, rows), lambda b, i: (b * nc + i, 0, 0)),
                  pl.BlockSpec((None, nb, CONV_W - 1, c), lambda b, i: (l, b, 0, 0)),
                  cspec(), nspec(), nspec(),
                  pl.BlockSpec((None, CONV_W, c), lambda b, i: (l, 0, 0)),
                  vec(), hmat(), hmat(),
                  pl.BlockSpec((None, 1, 128), lambda b, i: (l, 0, 0)),
                  pl.BlockSpec((8, rows), lambda b, i: (0, 0)),
                  vec(), vec()],
        out_specs=[pl.BlockSpec((nb, L, c), lambda b, i: (b, i, 0)), cspec(), nspec(), nspec()],
        out_shape=[jax.ShapeDtypeStruct((bsz, tlen, c), BF16),
                   jax.ShapeDtypeStruct(c_all.shape, F32),
                   jax.ShapeDtypeStruct(n_all.shape, F32),
                   jax.ShapeDtypeStruct(m_all.shape, F32)],
        input_output_aliases={6: 1, 7: 2, 8: 3},
        scratch_shapes=[pltpu.VMEM((nb, 8 + L, c), F32),
                        pltpu.VMEM((nb, ML_H, ML_HD, ML_HD) if carry else tiny, F32),
                        pltpu.VMEM((nb, ML_H, ML_HD) if carry else tiny, F32),
                        pltpu.VMEM((nb, ML_H, ML_HD) if carry else tiny, F32)],
        compiler_params=_cparams(("parallel", "arbitrary")),
        name="mlstm",
    )(z3, z3, z3, z3, zift, buf, c_all, n_all, m_all,
      p["mlstm_conv_w"], p["mlstm_conv_b"], p["mlstm_wq"], p["mlstm_wk"], p["mlstm_brow"], bcol,
      p["mlstm_norm_g"], p["mlstm_skip"])


def _block_diag(w):
    dp, nb, d, _ = w.shape
    eye = jnp.eye(nb, dtype=w.dtype)
    return jnp.einsum("lnij,nm->lnimj", w, eye).reshape(dp, nb * d, nb * d)


def _prepare(raw):
    p = {}
    w_in = raw["w_in"]
    zeros = lambda n: jnp.zeros(w_in.shape[:2] + (n,), w_in.dtype)
    p["w_mix"] = jnp.concatenate(
        [w_in[..., 0:3584], w_in[..., 3744:5280], w_in[..., 3584:3744], zeros(RW_TAIL_PAD - RW_TAIL),
         w_in[..., 5280:5288], zeros(Z_COLS - Z_MIF - 8)], axis=-1).astype(BF16)
    p["w_gate"] = w_in[..., P_SRC:].astype(BF16)
    for name in ("w_branch", "w_out", "w_ff_in", "w_ff_out"):
        p[name] = raw[name].astype(BF16)
    row = lambda a: a.reshape(DEPTH, 1, -1)
    for name in ("norm_pre_mix", "norm_post_mix", "norm_pre_ffn", "norm_post_ffn",
                 "lru_conv_b", "lru_ba", "lru_bx", "lru_lambda", "gmlp_ln_g", "gmlp_ln_b",
                 "rwkv_w0", "rwkv_a0", "rwkv_kk", "rwkv_ka", "rwkv_rk", "rwkv_ln_g", "rwkv_ln_b",
                 "mlstm_conv_b", "mlstm_norm_g", "mlstm_skip"):
        p[name] = row(raw[name])
    p["lru_conv_w"] = raw["lru_conv_w"]
    p["mlstm_conv_w"] = raw["mlstm_conv_w"]
    p["lru_wa"] = _block_diag(raw["lru_wa"]).astype(BF16)
    p["lru_wx"] = _block_diag(raw["lru_wx"]).astype(BF16)
    mu = raw["rwkv_mu"]
    p["rwkv_mu_rkv"] = row(mu[:, :RW_RKV])
    p["rwkv_mu_tail"] = row(jnp.pad(mu[:, RW_RKV:], ((0, 0), (0, RW_TAIL_PAD - RW_TAIL))))

    def lora(w, lo):
        return jnp.pad(w, ((0, 0), (lo, RW_TAIL_PAD - lo - w.shape[1]), (0, 0))).astype(BF16)

    p["rwkv_w2"] = lora(raw["rwkv_w2"], 0)
    p["rwkv_a2"] = lora(raw["rwkv_a2"], 32)
    p["rwkv_g2"] = lora(raw["rwkv_g2"], 64)
    p["mlstm_wq"] = raw["mlstm_wq"].astype(BF16)
    p["mlstm_wk"] = raw["mlstm_wk"].astype(BF16)
    bif = jnp.concatenate([raw["mlstm_bi"], raw["mlstm_bf"]], axis=-1)
    p["mlstm_bif"] = bif
    p["mlstm_brow"] = row(jnp.pad(bif, ((0, 0), (0, 128 - 8))))
    p["gmlp_ws"] = raw["gmlp_ws"]
    p["gmlp_bs"] = raw["gmlp_bs"]
    return p


def _gmlp_mix_weights(p, tlen):
    L = min(GM_CHUNK, tlen)
    rep = GM_CHUNK // L
    ws = jnp.tril(p["gmlp_ws"][:, :, :L, :L])
    eye = jnp.eye(rep, dtype=ws.dtype)
    ws_mix = jnp.einsum("lgps,ab->lgapbs", ws, eye).reshape(DEPTH, GM_GROUPS, GM_CHUNK, GM_CHUNK)
    bias = jnp.swapaxes(p["gmlp_bs"][:, :, :L], 1, 2)
    bias = jnp.repeat(bias, MIX_W // GM_GROUPS, axis=2)
    bias = jnp.tile(bias, (1, rep, 1))
    return ws_mix, bias


def _group_forward(x3, states, p, is_start, depth=DEPTH):
    bsz, tlen, _ = x3.shape
    m = bsz * tlen
    x = x3.reshape(m, D_MODEL)
    lru_buf, lru_h, rw_shift, rw_s, ml_buf, ml_c, ml_n, ml_m = states
    tm = 512
    if is_start:
        lru_bb, lru_tt, rw_nb, rw_l, ml_nb, ml_l = 1, 512, 1, RW_CHUNK, 1, ML_CHUNK_P
    else:
        lru_bb, lru_tt, rw_nb, rw_l, ml_nb, ml_l = 16, tlen, RW_ROWS // tlen, tlen, ML_ROWS // tlen, tlen
    ws_mix, gm_bias = _gmlp_mix_weights(p, tlen)
    nl = lru_h.shape[0]
    h_all = lru_h.reshape(nl, bsz, 1, MIX_W)
    sh_rkv = rw_shift[:, :, :RW_RKV].reshape(nl, bsz, 1, RW_RKV)
    sh_tail = jnp.pad(rw_shift[:, :, RW_RKV:], ((0, 0), (0, 0), (0, RW_TAIL_PAD - RW_TAIL)))
    sh_tail = sh_tail.reshape(nl, bsz, 1, RW_TAIL_PAD)
    s_all, c_all, n_all = rw_s, ml_c, ml_n
    m_all = jnp.broadcast_to(ml_m[:, :, :, None], (nl, bsz, ML_H, ML_HD))
    new_states = [[] for _ in range(8)]
    gm_vs = []
    for l in range(depth):
        z, hn = _inproj(x, p["norm_pre_mix"], p["w_mix"], l, tm)
        z3 = z.reshape(bsz, tlen, Z_COLS)

        y_a, h_all = _lru(z3, lru_buf, h_all, p, l, is_start, lru_bb, lru_tt)
        y_b, gm_v = _gmlp(z, p, l, ws_mix, gm_bias)
        y_c, s_all = _rwkv(z3, sh_rkv, sh_tail, s_all, p, l, rw_nb, rw_l)
        y_c = y_c.reshape(m, MIX_W)
        ml_rows = ml_nb * ml_l
        zift = jnp.swapaxes(z[:, Z_MIF:Z_MIF + 8].reshape(m // ml_rows, ml_rows, 8), 1, 2)
        y_d, c_all, n_all, m_all = _mlstm(z3, zift, ml_buf, c_all, n_all, m_all, p, l, ml_nb, ml_l)
        y_d = y_d.reshape(m, MIX_W)

        merged = _merge(hn, (y_a.reshape(m, MIX_W), y_b, y_c, y_d), p["w_gate"], p["w_branch"], l, tm)
        x1, hf = _outproj(merged, p["w_out"], x, p["norm_post_mix"], p["norm_pre_ffn"], l, 256)
        hmid = _ffup(hf, p["w_ff_in"], l, tm)
        x = _ffdown(hmid, p["w_ff_out"], x1, p["norm_post_ffn"], l, tm)

        new_states[0].append(z3[:, tlen - (CONV_W - 1):, 0:MIX_W])
        new_states[2].append(jnp.concatenate(
            [z3[:, tlen - 1, ZB_R * MIX_W:ZB_R * MIX_W + RW_RKV], z3[:, tlen - 1, Z_TAIL:Z_TAIL + RW_TAIL]], axis=-1))
        new_states[4].append(z3[:, tlen - (CONV_W - 1):, ZB_MX * MIX_W:(ZB_MX + 1) * MIX_W])
        gm_vs.append(gm_v.reshape(bsz, tlen, MIX_W))
    out_states = [jnp.stack(new_states[0], axis=0), h_all.reshape(nl, bsz, MIX_W)[:depth],
                  jnp.stack(new_states[2], axis=0), s_all[:depth],
                  jnp.stack(new_states[4], axis=0), c_all[:depth], n_all[:depth], m_all[:depth, :, :, 0]]
    return x.reshape(bsz, tlen, D_MODEL), out_states, jnp.stack(gm_vs, axis=0)


def kernel(x_prompt, x_sample, state_lru_conv, state_lru_h, state_rwkv_shift, state_rwkv_wkv, state_mlstm_conv, state_mlstm_C, state_mlstm_n, state_mlstm_m, norm_pre_mix, norm_post_mix, norm_pre_ffn, norm_post_ffn, w_in, lru_conv_w, lru_conv_b, lru_wa, lru_ba, lru_wx, lru_bx, lru_lambda, gmlp_ln_g, gmlp_ln_b, gmlp_ws, gmlp_bs, rwkv_mu, rwkv_w0, rwkv_w2, rwkv_a0, rwkv_a2, rwkv_g2, rwkv_kk, rwkv_ka, rwkv_rk, rwkv_ln_g, rwkv_ln_b, mlstm_conv_w, mlstm_conv_b, mlstm_wq, mlstm_wk, mlstm_bi, mlstm_bf, mlstm_norm_g, mlstm_skip, w_branch, w_out, w_ff_in, w_ff_out):
    raw = dict(norm_pre_mix=norm_pre_mix, norm_post_mix=norm_post_mix, norm_pre_ffn=norm_pre_ffn,
               norm_post_ffn=norm_post_ffn, w_in=w_in, lru_conv_w=lru_conv_w, lru_conv_b=lru_conv_b,
               lru_wa=lru_wa, lru_ba=lru_ba, lru_wx=lru_wx, lru_bx=lru_bx, lru_lambda=lru_lambda,
               gmlp_ln_g=gmlp_ln_g, gmlp_ln_b=gmlp_ln_b, gmlp_ws=gmlp_ws, gmlp_bs=gmlp_bs,
               rwkv_mu=rwkv_mu, rwkv_w0=rwkv_w0, rwkv_w2=rwkv_w2, rwkv_a0=rwkv_a0, rwkv_a2=rwkv_a2,
               rwkv_g2=rwkv_g2, rwkv_kk=rwkv_kk, rwkv_ka=rwkv_ka, rwkv_rk=rwkv_rk, rwkv_ln_g=rwkv_ln_g,
               rwkv_ln_b=rwkv_ln_b, mlstm_conv_w=mlstm_conv_w, mlstm_conv_b=mlstm_conv_b, mlstm_wq=mlstm_wq,
               mlstm_wk=mlstm_wk, mlstm_bi=mlstm_bi, mlstm_bf=mlstm_bf, mlstm_norm_g=mlstm_norm_g,
               mlstm_skip=mlstm_skip, w_branch=w_branch, w_out=w_out, w_ff_in=w_ff_in, w_ff_out=w_ff_out)
    p = _prepare(raw)
    bp = x_prompt.shape[0]
    zero = lambda *s: jnp.zeros((DEPTH, bp) + s, F32)
    prompt_states = (zero(CONV_W - 1, MIX_W), zero(MIX_W), zero(RW_RKV + RW_TAIL), zero(RW_H, RW_HD, RW_HD),
                     zero(CONV_W - 1, MIX_W), zero(ML_H, ML_HD, ML_HD), zero(ML_H, ML_HD), zero(ML_H))
    sample_states = (state_lru_conv, state_lru_h, state_rwkv_shift, state_rwkv_wkv,
                     state_mlstm_conv, state_mlstm_C, state_mlstm_n, state_mlstm_m)
    yp, st_p, _ = _group_forward(x_prompt, prompt_states, p, True)
    ys, st_s, gm_v = _group_forward(x_sample, sample_states, p, False)
    return (yp, ys, *st_p, *st_s, gm_v)
```

```python
import functools

import numpy as np
import jax
import jax.numpy as jnp
from jax import lax
from jax.experimental import pallas as pl
from jax.experimental.pallas import tpu as pltpu

F32 = jnp.float32
BF16 = jnp.bfloat16

D_MODEL = 2048
DEPTH = 4
MIX_W = 512
CONV_W = 4
LRU_BLOCKS = 8
LRU_C = 8.0
GM_CHUNK = 128
GM_GROUPS = 4
RW_HD = 64
RW_H = 8
RW_RKV = 3 * MIX_W
RW_TAIL = 160
RW_TAIL_PAD = 256
RW_LN_EPS = 64e-5
ML_H = 4
ML_HD = 128
D_FF = 5632
N_GATE = 4 * D_MODEL
P_SRC = 5288
Z_COLS = 5632

ZB_LX, ZB_LG, ZB_GU, ZB_GV, ZB_R, ZB_K, ZB_V, ZB_MX, ZB_MV, ZB_MO = range(10)
Z_TAIL = 5120
Z_MIF = 5376

RW_CHUNK = 64
RW_SEQS_P = 2
ML_CHUNK_P = 128
GM_TILE_ROWS = 512
TM_DENSE = 1024
TM_OUTPROJ = 512
OUTPROJ_SUB_ROWS = 256
FFUP_SUB_ROWS = 256
TM_FFDOWN = 512
TK_FFDOWN = 1408
VMEM_LIMIT = 56 * 1024 * 1024
HI = lax.Precision.HIGHEST


def _cparams(sem):
    return pltpu.CompilerParams(dimension_semantics=sem, vmem_limit_bytes=VMEM_LIMIT)


def _softplus(x):
    return jnp.maximum(x, 0.0) + jnp.log1p(jnp.exp(-jnp.abs(x)))


def _log_sigmoid(x):
    return -_softplus(-x)


def _rms(x, g):
    return x * lax.rsqrt(jnp.mean(x * x, axis=-1, keepdims=True) + 1e-6) * g


def _dot(a, b):
    return jnp.dot(a.astype(BF16), b.astype(BF16), preferred_element_type=F32)


def _dot_nt(a, b):
    return lax.dot_general(a.astype(BF16), b.astype(BF16), (((1,), (1,)), ((), ())),
                           preferred_element_type=F32)


def _dot_tn(a, b):
    return lax.dot_general(a.astype(BF16), b.astype(BF16), (((0,), (0,)), ((), ())),
                           preferred_element_type=F32)


def _dot_hi(a, b):
    return jnp.dot(a, b, preferred_element_type=F32, precision=HI)


def _inproj_kernel(x_ref, g_ref, w_ref, z_ref, hn_ref):
    @pl.when(pl.program_id(1) == 0)
    def _():
        hn_ref[...] = _rms(x_ref[...], g_ref[...]).astype(BF16)

    z_ref[...] = jnp.dot(hn_ref[...], w_ref[...], preferred_element_type=F32)


def _inproj(x, g, w, l, tm, tn=512):
    m = x.shape[0]
    return pl.pallas_call(
        _inproj_kernel,
        grid=(m // tm, Z_COLS // tn),
        in_specs=[pl.BlockSpec((tm, D_MODEL), lambda i, j: (i, 0)),
                  pl.BlockSpec((None, 1, D_MODEL), lambda i, j: (l, 0, 0)),
                  pl.BlockSpec((None, D_MODEL, tn), lambda i, j: (l, 0, j))],
        out_specs=[pl.BlockSpec((tm, tn), lambda i, j: (i, j)),
                   pl.BlockSpec((tm, D_MODEL), lambda i, j: (i, 0))],
        out_shape=[jax.ShapeDtypeStruct((m, Z_COLS), F32),
                   jax.ShapeDtypeStruct((m, D_MODEL), BF16)],
        compiler_params=_cparams(("parallel", "arbitrary")),
        name="inproj",
    )(x, g, w)


def _merge_kernel(hn_ref, ya_ref, yb_ref, yc_ref, yd_ref, g0_ref, g1_ref, g2_ref, g3_ref, wb_ref, o_ref):
    hn = hn_ref[...]
    acc = None
    for b, (y_ref, wg_ref) in enumerate(((ya_ref, g0_ref), (yb_ref, g1_ref), (yc_ref, g2_ref), (yd_ref, g3_ref))):
        zg = jnp.dot(hn, wg_ref[...], preferred_element_type=F32)
        br = jnp.dot(y_ref[...], wb_ref[b], preferred_element_type=F32)
        term = jax.nn.sigmoid(zg) * br
        acc = term if acc is None else acc + term
    o_ref[...] = acc.astype(BF16)


def _merge(hn, ys, wgate, wbranch, l, tm, tn=256):
    m = hn.shape[0]
    nb = D_MODEL // tn
    y_spec = pl.BlockSpec((tm, MIX_W), lambda j, i: (i, 0))
    g_specs = [pl.BlockSpec((None, D_MODEL, tn), lambda j, i, b=b: (l, 0, b * nb + j)) for b in range(4)]
    return pl.pallas_call(
        _merge_kernel,
        grid=(nb, m // tm),
        in_specs=[pl.BlockSpec((tm, D_MODEL), lambda j, i: (i, 0)), y_spec, y_spec, y_spec, y_spec,
                  *g_specs,
                  pl.BlockSpec((None, 4, MIX_W, tn), lambda j, i: (l, 0, 0, j))],
        out_specs=pl.BlockSpec((tm, tn), lambda j, i: (i, j)),
        out_shape=jax.ShapeDtypeStruct((m, D_MODEL), BF16),
        compiler_params=_cparams(("parallel", "arbitrary")),
        name="merge",
    )(hn, *ys, wgate, wgate, wgate, wgate, wbranch)


def _outproj_kernel(mg_ref, w_ref, x_ref, gpost_ref, gpre_ref, x1_ref, hf_ref):
    sub = OUTPROJ_SUB_ROWS
    for r0 in range(0, mg_ref.shape[0], sub):
        rs = slice(r0, r0 + sub)
        mix = jnp.dot(mg_ref[rs, :], w_ref[...], preferred_element_type=F32)
        x1 = x_ref[rs, :] + _rms(mix, gpost_ref[...])
        x1_ref[rs, :] = x1
        hf_ref[rs, :] = _rms(x1, gpre_ref[...]).astype(BF16)


def _outproj(merged, w_out, x, g_post, g_pre_ffn, l, tm):
    m = x.shape[0]
    g_spec = pl.BlockSpec((None, 1, D_MODEL), lambda i: (l, 0, 0))
    row_spec = pl.BlockSpec((tm, D_MODEL), lambda i: (i, 0))
    return pl.pallas_call(
        _outproj_kernel,
        grid=(m // tm,),
        in_specs=[row_spec, pl.BlockSpec((None, D_MODEL, D_MODEL), lambda i: (l, 0, 0)), row_spec, g_spec, g_spec],
        out_specs=[row_spec, row_spec],
        out_shape=[jax.ShapeDtypeStruct((m, D_MODEL), F32), jax.ShapeDtypeStruct((m, D_MODEL), BF16)],
        compiler_params=_cparams(("parallel",)),
        name="outproj",
    )(merged, w_out, x, g_post, g_pre_ffn)


def _ffup_kernel(hf_ref, wg_ref, wu_ref, o_ref):
    sub = min(FFUP_SUB_ROWS, hf_ref.shape[0])
    for r0 in range(0, hf_ref.shape[0], sub):
        rs = slice(r0, r0 + sub)
        hf = hf_ref[rs, :]
        g = jnp.dot(hf, wg_ref[...], preferred_element_type=F32)
        u = jnp.dot(hf, wu_ref[...], preferred_element_type=F32)
        o_ref[rs, :] = (jax.nn.silu(g) * u).astype(BF16)


def _ffup(hf, w_ff_in, l, tm, tn=512):
    m = hf.shape[0]
    nb = D_FF // tn
    return pl.pallas_call(
        _ffup_kernel,
        grid=(nb, m // tm),
        in_specs=[pl.BlockSpec((tm, D_MODEL), lambda j, i: (i, 0)),
                  pl.BlockSpec((None, D_MODEL, tn), lambda j, i: (l, 0, j)),
                  pl.BlockSpec((None, D_MODEL, tn), lambda j, i: (l, 0, nb + j))],
        out_specs=pl.BlockSpec((tm, tn), lambda j, i: (i, j)),
        out_shape=jax.ShapeDtypeStruct((m, D_FF), BF16),
        compiler_params=_cparams(("parallel", "arbitrary")),
        name="ffup",
    )(hf, w_ff_in, w_ff_in)


def _ffdown_kernel(h_ref, w_ref, x1_ref, g_ref, o_ref, acc_ref):
    k = pl.program_id(1)

    @pl.when(k == 0)
    def _():
        acc_ref[...] = jnp.zeros_like(acc_ref)

    acc_ref[...] += jnp.dot(h_ref[...], w_ref[...], preferred_element_type=F32)

    @pl.when(k == pl.num_programs(1) - 1)
    def _():
        o_ref[...] = x1_ref[...] + _rms(acc_ref[...], g_ref[...])


def _ffdown(h, w_ff_out, x1, g_post, l, tm, tk=TK_FFDOWN):
    m = x1.shape[0]
    row_spec = pl.BlockSpec((tm, D_MODEL), lambda i, k: (i, 0))
    return pl.pallas_call(
        _ffdown_kernel,
        grid=(m // tm, D_FF // tk),
        in_specs=[pl.BlockSpec((tm, tk), lambda i, k: (i, k)),
                  pl.BlockSpec((None, tk, D_MODEL), lambda i, k: (l, k, 0)),
                  row_spec,
                  pl.BlockSpec((None, 1, D_MODEL), lambda i, k: (l, 0, 0))],
        out_specs=row_spec,
        out_shape=jax.ShapeDtypeStruct((m, D_MODEL), F32),
        scratch_shapes=[pltpu.VMEM((tm, D_MODEL), F32)],
        compiler_params=_cparams(("parallel", "arbitrary")),
        name="ffdown",
    )(h, w_ff_out, x1, g_post)


def _lru_kernel(zx_ref, zg_ref, buf_ref, h0_ref, cw_ref, cb_ref, wa_ref, ba_ref, wx_ref, bx_ref, lam_ref,
                y_ref, hout_ref, xs_ref, a_ref, b_ref, h_ref, *, is_start, bb, tt):
    t = pl.program_id(1)
    c = MIX_W

    @pl.when(t == 0)
    def _():
        xs_ref[:, 5:8, :] = buf_ref[...]
        h_ref[...] = jnp.broadcast_to(h0_ref[...], (bb, 8, c))

    @pl.when(t > 0)
    def _():
        xs_ref[:, 5:8, :] = xs_ref[:, tt + 5:tt + 8, :]

    xs_ref[:, 8:8 + tt, :] = zx_ref[...]
    xc = cb_ref[...] + cw_ref[0:1, :] * xs_ref[:, 5:5 + tt, :]
    for j in range(1, CONV_W):
        xc = xc + cw_ref[j:j + 1, :] * xs_ref[:, 5 + j:5 + j + tt, :]
    xc2 = xc.reshape(bb * tt, c)
    r = jax.nn.sigmoid(_dot(xc2, wa_ref[...]) + ba_ref[...])
    i = jax.nn.sigmoid(_dot(xc2, wx_ref[...]) + bx_ref[...])
    log_a = LRU_C * r * _log_sigmoid(lam_ref[...])
    a = jnp.exp(log_a)
    mult = jnp.sqrt(1.0 - jnp.exp(2.0 * log_a))
    if is_start:
        tpos = lax.broadcasted_iota(jnp.int32, (bb, tt, c), 1).reshape(bb * tt, c) + t * tt
        mult = jnp.where(tpos == 0, 1.0, mult)
    a_ref[...] = a.reshape(bb, tt, c)
    b_ref[...] = (mult * i * xc2).reshape(bb, tt, c)

    row = lax.broadcasted_iota(jnp.int32, (bb, 8, c), 1).reshape(bb * 8, c)

    def group(gi, carry):
        off = pl.multiple_of(gi * 8, 8)
        av = a_ref[:, pl.ds(off, 8), :].reshape(bb * 8, c)
        bv = b_ref[:, pl.ds(off, 8), :].reshape(bb * 8, c)
        for s in (1, 2, 4):
            keep = row >= s
            a_sh = pltpu.roll(av, s, 0)
            b_sh = pltpu.roll(bv, s, 0)
            bv = jnp.where(keep, av * b_sh + bv, bv)
            av = jnp.where(keep, av * a_sh, av)
        hh = (av * h_ref[...].reshape(bb * 8, c) + bv).reshape(bb, 8, c)
        b_ref[:, pl.ds(off, 8), :] = hh
        h_ref[...] = jnp.broadcast_to(hh[:, 7:8, :], (bb, 8, c))
        return carry

    lax.fori_loop(0, tt // 8, group, 0)
    y_ref[...] = (b_ref[...] * jax.nn.gelu(zg_ref[...])).astype(BF16)

    @pl.when(t == pl.num_programs(1) - 1)
    def _():
        hout_ref[...] = h_ref[:, 7:8, :]


def _lru(z3, buf, h_all, p, l, is_start, bb, tt):
    bsz, tlen, _ = z3.shape
    c = MIX_W
    vec = lambda: pl.BlockSpec((None, 1, c), lambda b, t: (l, 0, 0))
    mat = lambda: pl.BlockSpec((None, c, c), lambda b, t: (l, 0, 0))
    hspec = lambda: pl.BlockSpec((None, bb, 1, c), lambda b, t: (l, b, 0, 0))
    kern = functools.partial(_lru_kernel, is_start=is_start, bb=bb, tt=tt)
    return pl.pallas_call(
        kern,
        grid=(bsz // bb, tlen // tt),
        in_specs=[pl.BlockSpec((bb, tt, c), lambda b, t: (b, t, ZB_LX)),
                  pl.BlockSpec((bb, tt, c), lambda b, t: (b, t, ZB_LG)),
                  pl.BlockSpec((None, bb, CONV_W - 1, c), lambda b, t: (l, b, 0, 0)),
                  hspec(),
                  pl.BlockSpec((None, CONV_W, c), lambda b, t: (l, 0, 0)),
                  vec(), mat(), vec(), mat(), vec(), vec()],
        out_specs=[pl.BlockSpec((bb, tt, c), lambda b, t: (b, t, 0)), hspec()],
        out_shape=[jax.ShapeDtypeStruct((bsz, tlen, c), BF16),
                   jax.ShapeDtypeStruct(h_all.shape, F32)],
        input_output_aliases={3: 1},
        scratch_shapes=[pltpu.VMEM((bb, 8 + tt, c), F32), pltpu.VMEM((bb, tt, c), F32),
                        pltpu.VMEM((bb, tt, c), F32), pltpu.VMEM((bb, 8, c), F32)],
        compiler_params=_cparams(("parallel", "arbitrary")),
        name="rglru",
    )(z3, z3, buf, h_all, p["lru_conv_w"], p["lru_conv_b"], p["lru_wa"], p["lru_ba"], p["lru_wx"], p["lru_bx"],
      p["lru_lambda"])


def _gmlp_kernel(zu_ref, zv_ref, lng_ref, lnb_ref, ws_ref, bias_ref, y_ref, v_ref):
    u = jax.nn.gelu(zu_ref[...])
    gv = jax.nn.gelu(zv_ref[...])
    vc = gv - jnp.mean(gv, axis=-1, keepdims=True)
    v = vc * lax.rsqrt(jnp.mean(vc * vc, axis=-1, keepdims=True) + 1e-5) * lng_ref[...] + lnb_ref[...]
    v_ref[...] = v
    gd = MIX_W // GM_GROUPS
    vb = v.astype(BF16)
    for ck in range(v.shape[0] // GM_CHUNK):
        rs = slice(ck * GM_CHUNK, (ck + 1) * GM_CHUNK)
        for g in range(GM_GROUPS):
            sl = slice(g * gd, (g + 1) * gd)
            s = jnp.dot(ws_ref[g], vb[rs, sl], preferred_element_type=F32) + bias_ref[:, sl]
            y_ref[rs, sl] = (u[rs, sl] * s).astype(BF16)


def _gmlp(z2, p, l, ws_mix, bias_tile):
    m = z2.shape[0]
    c = MIX_W
    rows = GM_TILE_ROWS
    vec = lambda: pl.BlockSpec((None, 1, c), lambda i: (l, 0, 0))
    return pl.pallas_call(
        _gmlp_kernel,
        grid=(m // rows,),
        in_specs=[pl.BlockSpec((rows, c), lambda i: (i, ZB_GU)),
                  pl.BlockSpec((rows, c), lambda i: (i, ZB_GV)),
                  vec(), vec(),
                  pl.BlockSpec((None, GM_GROUPS, GM_CHUNK, GM_CHUNK), lambda i: (l, 0, 0, 0)),
                  pl.BlockSpec((None, GM_CHUNK, c), lambda i: (l, 0, 0))],
        out_specs=[pl.BlockSpec((rows, c), lambda i: (i, 0)),
                   pl.BlockSpec((rows, c), lambda i: (i, 0))],
        out_shape=[jax.ShapeDtypeStruct((m, c), BF16), jax.ShapeDtypeStruct((m, c), F32)],
        compiler_params=_cparams(("parallel",)),
        name="gmlp",
    )(z2, z2, p["gmlp_ln_g"], p["gmlp_ln_b"], ws_mix, bias_tile)


RW_ROWS = 64


def _split_bf16(x):
    hi = x.astype(BF16)
    lo = (x - hi.astype(F32)).astype(BF16)
    return hi, lo


def _dot_inv(a, b):
    a_hi, a_lo = _split_bf16(a)
    b_hi, b_lo = _split_bf16(b)
    d = lambda x, y: jnp.dot(x, y, preferred_element_type=F32)
    return d(a_hi, b_hi) + (d(a_hi, b_lo) + d(a_lo, b_hi))


def _dot_ones(x, ones, pieces, ones_on_left=False):
    acc = None
    rem = x
    for i in range(pieces):
        part = rem.astype(BF16)
        if i + 1 < pieces:
            rem = rem - part.astype(F32)
        term = (jnp.dot(ones, part, preferred_element_type=F32) if ones_on_left
                else jnp.dot(part, ones, preferred_element_type=F32))
        acc = term if acc is None else acc + term
    return acc


def _head_sums(x, ones):
    half = MIX_W // 2
    rows = x.shape[0]
    stacked = jnp.concatenate([x[:, :half], x[:, half:]], axis=0)
    s = _dot_ones(stacked, ones, 2)
    return jnp.concatenate([s[:rows], s[rows:]], axis=1)


def _rwkv_kernel(zr_ref, zk_ref, zv_ref, zt_ref, shr_ref, sht_ref, s0_ref,
                 mur_ref, mut_ref, w0_ref, a0_ref, kkp_ref, kap_ref, rk_ref, lng_ref, lnb_ref,
                 w2_ref, a2_ref, g2_ref, hsum_ref,
                 y_ref, sout_ref, prev_ref, s_ref, *, nb, L, carry):
    ci = pl.program_id(1)
    c = MIX_W
    rows = nb * L
    bb = RW_ROWS // L
    ngrp = rows // RW_ROWS
    log_l = L.bit_length() - 1

    @pl.when(ci == 0)
    def _():
        prev_ref[:, :, 0:RW_RKV] = shr_ref[...]
        prev_ref[:, :, RW_RKV:RW_RKV + RW_TAIL_PAD] = sht_ref[...]
        if carry:
            s_ref[...] = s0_ref[...]

    first = (lax.broadcasted_iota(jnp.int32, (rows, 1), 0) & (L - 1)) == 0

    def shift(z_ref, lo, hi, mu):
        w = hi - lo
        z3 = z_ref[...]
        z = z3.reshape(rows, w)
        prev = jnp.broadcast_to(prev_ref[:, :, lo:hi], (nb, L, w)).reshape(rows, w)
        zp = jnp.where(first, prev, pltpu.roll(z, 1, 0))
        prev_ref[:, :, lo:hi] = z3[:, L - 1:L, :]
        return z + (zp - z) * mu

    r = shift(zr_ref, 0, c, mur_ref[:, 0:c])
    k = shift(zk_ref, c, 2 * c, mur_ref[:, c:2 * c])
    v = shift(zv_ref, 2 * c, 3 * c, mur_ref[:, 2 * c:3 * c])
    tl = shift(zt_ref, RW_RKV, RW_RKV + RW_TAIL_PAD, mut_ref[...])

    wlin = w0_ref[...] + _dot(jnp.tanh(tl), w2_ref[...])
    logw = -jnp.exp(-_softplus(-wlin) - 0.5)
    a = jax.nn.sigmoid(a0_ref[...] + _dot(tl, a2_ref[...]))
    gate = _dot(jax.nn.sigmoid(tl), g2_ref[...])
    kkf = k * kkp_ref[...]
    kmod = k * (1.0 + (a - 1.0) * kap_ref[...])
    hsum = hsum_ref[...]
    kkn = kkf / jnp.maximum(jnp.sqrt(_head_sums(kkf * kkf, hsum)), 1e-12)
    beta = kkn * a
    bonus = _head_sums(r * kmod * rk_ref[...], hsum) * v

    ri_f = lax.broadcasted_iota(jnp.int32, (rows, rows), 0)
    cj_f = lax.broadcasted_iota(jnp.int32, (rows, rows), 1)
    cum_op = (((ri_f >> log_l) == (cj_f >> log_l)) & (cj_f <= ri_f)).astype(BF16)
    cum = _dot_ones(logw, cum_op, 3, ones_on_left=True)
    c_last = cum.reshape(nb, L, c)[:, L - 1:L, :]
    e_last = jnp.exp(c_last)
    e_rest = jnp.exp(jnp.broadcast_to(c_last, (nb, L, c)).reshape(rows, c) - cum)
    e_mc = jnp.exp(-cum)
    a_t = -kkn * jnp.exp(cum - logw)
    r_t = r * jnp.exp(cum)
    b_t = beta * e_mc
    k_t = kmod * e_mc
    b_hat = beta * e_rest
    k_hat = kmod * e_rest

    n = RW_ROWS
    ri = lax.broadcasted_iota(jnp.int32, (n, n), 0)
    cj = lax.broadcasted_iota(jnp.int32, (n, n), 1)
    same = (ri >> log_l) == (cj >> log_l)
    strict = same & (cj < ri)
    incl = same & (cj <= ri)
    eye = (cj == ri).astype(F32)
    lvl = [((ri >> (s + 1)) == (cj >> (s + 1))) & ((ri >> s) != (cj >> s)) & (cj < ri) for s in range(log_l)]

    probs = [(g, h) for g in range(ngrp) for h in range(RW_H)]

    def cut(x, g, h):
        return x[g * n:(g + 1) * n, h * RW_HD:(h + 1) * RW_HD]

    state_in = s_ref if carry else s0_ref
    s0 = {(g, h, j): state_in[g * bb + j, h] for (g, h) in probs for j in range(bb)}
    ar, m_ba, m_ka, m_br, m_kr = {}, {}, {}, {}, {}
    for p in probs:
        ar[p] = jnp.concatenate([cut(a_t, *p), cut(r_t, *p)], axis=0)
        mb = _dot_nt(ar[p], cut(b_t, *p))
        mk = _dot_nt(ar[p], cut(k_t, *p))
        m_ba[p] = jnp.where(strict, mb[0:n], 0.0)
        m_br[p] = jnp.where(incl, mb[n:2 * n], 0.0)
        m_ka[p] = jnp.where(strict, mk[0:n], 0.0)
        m_kr[p] = jnp.where(incl, mk[n:2 * n], 0.0)

    inv = {p: eye + jnp.where(lvl[0], m_ba[p], 0.0) for p in probs}
    for s in range(1, log_l):
        half = {p: _dot_inv(inv[p], jnp.where(lvl[s], m_ba[p], 0.0)) for p in probs}
        inv = {p: inv[p] + _dot_inv(half[p], inv[p]) for p in probs}

    xa, xr = {}, {}
    for (g, h) in probs:
        pa, pr = [], []
        for j in range(bb):
            arj = ar[(g, h)] if bb == 1 else jnp.concatenate(
                [cut(a_t, g, h)[j * L:(j + 1) * L], cut(r_t, g, h)[j * L:(j + 1) * L]], axis=0)
            as0 = _dot_nt(arj, s0[(g, h, j)])
            pa.append(as0[0:L])
            pr.append(as0[L:2 * L])
        xa[(g, h)] = pa[0] if bb == 1 else jnp.concatenate(pa, axis=0)
        xr[(g, h)] = pr[0] if bb == 1 else jnp.concatenate(pr, axis=0)

    vv = {p: cut(v, *p) for p in probs}
    u = {p: _dot_inv(inv[p], xa[p] + _dot(m_ka[p], vv[p])) for p in probs}
    o = {p: xr[p] + _dot(m_br[p], u[p]) + _dot(m_kr[p], vv[p]) for p in probs}

    state_out = s_ref if carry else sout_ref
    for (g, h) in probs:
        bh, kh = cut(b_hat, g, h), cut(k_hat, g, h)
        for j in range(bb):
            js = slice(j * L, (j + 1) * L)
            uv = jnp.concatenate([u[(g, h)][js], vv[(g, h)][js]], axis=0)
            bk = jnp.concatenate([bh[js], kh[js]], axis=0)
            seq = g * bb + j
            state_out[seq, h] = s0[(g, h, j)] * e_last[seq][:, h * RW_HD:(h + 1) * RW_HD] + _dot_tn(uv, bk)

    o_rows = [jnp.concatenate([o[(g, h)] for h in range(RW_H)], axis=1) for g in range(ngrp)]
    o_all = o_rows[0] if ngrp == 1 else jnp.concatenate(o_rows, axis=0)
    oc = o_all - _head_sums(o_all, hsum) * (1.0 / RW_HD)
    on = oc * lax.rsqrt(_head_sums(oc * oc, hsum) * (1.0 / RW_HD) + RW_LN_EPS)
    y = (on * lng_ref[...] + lnb_ref[...] + bonus) * gate
    y_ref[...] = y.reshape(nb, L, c).astype(BF16)

    if carry:
        @pl.when(ci == pl.num_programs(1) - 1)
        def _():
            sout_ref[...] = s_ref[...]


def _rwkv(z3, sh_rkv, sh_tail, s_all, p, l, nb, L):
    bsz, tlen, _ = z3.shape
    c = MIX_W
    nc = tlen // L
    carry = nc > 1
    half = c // 2
    hsum = jnp.asarray(np.kron(np.eye(half // RW_HD, dtype=np.float32), np.ones((RW_HD, RW_HD), np.float32)), BF16)
    vec = lambda w=c: pl.BlockSpec((None, 1, w), lambda b, i: (l, 0, 0))
    lora = lambda: pl.BlockSpec((None, RW_TAIL_PAD, c), lambda b, i: (l, 0, 0))
    zspec = lambda blk: pl.BlockSpec((nb, L, c), lambda b, i, blk=blk: (b, i, blk))
    sspec = lambda: pl.BlockSpec((None, nb, RW_H, RW_HD, RW_HD), lambda b, i: (l, b, 0, 0, 0))
    kern = functools.partial(_rwkv_kernel, nb=nb, L=L, carry=carry)
    return pl.pallas_call(
        kern,
        grid=(bsz // nb, nc),
        in_specs=[zspec(ZB_R), zspec(ZB_K), zspec(ZB_V),
                  pl.BlockSpec((nb, L, RW_TAIL_PAD), lambda b, i: (b, i, Z_TAIL // RW_TAIL_PAD)),
                  pl.BlockSpec((None, nb, 1, RW_RKV), lambda b, i: (l, b, 0, 0)),
                  pl.BlockSpec((None, nb, 1, RW_TAIL_PAD), lambda b, i: (l, b, 0, 0)),
                  sspec(),
                  vec(RW_RKV), vec(RW_TAIL_PAD), vec(), vec(), vec(), vec(), vec(), vec(), vec(),
                  lora(), lora(), lora(),
                  pl.BlockSpec((half, half), lambda b, i: (0, 0))],
        out_specs=[pl.BlockSpec((nb, L, c), lambda b, i: (b, i, 0)), sspec()],
        out_shape=[jax.ShapeDtypeStruct((bsz, tlen, c), BF16),
                   jax.ShapeDtypeStruct(s_all.shape, F32)],
        input_output_aliases={6: 1},
        scratch_shapes=[pltpu.VMEM((nb, 1, RW_RKV + RW_TAIL_PAD), F32),
                        pltpu.VMEM((nb, RW_H, RW_HD, RW_HD) if carry else (1, 1, 8, 128), F32)],
        compiler_params=_cparams(("parallel", "arbitrary")),
        name="rwkv7",
    )(z3, z3, z3, z3, sh_rkv, sh_tail, s_all,
      p["rwkv_mu_rkv"], p["rwkv_mu_tail"], p["rwkv_w0"], p["rwkv_a0"], p["rwkv_kk"], p["rwkv_ka"], p["rwkv_rk"],
      p["rwkv_ln_g"], p["rwkv_ln_b"], p["rwkv_w2"], p["rwkv_a2"], p["rwkv_g2"], hsum)


ML_ROWS = 128


def _mlstm_kernel(zx_ref, zv_ref, zo_ref, zif_ref, zift_ref, buf_ref, c0_ref, n0_ref, m0_ref,
                  cw_ref, cb_ref, wq_ref, wk_ref, brow_ref, bcol_ref, ng_ref, skip_ref,
                  y_ref, cout_ref, nout_ref, mout_ref, xs_ref, c_ref, n_ref, m_ref, *, nb, L, carry):
    ci = pl.program_id(1)
    c = MIX_W
    rows = nb * L
    log_l = L.bit_length() - 1

    @pl.when(ci == 0)
    def _():
        xs_ref[:, 5:8, :] = buf_ref[...]
        if carry:
            c_ref[...] = c0_ref[...]
            n_ref[...] = n0_ref[...]
            m_ref[...] = m0_ref[...]

    if carry:
        @pl.when(ci > 0)
        def _():
            xs_ref[:, 5:8, :] = xs_ref[:, L + 5:L + 8, :]

    xs_ref[:, 8:8 + L, :] = zx_ref[...]
    conv = cb_ref[...] + cw_ref[0:1, :] * xs_ref[:, 5:5 + L, :]
    for j in range(1, CONV_W):
        conv = conv + cw_ref[j:j + 1, :] * xs_ref[:, 5 + j:5 + j + L, :]
    cc = jax.nn.silu(conv).reshape(rows, c)
    zv = zv_ref[...].reshape(rows, c)
    zo = zo_ref[...].reshape(rows, c)

    ri = lax.broadcasted_iota(jnp.int32, (rows, rows), 0)
    cj = lax.broadcasted_iota(jnp.int32, (rows, rows), 1)
    same = (ri >> log_l) == (cj >> log_l)
    causal = same & (cj <= ri)

    gate_col = zif_ref[...].reshape(rows, 128) + brow_ref[...]
    bcum_col = _dot_ones(_log_sigmoid(gate_col), causal.astype(BF16), 3, ones_on_left=True)
    gate_row = zift_ref[...] + bcol_ref[...]
    bcum_row = _dot_ones(_log_sigmoid(gate_row), (same & (ri <= cj)).astype(BF16), 3)

    def per_row(x3):
        return jnp.broadcast_to(x3, (nb, L, x3.shape[-1])).reshape(rows, x3.shape[-1])

    def last(x):
        return x.reshape(nb, L, x.shape[-1])[:, L - 1:L, :]

    c_in, n_in, m_in = (c_ref, n_ref, m_ref) if carry else (c0_ref, n0_ref, m0_ref)
    c_out, n_out, m_out = (c_ref, n_ref, m_ref) if carry else (cout_ref, nout_ref, mout_ref)
    heads = range(ML_H)
    sls = [slice(h * ML_HD, (h + 1) * ML_HD) for h in heads]
    cmat = {(h, j): c_in[j, h] for h in heads for j in range(nb)}
    nvec = [n_in[:, h:h + 1, :] for h in heads]
    m_prev = [m_in[:, h:h + 1, 0:1] for h in heads]

    ch = [cc[:, sls[h]] for h in heads]
    q = [_dot(ch[h], wq_ref[h]) for h in heads]
    k = [_dot(ch[h], wk_ref[h]) * (ML_HD ** -0.5) for h in heads]
    v = [zv[:, sls[h]] for h in heads]
    qk = [_dot_nt(q[h], k[h]) for h in heads]
    qc = []
    for h in heads:
        parts = [_dot_nt(q[h][j * L:(j + 1) * L], cmat[(h, j)]) for j in range(nb)]
        qc.append(parts[0] if nb == 1 else jnp.concatenate(parts, axis=0))

    b_col = [bcum_col[:, ML_H + h:ML_H + h + 1] for h in heads]
    li_col = [gate_col[:, h:h + 1] for h in heads]
    m_t, s, sc = [], [], []
    for h in heads:
        log_d = jnp.where(causal, b_col[h] - bcum_row[ML_H + h:ML_H + h + 1, :] + gate_row[h:h + 1, :], -jnp.inf)
        inter = b_col[h] + per_row(m_prev[h])
        mt = jnp.maximum(jnp.max(log_d, axis=-1, keepdims=True), inter)
        m_t.append(mt)
        s.append(qk[h] * jnp.exp(log_d - mt))
        sc.append(jnp.exp(inter - mt))

    for h in heads:
        num = _dot(s[h], v[h]) + sc[h] * qc[h]
        den = (jnp.sum(s[h], axis=-1, keepdims=True)
               + sc[h] * jnp.sum(q[h] * per_row(nvec[h]), axis=-1, keepdims=True))
        hh = num / jnp.maximum(jnp.abs(den), jnp.exp(-m_t[h]))
        hc = hh - jnp.mean(hh, axis=-1, keepdims=True)
        hn = hc * lax.rsqrt(jnp.mean(hc * hc, axis=-1, keepdims=True) + 1e-6) * ng_ref[:, sls[h]]
        y = jax.nn.sigmoid(zo[:, sls[h]]) * (hn + skip_ref[:, sls[h]] * ch[h])
        y_ref[:, :, sls[h]] = y.reshape(nb, L, ML_HD).astype(BF16)

    for h in heads:
        m_new = last(m_t[h])
        b_last = last(b_col[h])
        wj = jnp.exp(per_row(b_last - m_new) - b_col[h] + li_col[h])
        dec = jnp.exp(b_last + m_prev[h] - m_new)
        wv = wj * v[h]
        for j in range(nb):
            js = slice(j * L, (j + 1) * L)
            c_out[j, h] = dec[j] * cmat[(h, j)] + _dot_tn(wv[js], k[h][js])
        n_out[:, h:h + 1, :] = dec * nvec[h] + jnp.sum((wj * k[h]).reshape(nb, L, ML_HD), axis=1, keepdims=True)
        m_out[:, h:h + 1, :] = jnp.broadcast_to(m_new, (nb, 1, ML_HD))

    if carry:
        @pl.when(ci == pl.num_programs(1) - 1)
        def _():
            cout_ref[...] = c_ref[...]
            nout_ref[...] = n_ref[...]
            mout_ref[...] = m_ref[...]


def _mlstm(z3, zift, buf, c_all, n_all, m_all, p, l, nb, L):
    bsz, tlen, _ = z3.shape
    c = MIX_W
    nc = tlen // L
    rows = nb * L
    carry = nc > 1
    bcol = jnp.broadcast_to(p["mlstm_bif"][l][:, None], (8, rows))
    vec = lambda: pl.BlockSpec((None, 1, c), lambda b, i: (l, 0, 0))
    zspec = lambda blk: pl.BlockSpec((nb, L, c), lambda b, i, blk=blk: (b, i, blk))
    hmat = lambda: pl.BlockSpec((None, ML_H, ML_HD, ML_HD), lambda b, i: (l, 0, 0, 0))
    cspec = lambda: pl.BlockSpec((None, nb, ML_H, ML_HD, ML_HD), lambda b, i: (l, b, 0, 0, 0))
    nspec = lambda: pl.BlockSpec((None, nb, ML_H, ML_HD), lambda b, i: (l, b, 0, 0))
    kern = functools.partial(_mlstm_kernel, nb=nb, L=L, carry=carry)
    tiny = (1, 8, 128)
    return pl.pallas_call(
        kern,
        grid=(bsz // nb, nc),
        in_specs=[zspec(ZB_MX), zspec(ZB_MV), zspec(ZB_MO),
                  pl.BlockSpec((nb, L, 128), lambda b, i: (b, i, Z_MIF // 128)),
                  pl.BlockSpec((None, 8, rows), lambda b, i: (b * nc + i, 0, 0)),
                  pl.BlockSpec((None, nb, CONV_W - 1, c), lambda b, i: (l, b, 0, 0)),
                  cspec(), nspec(), nspec(),
                  pl.BlockSpec((None, CONV_W, c), lambda b, i: (l, 0, 0)),
                  vec(), hmat(), hmat(),
                  pl.BlockSpec((None, 1, 128), lambda b, i: (l, 0, 0)),
                  pl.BlockSpec((8, rows), lambda b, i: (0, 0)),
                  vec(), vec()],
        out_specs=[pl.BlockSpec((nb, L, c), lambda b, i: (b, i, 0)), cspec(), nspec(), nspec()],
        out_shape=[jax.ShapeDtypeStruct((bsz, tlen, c), BF16),
                   jax.ShapeDtypeStruct(c_all.shape, F32),
                   jax.ShapeDtypeStruct(n_all.shape, F32),
                   jax.ShapeDtypeStruct(m_all.shape, F32)],
        input_output_aliases={6: 1, 7: 2, 8: 3},
        scratch_shapes=[pltpu.VMEM((nb, 8 + L, c), F32),
                        pltpu.VMEM((nb, ML_H, ML_HD, ML_HD) if carry else tiny, F32),
                        pltpu.VMEM((nb, ML_H, ML_HD) if carry else tiny, F32),
                        pltpu.VMEM((nb, ML_H, ML_HD) if carry else tiny, F32)],
        compiler_params=_cparams(("parallel", "arbitrary")),
        name="mlstm",
    )(z3, z3, z3, z3, zift, buf, c_all, n_all, m_all,
      p["mlstm_conv_w"], p["mlstm_conv_b"], p["mlstm_wq"], p["mlstm_wk"], p["mlstm_brow"], bcol,
      p["mlstm_norm_g"], p["mlstm_skip"])


def _block_diag(w):
    dp, nb, d, _ = w.shape
    eye = jnp.eye(nb, dtype=w.dtype)
    return jnp.einsum("lnij,nm->lnimj", w, eye).reshape(dp, nb * d, nb * d)


PACK_W = 512
PACK_ROWS = 1024
SRC_TAIL = 3584
SRC_ML = SRC_TAIL + RW_TAIL
SRC_MIF = 5280


def _shifted(prev, cur, off):
    return jnp.concatenate([prev[:, off:], cur[:, :off]], axis=1)


def _pack_gate_kernel(src_ref, out_ref, prev_ref):
    cur = src_ref[...]

    @pl.when(pl.program_id(2) > 0)
    def _():
        out_ref[...] = _shifted(prev_ref[...], cur, P_SRC % PACK_W).astype(BF16)

    prev_ref[...] = cur


def _pack_gate(w_in):
    first = P_SRC // PACK_W
    nt = N_GATE // PACK_W
    return pl.pallas_call(
        _pack_gate_kernel,
        grid=(DEPTH, D_MODEL // PACK_ROWS, nt + 1),
        in_specs=[pl.BlockSpec((None, PACK_ROWS, PACK_W), lambda l, r, s: (l, r, first + s))],
        out_specs=pl.BlockSpec((None, PACK_ROWS, PACK_W), lambda l, r, s: (l, r, jnp.maximum(s - 1, 0))),
        out_shape=jax.ShapeDtypeStruct((DEPTH, D_MODEL, N_GATE), BF16),
        scratch_shapes=[pltpu.VMEM((PACK_ROWS, PACK_W), F32)],
        compiler_params=_cparams(("parallel", "parallel", "arbitrary")),
        name="pack_gate",
    )(w_in)


def _pack_mix_kernel(src_ref, out_ref, prev_ref, tail_ref):
    s = pl.program_id(2)
    n_plain = SRC_TAIL // PACK_W
    n_shift = 3 * MIX_W // PACK_W
    cur = src_ref[...]

    @pl.when(s < n_plain)
    def _():
        out_ref[...] = cur.astype(BF16)

    @pl.when(s == n_plain)
    def _():
        tail_ref[...] = cur[:, 0:RW_TAIL_PAD]

    @pl.when((s > n_plain) & (s <= n_plain + n_shift))
    def _():
        out_ref[...] = _shifted(prev_ref[...], cur, SRC_ML % PACK_W).astype(BF16)

    @pl.when(s == n_plain + n_shift + 1)
    def _():
        rows = cur.shape[0]
        mif = SRC_MIF % PACK_W
        tile = jnp.concatenate(
            [tail_ref[:, 0:RW_TAIL], jnp.zeros((rows, RW_TAIL_PAD - RW_TAIL), F32),
             cur[:, mif:mif + 8], jnp.zeros((rows, PACK_W - RW_TAIL_PAD - 8), F32)], axis=1)
        out_ref[...] = tile.astype(BF16)

    prev_ref[...] = cur


def _pack_mix(w_in):
    n_plain = SRC_TAIL // PACK_W
    n_shift = 3 * MIX_W // PACK_W
    last_src = SRC_MIF // PACK_W
    steps = n_plain + n_shift + 2

    def src_idx(l, r, s):
        return (l, r, jnp.minimum(s, last_src))

    def out_idx(l, r, s):
        return (l, r, jnp.where(s <= n_plain, s, s - 1))

    return pl.pallas_call(
        _pack_mix_kernel,
        grid=(DEPTH, D_MODEL // PACK_ROWS, steps),
        in_specs=[pl.BlockSpec((None, PACK_ROWS, PACK_W), src_idx)],
        out_specs=pl.BlockSpec((None, PACK_ROWS, PACK_W), out_idx),
        out_shape=jax.ShapeDtypeStruct((DEPTH, D_MODEL, Z_COLS), BF16),
        scratch_shapes=[pltpu.VMEM((PACK_ROWS, PACK_W), F32), pltpu.VMEM((PACK_ROWS, RW_TAIL_PAD), F32)],
        compiler_params=_cparams(("parallel", "parallel", "arbitrary")),
        name="pack_mix",
    )(w_in)


def _prepare(raw):
    p = {}
    w_in = raw["w_in"]
    p["w_mix"] = _pack_mix(w_in)
    p["w_gate"] = _pack_gate(w_in)
    for name in ("w_branch", "w_out", "w_ff_in", "w_ff_out"):
        p[name] = raw[name].astype(BF16)
    row = lambda a: a.reshape(DEPTH, 1, -1)
    for name in ("norm_pre_mix", "norm_post_mix", "norm_pre_ffn", "norm_post_ffn",
                 "lru_conv_b", "lru_ba", "lru_bx", "lru_lambda", "gmlp_ln_g", "gmlp_ln_b",
                 "rwkv_w0", "rwkv_a0", "rwkv_kk", "rwkv_ka", "rwkv_rk", "rwkv_ln_g", "rwkv_ln_b",
                 "mlstm_conv_b", "mlstm_norm_g", "mlstm_skip"):
        p[name] = row(raw[name])
    p["lru_conv_w"] = raw["lru_conv_w"]
    p["mlstm_conv_w"] = raw["mlstm_conv_w"]
    p["lru_wa"] = _block_diag(raw["lru_wa"]).astype(BF16)
    p["lru_wx"] = _block_diag(raw["lru_wx"]).astype(BF16)
    mu = raw["rwkv_mu"]
    p["rwkv_mu_rkv"] = row(mu[:, :RW_RKV])
    p["rwkv_mu_tail"] = row(jnp.pad(mu[:, RW_RKV:], ((0, 0), (0, RW_TAIL_PAD - RW_TAIL))))

    def lora(w, lo):
        return jnp.pad(w, ((0, 0), (lo, RW_TAIL_PAD - lo - w.shape[1]), (0, 0))).astype(BF16)

    p["rwkv_w2"] = lora(raw["rwkv_w2"], 0)
    p["rwkv_a2"] = lora(raw["rwkv_a2"], 32)
    p["rwkv_g2"] = lora(raw["rwkv_g2"], 64)
    p["mlstm_wq"] = raw["mlstm_wq"].astype(BF16)
    p["mlstm_wk"] = raw["mlstm_wk"].astype(BF16)
    bif = jnp.concatenate([raw["mlstm_bi"], raw["mlstm_bf"]], axis=-1)
    p["mlstm_bif"] = bif
    p["mlstm_brow"] = row(jnp.pad(bif, ((0, 0), (0, 128 - 8))))
    p["gmlp_ws"] = raw["gmlp_ws"]
    p["gmlp_bs"] = raw["gmlp_bs"]
    return p


def _gmlp_mix_weights(p, tlen):
    L = min(GM_CHUNK, tlen)
    rep = GM_CHUNK // L
    ws = jnp.tril(p["gmlp_ws"][:, :, :L, :L])
    eye = jnp.eye(rep, dtype=ws.dtype)
    ws_mix = jnp.einsum("lgps,ab->lgapbs", ws, eye).reshape(DEPTH, GM_GROUPS, GM_CHUNK, GM_CHUNK).astype(BF16)
    bias = jnp.swapaxes(p["gmlp_bs"][:, :, :L], 1, 2)
    bias = jnp.repeat(bias, MIX_W // GM_GROUPS, axis=2)
    bias = jnp.tile(bias, (1, rep, 1))
    return ws_mix, bias


def _group_forward(x3, states, p, is_start, depth=DEPTH):
    bsz, tlen, _ = x3.shape
    m = bsz * tlen
    x = x3.reshape(m, D_MODEL)
    lru_buf, lru_h, rw_shift, rw_s, ml_buf, ml_c, ml_n, ml_m = states
    tm = min(TM_DENSE, m)
    if is_start:
        lru_bb, lru_tt, rw_nb, rw_l, ml_nb, ml_l = 1, 512, min(RW_SEQS_P, bsz), RW_CHUNK, 1, ML_CHUNK_P
    else:
        lru_bb, lru_tt, rw_nb, rw_l, ml_nb, ml_l = 16, tlen, RW_ROWS // tlen, tlen, ML_ROWS // tlen, tlen
    ws_mix, gm_bias = _gmlp_mix_weights(p, tlen)
    nl = lru_h.shape[0]
    h_all = lru_h.reshape(nl, bsz, 1, MIX_W)
    sh_rkv = rw_shift[:, :, :RW_RKV].reshape(nl, bsz, 1, RW_RKV)
    sh_tail = jnp.pad(rw_shift[:, :, RW_RKV:], ((0, 0), (0, 0), (0, RW_TAIL_PAD - RW_TAIL)))
    sh_tail = sh_tail.reshape(nl, bsz, 1, RW_TAIL_PAD)
    s_all, c_all, n_all = rw_s, ml_c, ml_n
    m_all = jnp.broadcast_to(ml_m[:, :, :, None], (nl, bsz, ML_H, ML_HD))
    new_states = [[] for _ in range(8)]
    gm_vs = []
    for l in range(depth):
        z, hn = _inproj(x, p["norm_pre_mix"], p["w_mix"], l, tm)
        z3 = z.reshape(bsz, tlen, Z_COLS)

        y_a, h_all = _lru(z3, lru_buf, h_all, p, l, is_start, lru_bb, lru_tt)
        y_b, gm_v = _gmlp(z, p, l, ws_mix, gm_bias)
        y_c, s_all = _rwkv(z3, sh_rkv, sh_tail, s_all, p, l, rw_nb, rw_l)
        y_c = y_c.reshape(m, MIX_W)
        ml_rows = ml_nb * ml_l
        zift = jnp.swapaxes(z[:, Z_MIF:Z_MIF + 8].reshape(m // ml_rows, ml_rows, 8), 1, 2)
        y_d, c_all, n_all, m_all = _mlstm(z3, zift, ml_buf, c_all, n_all, m_all, p, l, ml_nb, ml_l)
        y_d = y_d.reshape(m, MIX_W)

        merged = _merge(hn, (y_a.reshape(m, MIX_W), y_b, y_c, y_d), p["w_gate"], p["w_branch"], l, tm)
        x1, hf = _outproj(merged, p["w_out"], x, p["norm_post_mix"], p["norm_pre_ffn"], l, min(TM_OUTPROJ, m))
        hmid = _ffup(hf, p["w_ff_in"], l, tm)
        x = _ffdown(hmid, p["w_ff_out"], x1, p["norm_post_ffn"], l, min(TM_FFDOWN, m))

        new_states[0].append(z3[:, tlen - (CONV_W - 1):, 0:MIX_W])
        new_states[2].append(jnp.concatenate(
            [z3[:, tlen - 1, ZB_R * MIX_W:ZB_R * MIX_W + RW_RKV], z3[:, tlen - 1, Z_TAIL:Z_TAIL + RW_TAIL]], axis=-1))
        new_states[4].append(z3[:, tlen - (CONV_W - 1):, ZB_MX * MIX_W:(ZB_MX + 1) * MIX_W])
        gm_vs.append(gm_v.reshape(bsz, tlen, MIX_W))
    out_states = [jnp.stack(new_states[0], axis=0), h_all.reshape(nl, bsz, MIX_W)[:depth],
                  jnp.stack(new_states[2], axis=0), s_all[:depth],
                  jnp.stack(new_states[4], axis=0), c_all[:depth], n_all[:depth], m_all[:depth, :, :, 0]]
    return x.reshape(bsz, tlen, D_MODEL), out_states, jnp.stack(gm_vs, axis=0)


def kernel(x_prompt, x_sample, state_lru_conv, state_lru_h, state_rwkv_shift, state_rwkv_wkv, state_mlstm_conv, state_mlstm_C, state_mlstm_n, state_mlstm_m, norm_pre_mix, norm_post_mix, norm_pre_ffn, norm_post_ffn, w_in, lru_conv_w, lru_conv_b, lru_wa, lru_ba, lru_wx, lru_bx, lru_lambda, gmlp_ln_g, gmlp_ln_b, gmlp_ws, gmlp_bs, rwkv_mu, rwkv_w0, rwkv_w2, rwkv_a0, rwkv_a2, rwkv_g2, rwkv_kk, rwkv_ka, rwkv_rk, rwkv_ln_g, rwkv_ln_b, mlstm_conv_w, mlstm_conv_b, mlstm_wq, mlstm_wk, mlstm_bi, mlstm_bf, mlstm_norm_g, mlstm_skip, w_branch, w_out, w_ff_in, w_ff_out):
    raw = dict(norm_pre_mix=norm_pre_mix, norm_post_mix=norm_post_mix, norm_pre_ffn=norm_pre_ffn,
               norm_post_ffn=norm_post_ffn, w_in=w_in, lru_conv_w=lru_conv_w, lru_conv_b=lru_conv_b,
               lru_wa=lru_wa, lru_ba=lru_ba, lru_wx=lru_wx, lru_bx=lru_bx, lru_lambda=lru_lambda,
               gmlp_ln_g=gmlp_ln_g, gmlp_ln_b=gmlp_ln_b, gmlp_ws=gmlp_ws, gmlp_bs=gmlp_bs,
               rwkv_mu=rwkv_mu, rwkv_w0=rwkv_w0, rwkv_w2=rwkv_w2, rwkv_a0=rwkv_a0, rwkv_a2=rwkv_a2,
               rwkv_g2=rwkv_g2, rwkv_kk=rwkv_kk, rwkv_ka=rwkv_ka, rwkv_rk=rwkv_rk, rwkv_ln_g=rwkv_ln_g,
               rwkv_ln_b=rwkv_ln_b, mlstm_conv_w=mlstm_conv_w, mlstm_conv_b=mlstm_conv_b, mlstm_wq=mlstm_wq,
               mlstm_wk=mlstm_wk, mlstm_bi=mlstm_bi, mlstm_bf=mlstm_bf, mlstm_norm_g=mlstm_norm_g,
               mlstm_skip=mlstm_skip, w_branch=w_branch, w_out=w_out, w_ff_in=w_ff_in, w_ff_out=w_ff_out)
    p = _prepare(raw)
    bp = x_prompt.shape[0]
    zero = lambda *s: jnp.zeros((DEPTH, bp) + s, F32)
    prompt_states = (zero(CONV_W - 1, MIX_W), zero(MIX_W), zero(RW_RKV + RW_TAIL), zero(RW_H, RW_HD, RW_HD),
                     zero(CONV_W - 1, MIX_W), zero(ML_H, ML_HD, ML_HD), zero(ML_H, ML_HD), zero(ML_H))
    sample_states = (state_lru_conv, state_lru_h, state_rwkv_shift, state_rwkv_wkv,
                     state_mlstm_conv, state_mlstm_C, state_mlstm_n, state_mlstm_m)
    yp, st_p, _ = _group_forward(x_prompt, prompt_states, p, True)
    ys, st_s, gm_v = _group_forward(x_sample, sample_states, p, False)
    return (yp, ys, *st_p, *st_s, gm_v)
```

```python
import functools

import numpy as np
import jax
import jax.numpy as jnp
from jax import lax
from jax.experimental import pallas as pl
from jax.experimental.pallas import tpu as pltpu

F32 = jnp.float32
BF16 = jnp.bfloat16

D_MODEL = 2048
DEPTH = 4
MIX_W = 512
CONV_W = 4
LRU_BLOCKS = 8
LRU_C = 8.0
GM_CHUNK = 128
GM_GROUPS = 4
RW_HD = 64
RW_H = 8
RW_RKV = 3 * MIX_W
RW_TAIL = 160
RW_TAIL_PAD = 256
RW_LN_EPS = 64e-5
ML_H = 4
ML_HD = 128
D_FF = 5632
N_GATE = 4 * D_MODEL
P_SRC = 5288
Z_COLS = 5632

ZB_LX, ZB_LG, ZB_GU, ZB_GV, ZB_R, ZB_K, ZB_V, ZB_MX, ZB_MV, ZB_MO = range(10)
Z_TAIL = 5120
Z_MIF = 5376

RW_CHUNK = 64
RW_SEQS_P = 2
ML_CHUNK_P = 128
GM_TILE_ROWS = 512
TM_DENSE = 1024
TM_OUTPROJ = 512
OUTPROJ_SUB_ROWS = 256
FFUP_SUB_ROWS = 256
FFDOWN_SUB_ROWS = 128
TM_FFDOWN = 512
TK_FFDOWN = 1408
VMEM_LIMIT = 56 * 1024 * 1024
HI = lax.Precision.HIGHEST


def _cparams(sem):
    return pltpu.CompilerParams(dimension_semantics=sem, vmem_limit_bytes=VMEM_LIMIT)


def _softplus(x):
    return jnp.maximum(x, 0.0) + jnp.log1p(jnp.exp(-jnp.abs(x)))


def _log_sigmoid(x):
    return -_softplus(-x)


def _rms(x, g):
    return x * lax.rsqrt(jnp.mean(x * x, axis=-1, keepdims=True) + 1e-6) * g


def _dot(a, b):
    return jnp.dot(a.astype(BF16), b.astype(BF16), preferred_element_type=F32)


def _nt(a, b):
    return lax.dot_general(a, b, (((1,), (1,)), ((), ())), preferred_element_type=F32)


def _dot_nt(a, b):
    return lax.dot_general(a.astype(BF16), b.astype(BF16), (((1,), (1,)), ((), ())),
                           preferred_element_type=F32)


def _dot_tn(a, b):
    return lax.dot_general(a.astype(BF16), b.astype(BF16), (((0,), (0,)), ((), ())),
                           preferred_element_type=F32)


def _dot_hi(a, b):
    return jnp.dot(a, b, preferred_element_type=F32, precision=HI)


def _inproj_kernel(x_ref, g_ref, w_ref, z_ref, hn_ref):
    @pl.when(pl.program_id(1) == 0)
    def _():
        hn_ref[...] = _rms(x_ref[...], g_ref[...]).astype(BF16)

    z_ref[...] = _nt(hn_ref[...], w_ref[...])


def _inproj(x, g, w, l, tm, tn=512):
    m = x.shape[0]
    return pl.pallas_call(
        _inproj_kernel,
        grid=(m // tm, Z_COLS // tn),
        in_specs=[pl.BlockSpec((tm, D_MODEL), lambda i, j: (i, 0)),
                  pl.BlockSpec((None, 1, D_MODEL), lambda i, j: (l, 0, 0)),
                  pl.BlockSpec((None, tn, D_MODEL), lambda i, j: (l, j, 0))],
        out_specs=[pl.BlockSpec((tm, tn), lambda i, j: (i, j)),
                   pl.BlockSpec((tm, D_MODEL), lambda i, j: (i, 0))],
        out_shape=[jax.ShapeDtypeStruct((m, Z_COLS), F32),
                   jax.ShapeDtypeStruct((m, D_MODEL), BF16)],
        compiler_params=_cparams(("parallel", "arbitrary")),
        name="inproj",
    )(x, g, w)


def _merge_kernel(hn_ref, ya_ref, yb_ref, yc_ref, yd_ref, g0_ref, g1_ref, g2_ref, g3_ref, wb_ref, o_ref):
    hn = hn_ref[...]
    acc = None
    for b, (y_ref, wg_ref) in enumerate(((ya_ref, g0_ref), (yb_ref, g1_ref), (yc_ref, g2_ref), (yd_ref, g3_ref))):
        zg = _nt(hn, wg_ref[...])
        br = jnp.dot(y_ref[...], wb_ref[b], preferred_element_type=F32)
        term = jax.nn.sigmoid(zg) * br
        acc = term if acc is None else acc + term
    o_ref[...] = acc.astype(BF16)


def _merge(hn, ys, wgate, wbranch, l, tm, tn=256):
    m = hn.shape[0]
    nb = D_MODEL // tn
    y_spec = pl.BlockSpec((tm, MIX_W), lambda j, i: (i, 0))
    g_specs = [pl.BlockSpec((None, tn, D_MODEL), lambda j, i, b=b: (l, b * nb + j, 0)) for b in range(4)]
    return pl.pallas_call(
        _merge_kernel,
        grid=(nb, m // tm),
        in_specs=[pl.BlockSpec((tm, D_MODEL), lambda j, i: (i, 0)), y_spec, y_spec, y_spec, y_spec,
                  *g_specs,
                  pl.BlockSpec((None, 4, MIX_W, tn), lambda j, i: (l, 0, 0, j))],
        out_specs=pl.BlockSpec((tm, tn), lambda j, i: (i, j)),
        out_shape=jax.ShapeDtypeStruct((m, D_MODEL), BF16),
        compiler_params=_cparams(("parallel", "arbitrary")),
        name="merge",
    )(hn, *ys, wgate, wgate, wgate, wgate, wbranch)


def _outproj_kernel(mg_ref, w_ref, x_ref, gpost_ref, gpre_ref, x1_ref, hf_ref):
    sub = OUTPROJ_SUB_ROWS
    for r0 in range(0, mg_ref.shape[0], sub):
        rs = slice(r0, r0 + sub)
        mix = jnp.dot(mg_ref[rs, :], w_ref[...], preferred_element_type=F32)
        x1 = x_ref[rs, :] + _rms(mix, gpost_ref[...])
        x1_ref[rs, :] = x1
        hf_ref[rs, :] = _rms(x1, gpre_ref[...]).astype(BF16)


def _outproj(merged, w_out, x, g_post, g_pre_ffn, l, tm):
    m = x.shape[0]
    g_spec = pl.BlockSpec((None, 1, D_MODEL), lambda i: (l, 0, 0))
    row_spec = pl.BlockSpec((tm, D_MODEL), lambda i: (i, 0))
    return pl.pallas_call(
        _outproj_kernel,
        grid=(m // tm,),
        in_specs=[row_spec, pl.BlockSpec((None, D_MODEL, D_MODEL), lambda i: (l, 0, 0)), row_spec, g_spec, g_spec],
        out_specs=[row_spec, row_spec],
        out_shape=[jax.ShapeDtypeStruct((m, D_MODEL), F32), jax.ShapeDtypeStruct((m, D_MODEL), BF16)],
        compiler_params=_cparams(("parallel",)),
        name="outproj",
    )(merged, w_out, x, g_post, g_pre_ffn)


def _ffup_kernel(hf_ref, wg_ref, wu_ref, o_ref, wgb_ref, wub_ref):
    @pl.when(pl.program_id(1) == 0)
    def _():
        wgb_ref[...] = wg_ref[...].astype(BF16)
        wub_ref[...] = wu_ref[...].astype(BF16)

    sub = min(FFUP_SUB_ROWS, hf_ref.shape[0])
    for r0 in range(0, hf_ref.shape[0], sub):
        rs = slice(r0, r0 + sub)
        hf = hf_ref[rs, :]
        g = jnp.dot(hf, wgb_ref[...], preferred_element_type=F32)
        u = jnp.dot(hf, wub_ref[...], preferred_element_type=F32)
        o_ref[rs, :] = (jax.nn.silu(g) * u).astype(BF16)


def _ffup(hf, w_ff_in, l, tm, tn=512):
    m = hf.shape[0]
    nb = D_FF // tn
    return pl.pallas_call(
        _ffup_kernel,
        grid=(nb, m // tm),
        in_specs=[pl.BlockSpec((tm, D_MODEL), lambda j, i: (i, 0)),
                  pl.BlockSpec((None, D_MODEL, tn), lambda j, i: (l, 0, j)),
                  pl.BlockSpec((None, D_MODEL, tn), lambda j, i: (l, 0, nb + j))],
        out_specs=pl.BlockSpec((tm, tn), lambda j, i: (i, j)),
        out_shape=jax.ShapeDtypeStruct((m, D_FF), BF16),
        scratch_shapes=[pltpu.VMEM((D_MODEL, tn), BF16), pltpu.VMEM((D_MODEL, tn), BF16)],
        compiler_params=_cparams(("parallel", "arbitrary")),
        name="ffup",
    )(hf, w_ff_in, w_ff_in)


def _ffdown_kernel(h_ref, w_ref, x1_ref, g_ref, o_ref, acc_ref):
    k = pl.program_id(1)
    last = pl.num_programs(1) - 1

    @pl.when(k == 0)
    def _():
        acc_ref[...] = jnp.dot(h_ref[...], w_ref[...], preferred_element_type=F32)

    @pl.when((k > 0) & (k < last))
    def _():
        acc_ref[...] += jnp.dot(h_ref[...], w_ref[...], preferred_element_type=F32)

    @pl.when(k == last)
    def _():
        sub = FFDOWN_SUB_ROWS
        for r0 in range(0, acc_ref.shape[0], sub):
            rs = slice(r0, r0 + sub)
            ff = acc_ref[rs, :] + jnp.dot(h_ref[rs, :], w_ref[...], preferred_element_type=F32)
            o_ref[rs, :] = x1_ref[rs, :] + _rms(ff, g_ref[...])


def _ffdown(h, w_ff_out, x1, g_post, l, tm, tk=TK_FFDOWN):
    m = x1.shape[0]
    row_spec = pl.BlockSpec((tm, D_MODEL), lambda i, k: (i, 0))
    return pl.pallas_call(
        _ffdown_kernel,
        grid=(m // tm, D_FF // tk),
        in_specs=[pl.BlockSpec((tm, tk), lambda i, k: (i, k)),
                  pl.BlockSpec((None, tk, D_MODEL), lambda i, k: (l, k, 0)),
                  row_spec,
                  pl.BlockSpec((None, 1, D_MODEL), lambda i, k: (l, 0, 0))],
        out_specs=row_spec,
        out_shape=jax.ShapeDtypeStruct((m, D_MODEL), F32),
        scratch_shapes=[pltpu.VMEM((tm, D_MODEL), F32)],
        compiler_params=_cparams(("parallel", "arbitrary")),
        name="ffdown",
    )(h, w_ff_out, x1, g_post)


def _lru_kernel(zx_ref, zg_ref, buf_ref, h0_ref, cw_ref, cb_ref, wa_ref, ba_ref, wx_ref, bx_ref, lam_ref,
                y_ref, hout_ref, xs_ref, a_ref, b_ref, h_ref, *, is_start, bb, tt):
    t = pl.program_id(1)
    c = MIX_W

    @pl.when(t == 0)
    def _():
        xs_ref[:, 5:8, :] = buf_ref[...]
        h_ref[...] = jnp.broadcast_to(h0_ref[...], (bb, 8, c))

    @pl.when(t > 0)
    def _():
        xs_ref[:, 5:8, :] = xs_ref[:, tt + 5:tt + 8, :]

    xs_ref[:, 8:8 + tt, :] = zx_ref[...]
    xc = cb_ref[...] + cw_ref[0:1, :] * xs_ref[:, 5:5 + tt, :]
    for j in range(1, CONV_W):
        xc = xc + cw_ref[j:j + 1, :] * xs_ref[:, 5 + j:5 + j + tt, :]
    xc2 = xc.reshape(bb * tt, c)
    r = jax.nn.sigmoid(_dot(xc2, wa_ref[...]) + ba_ref[...])
    i = jax.nn.sigmoid(_dot(xc2, wx_ref[...]) + bx_ref[...])
    log_a = LRU_C * r * _log_sigmoid(lam_ref[...])
    a = jnp.exp(log_a)
    mult = jnp.sqrt(1.0 - jnp.exp(2.0 * log_a))
    if is_start:
        tpos = lax.broadcasted_iota(jnp.int32, (bb, tt, c), 1).reshape(bb * tt, c) + t * tt
        mult = jnp.where(tpos == 0, 1.0, mult)
    a_ref[...] = a.reshape(bb, tt, c)
    b_ref[...] = (mult * i * xc2).reshape(bb, tt, c)

    row = lax.broadcasted_iota(jnp.int32, (bb, 8, c), 1).reshape(bb * 8, c)

    def group(gi, carry):
        off = pl.multiple_of(gi * 8, 8)
        av = a_ref[:, pl.ds(off, 8), :].reshape(bb * 8, c)
        bv = b_ref[:, pl.ds(off, 8), :].reshape(bb * 8, c)
        for s in (1, 2, 4):
            keep = row >= s
            a_sh = pltpu.roll(av, s, 0)
            b_sh = pltpu.roll(bv, s, 0)
            bv = jnp.where(keep, av * b_sh + bv, bv)
            av = jnp.where(keep, av * a_sh, av)
        hh = (av * h_ref[...].reshape(bb * 8, c) + bv).reshape(bb, 8, c)
        b_ref[:, pl.ds(off, 8), :] = hh
        h_ref[...] = jnp.broadcast_to(hh[:, 7:8, :], (bb, 8, c))
        return carry

    lax.fori_loop(0, tt // 8, group, 0)
    y_ref[...] = (b_ref[...] * jax.nn.gelu(zg_ref[...])).astype(BF16)

    @pl.when(t == pl.num_programs(1) - 1)
    def _():
        hout_ref[...] = h_ref[:, 7:8, :]


def _lru(z3, buf, h_all, p, l, is_start, bb, tt):
    bsz, tlen, _ = z3.shape
    c = MIX_W
    vec = lambda: pl.BlockSpec((None, 1, c), lambda b, t: (l, 0, 0))
    mat = lambda: pl.BlockSpec((None, c, c), lambda b, t: (l, 0, 0))
    hspec = lambda: pl.BlockSpec((None, bb, 1, c), lambda b, t: (l, b, 0, 0))
    kern = functools.partial(_lru_kernel, is_start=is_start, bb=bb, tt=tt)
    return pl.pallas_call(
        kern,
        grid=(bsz // bb, tlen // tt),
        in_specs=[pl.BlockSpec((bb, tt, c), lambda b, t: (b, t, ZB_LX)),
                  pl.BlockSpec((bb, tt, c), lambda b, t: (b, t, ZB_LG)),
                  pl.BlockSpec((None, bb, CONV_W - 1, c), lambda b, t: (l, b, 0, 0)),
                  hspec(),
                  pl.BlockSpec((None, CONV_W, c), lambda b, t: (l, 0, 0)),
                  vec(), mat(), vec(), mat(), vec(), vec()],
        out_specs=[pl.BlockSpec((bb, tt, c), lambda b, t: (b, t, 0)), hspec()],
        out_shape=[jax.ShapeDtypeStruct((bsz, tlen, c), BF16),
                   jax.ShapeDtypeStruct(h_all.shape, F32)],
        input_output_aliases={3: 1},
        scratch_shapes=[pltpu.VMEM((bb, 8 + tt, c), F32), pltpu.VMEM((bb, tt, c), F32),
                        pltpu.VMEM((bb, tt, c), F32), pltpu.VMEM((bb, 8, c), F32)],
        compiler_params=_cparams(("parallel", "arbitrary")),
        name="rglru",
    )(z3, z3, buf, h_all, p["lru_conv_w"], p["lru_conv_b"], p["lru_wa"], p["lru_ba"], p["lru_wx"], p["lru_bx"],
      p["lru_lambda"])


def _gmlp_kernel(zu_ref, zv_ref, lng_ref, lnb_ref, ws_ref, bias_ref, y_ref, v_ref):
    u = jax.nn.gelu(zu_ref[...])
    gv = jax.nn.gelu(zv_ref[...])
    vc = gv - jnp.mean(gv, axis=-1, keepdims=True)
    v = vc * lax.rsqrt(jnp.mean(vc * vc, axis=-1, keepdims=True) + 1e-5) * lng_ref[...] + lnb_ref[...]
    v_ref[...] = v
    gd = MIX_W // GM_GROUPS
    vb = v.astype(BF16)
    for ck in range(v.shape[0] // GM_CHUNK):
        rs = slice(ck * GM_CHUNK, (ck + 1) * GM_CHUNK)
        for g in range(GM_GROUPS):
            sl = slice(g * gd, (g + 1) * gd)
            s = jnp.dot(ws_ref[g], vb[rs, sl], preferred_element_type=F32) + bias_ref[:, sl]
            y_ref[rs, sl] = (u[rs, sl] * s).astype(BF16)


def _gmlp(z2, p, l, ws_mix, bias_tile):
    m = z2.shape[0]
    c = MIX_W
    rows = GM_TILE_ROWS
    vec = lambda: pl.BlockSpec((None, 1, c), lambda i: (l, 0, 0))
    return pl.pallas_call(
        _gmlp_kernel,
        grid=(m // rows,),
        in_specs=[pl.BlockSpec((rows, c), lambda i: (i, ZB_GU)),
                  pl.BlockSpec((rows, c), lambda i: (i, ZB_GV)),
                  vec(), vec(),
                  pl.BlockSpec((None, GM_GROUPS, GM_CHUNK, GM_CHUNK), lambda i: (l, 0, 0, 0)),
                  pl.BlockSpec((None, GM_CHUNK, c), lambda i: (l, 0, 0))],
        out_specs=[pl.BlockSpec((rows, c), lambda i: (i, 0)),
                   pl.BlockSpec((rows, c), lambda i: (i, 0))],
        out_shape=[jax.ShapeDtypeStruct((m, c), BF16), jax.ShapeDtypeStruct((m, c), F32)],
        compiler_params=_cparams(("parallel",)),
        name="gmlp",
    )(z2, z2, p["gmlp_ln_g"], p["gmlp_ln_b"], ws_mix, bias_tile)


RW_ROWS = 64


def _split_bf16(x):
    hi = x.astype(BF16)
    lo = (x - hi.astype(F32)).astype(BF16)
    return hi, lo


def _dot_inv(a, b):
    a_hi, a_lo = _split_bf16(a)
    b_hi, b_lo = _split_bf16(b)
    d = lambda x, y: jnp.dot(x, y, preferred_element_type=F32)
    return d(a_hi, b_hi) + (d(a_hi, b_lo) + d(a_lo, b_hi))


def _dot_ones(x, ones, pieces, ones_on_left=False):
    acc = None
    rem = x
    for i in range(pieces):
        part = rem.astype(BF16)
        if i + 1 < pieces:
            rem = rem - part.astype(F32)
        term = (jnp.dot(ones, part, preferred_element_type=F32) if ones_on_left
                else jnp.dot(part, ones, preferred_element_type=F32))
        acc = term if acc is None else acc + term
    return acc


def _head_sums(x, ones):
    half = MIX_W // 2
    rows = x.shape[0]
    stacked = jnp.concatenate([x[:, :half], x[:, half:]], axis=0)
    s = _dot_ones(stacked, ones, 2)
    return jnp.concatenate([s[:rows], s[rows:]], axis=1)


def _rwkv_kernel(zr_ref, zk_ref, zv_ref, zt_ref, shr_ref, sht_ref, s0_ref,
                 mur_ref, mut_ref, w0_ref, a0_ref, kkp_ref, kap_ref, rk_ref, lng_ref, lnb_ref,
                 w2_ref, a2_ref, g2_ref, hsum_ref,
                 y_ref, sout_ref, prev_ref, s_ref, *, nb, L, carry):
    ci = pl.program_id(1)
    c = MIX_W
    rows = nb * L
    bb = RW_ROWS // L
    ngrp = rows // RW_ROWS
    log_l = L.bit_length() - 1

    @pl.when(ci == 0)
    def _():
        prev_ref[:, :, 0:RW_RKV] = shr_ref[...]
        prev_ref[:, :, RW_RKV:RW_RKV + RW_TAIL_PAD] = sht_ref[...]
        if carry:
            s_ref[...] = s0_ref[...]

    first = (lax.broadcasted_iota(jnp.int32, (rows, 1), 0) & (L - 1)) == 0

    def shift(z_ref, lo, hi, mu):
        w = hi - lo
        z3 = z_ref[...]
        z = z3.reshape(rows, w)
        prev = jnp.broadcast_to(prev_ref[:, :, lo:hi], (nb, L, w)).reshape(rows, w)
        zp = jnp.where(first, prev, pltpu.roll(z, 1, 0))
        prev_ref[:, :, lo:hi] = z3[:, L - 1:L, :]
        return z + (zp - z) * mu

    r = shift(zr_ref, 0, c, mur_ref[:, 0:c])
    k = shift(zk_ref, c, 2 * c, mur_ref[:, c:2 * c])
    v = shift(zv_ref, 2 * c, 3 * c, mur_ref[:, 2 * c:3 * c])
    tl = shift(zt_ref, RW_RKV, RW_RKV + RW_TAIL_PAD, mut_ref[...])

    wlin = w0_ref[...] + _dot(jnp.tanh(tl), w2_ref[...])
    logw = -jnp.exp(-_softplus(-wlin) - 0.5)
    a = jax.nn.sigmoid(a0_ref[...] + _dot(tl, a2_ref[...]))
    gate = _dot(jax.nn.sigmoid(tl), g2_ref[...])
    kkf = k * kkp_ref[...]
    kmod = k * (1.0 + (a - 1.0) * kap_ref[...])
    hsum = hsum_ref[...]
    kkn = kkf / jnp.maximum(jnp.sqrt(_head_sums(kkf * kkf, hsum)), 1e-12)
    beta = kkn * a
    bonus = _head_sums(r * kmod * rk_ref[...], hsum) * v

    ri_f = lax.broadcasted_iota(jnp.int32, (rows, rows), 0)
    cj_f = lax.broadcasted_iota(jnp.int32, (rows, rows), 1)
    cum_op = (((ri_f >> log_l) == (cj_f >> log_l)) & (cj_f <= ri_f)).astype(BF16)
    cum = _dot_ones(logw, cum_op, 3, ones_on_left=True)
    c_last = cum.reshape(nb, L, c)[:, L - 1:L, :]
    e_last = jnp.exp(c_last)
    e_rest = jnp.exp(jnp.broadcast_to(c_last, (nb, L, c)).reshape(rows, c) - cum)
    e_mc = jnp.exp(-cum)
    a_t = -kkn * jnp.exp(cum - logw)
    r_t = r * jnp.exp(cum)
    b_t = beta * e_mc
    k_t = kmod * e_mc
    b_hat = beta * e_rest
    k_hat = kmod * e_rest

    n = RW_ROWS
    ri = lax.broadcasted_iota(jnp.int32, (n, n), 0)
    cj = lax.broadcasted_iota(jnp.int32, (n, n), 1)
    same = (ri >> log_l) == (cj >> log_l)
    strict = same & (cj < ri)
    incl = same & (cj <= ri)
    eye = (cj == ri).astype(F32)
    lvl = [((ri >> (s + 1)) == (cj >> (s + 1))) & ((ri >> s) != (cj >> s)) & (cj < ri) for s in range(log_l)]

    probs = [(g, h) for g in range(ngrp) for h in range(RW_H)]

    def cut(x, g, h):
        return x[g * n:(g + 1) * n, h * RW_HD:(h + 1) * RW_HD]

    state_in = s_ref if carry else s0_ref
    s0 = {(g, h, j): state_in[g * bb + j, h] for (g, h) in probs for j in range(bb)}
    ar, m_ba, m_ka, m_br, m_kr = {}, {}, {}, {}, {}
    for p in probs:
        ar[p] = jnp.concatenate([cut(a_t, *p), cut(r_t, *p)], axis=0)
        mb = _dot_nt(ar[p], cut(b_t, *p))
        mk = _dot_nt(ar[p], cut(k_t, *p))
        m_ba[p] = jnp.where(strict, mb[0:n], 0.0)
        m_br[p] = jnp.where(incl, mb[n:2 * n], 0.0)
        m_ka[p] = jnp.where(strict, mk[0:n], 0.0)
        m_kr[p] = jnp.where(incl, mk[n:2 * n], 0.0)

    inv = {p: eye + jnp.where(lvl[0], m_ba[p], 0.0) for p in probs}
    for s in range(1, log_l):
        half = {p: _dot_inv(inv[p], jnp.where(lvl[s], m_ba[p], 0.0)) for p in probs}
        inv = {p: inv[p] + _dot_inv(half[p], inv[p]) for p in probs}

    xa, xr = {}, {}
    for (g, h) in probs:
        pa, pr = [], []
        for j in range(bb):
            arj = ar[(g, h)] if bb == 1 else jnp.concatenate(
                [cut(a_t, g, h)[j * L:(j + 1) * L], cut(r_t, g, h)[j * L:(j + 1) * L]], axis=0)
            as0 = _dot_nt(arj, s0[(g, h, j)])
            pa.append(as0[0:L])
            pr.append(as0[L:2 * L])
        xa[(g, h)] = pa[0] if bb == 1 else jnp.concatenate(pa, axis=0)
        xr[(g, h)] = pr[0] if bb == 1 else jnp.concatenate(pr, axis=0)

    vv = {p: cut(v, *p) for p in probs}
    u = {p: _dot_inv(inv[p], xa[p] + _dot(m_ka[p], vv[p])) for p in probs}
    o = {p: xr[p] + _dot(m_br[p], u[p]) + _dot(m_kr[p], vv[p]) for p in probs}

    state_out = s_ref if carry else sout_ref
    for (g, h) in probs:
        bh, kh = cut(b_hat, g, h), cut(k_hat, g, h)
        for j in range(bb):
            js = slice(j * L, (j + 1) * L)
            uv = jnp.concatenate([u[(g, h)][js], vv[(g, h)][js]], axis=0)
            bk = jnp.concatenate([bh[js], kh[js]], axis=0)
            seq = g * bb + j
            state_out[seq, h] = s0[(g, h, j)] * e_last[seq][:, h * RW_HD:(h + 1) * RW_HD] + _dot_tn(uv, bk)

    o_rows = [jnp.concatenate([o[(g, h)] for h in range(RW_H)], axis=1) for g in range(ngrp)]
    o_all = o_rows[0] if ngrp == 1 else jnp.concatenate(o_rows, axis=0)
    oc = o_all - _head_sums(o_all, hsum) * (1.0 / RW_HD)
    on = oc * lax.rsqrt(_head_sums(oc * oc, hsum) * (1.0 / RW_HD) + RW_LN_EPS)
    y = (on * lng_ref[...] + lnb_ref[...] + bonus) * gate
    y_ref[...] = y.reshape(nb, L, c).astype(BF16)

    if carry:
        @pl.when(ci == pl.num_programs(1) - 1)
        def _():
            sout_ref[...] = s_ref[...]


def _rwkv(z3, sh_rkv, sh_tail, s_all, p, l, nb, L):
    bsz, tlen, _ = z3.shape
    c = MIX_W
    nc = tlen // L
    carry = nc > 1
    half = c // 2
    hsum = jnp.asarray(np.kron(np.eye(half // RW_HD, dtype=np.float32), np.ones((RW_HD, RW_HD), np.float32)), BF16)
    vec = lambda w=c: pl.BlockSpec((None, 1, w), lambda b, i: (l, 0, 0))
    lora = lambda: pl.BlockSpec((None, RW_TAIL_PAD, c), lambda b, i: (l, 0, 0))
    zspec = lambda blk: pl.BlockSpec((nb, L, c), lambda b, i, blk=blk: (b, i, blk))
    sspec = lambda: pl.BlockSpec((None, nb, RW_H, RW_HD, RW_HD), lambda b, i: (l, b, 0, 0, 0))
    kern = functools.partial(_rwkv_kernel, nb=nb, L=L, carry=carry)
    return pl.pallas_call(
        kern,
        grid=(bsz // nb, nc),
        in_specs=[zspec(ZB_R), zspec(ZB_K), zspec(ZB_V),
                  pl.BlockSpec((nb, L, RW_TAIL_PAD), lambda b, i: (b, i, Z_TAIL // RW_TAIL_PAD)),
                  pl.BlockSpec((None, nb, 1, RW_RKV), lambda b, i: (l, b, 0, 0)),
                  pl.BlockSpec((None, nb, 1, RW_TAIL_PAD), lambda b, i: (l, b, 0, 0)),
                  sspec(),
                  vec(RW_RKV), vec(RW_TAIL_PAD), vec(), vec(), vec(), vec(), vec(), vec(), vec(),
                  lora(), lora(), lora(),
                  pl.BlockSpec((half, half), lambda b, i: (0, 0))],
        out_specs=[pl.BlockSpec((nb, L, c), lambda b, i: (b, i, 0)), sspec()],
        out_shape=[jax.ShapeDtypeStruct((bsz, tlen, c), BF16),
                   jax.ShapeDtypeStruct(s_all.shape, F32)],
        input_output_aliases={6: 1},
        scratch_shapes=[pltpu.VMEM((nb, 1, RW_RKV + RW_TAIL_PAD), F32),
                        pltpu.VMEM((nb, RW_H, RW_HD, RW_HD) if carry else (1, 1, 8, 128), F32)],
        compiler_params=_cparams(("parallel", "arbitrary")),
        name="rwkv7",
    )(z3, z3, z3, z3, sh_rkv, sh_tail, s_all,
      p["rwkv_mu_rkv"], p["rwkv_mu_tail"], p["rwkv_w0"], p["rwkv_a0"], p["rwkv_kk"], p["rwkv_ka"], p["rwkv_rk"],
      p["rwkv_ln_g"], p["rwkv_ln_b"], p["rwkv_w2"], p["rwkv_a2"], p["rwkv_g2"], hsum)


ML_ROWS = 128


def _mlstm_kernel(zx_ref, zv_ref, zo_ref, zif_ref, zift_ref, buf_ref, c0_ref, n0_ref, m0_ref,
                  cw_ref, cb_ref, wq_ref, wk_ref, brow_ref, bcol_ref, ng_ref, skip_ref,
                  y_ref, cout_ref, nout_ref, mout_ref, xs_ref, c_ref, n_ref, m_ref, *, nb, L, carry):
    ci = pl.program_id(1)
    c = MIX_W
    rows = nb * L
    log_l = L.bit_length() - 1

    @pl.when(ci == 0)
    def _():
        xs_ref[:, 5:8, :] = buf_ref[...]
        if carry:
            c_ref[...] = c0_ref[...]
            n_ref[...] = n0_ref[...]
            m_ref[...] = m0_ref[...]

    if carry:
        @pl.when(ci > 0)
        def _():
            xs_ref[:, 5:8, :] = xs_ref[:, L + 5:L + 8, :]

    xs_ref[:, 8:8 + L, :] = zx_ref[...]
    conv = cb_ref[...] + cw_ref[0:1, :] * xs_ref[:, 5:5 + L, :]
    for j in range(1, CONV_W):
        conv = conv + cw_ref[j:j + 1, :] * xs_ref[:, 5 + j:5 + j + L, :]
    cc = jax.nn.silu(conv).reshape(rows, c)
    zv = zv_ref[...].reshape(rows, c)
    zo = zo_ref[...].reshape(rows, c)

    ri = lax.broadcasted_iota(jnp.int32, (rows, rows), 0)
    cj = lax.broadcasted_iota(jnp.int32, (rows, rows), 1)
    same = (ri >> log_l) == (cj >> log_l)
    causal = same & (cj <= ri)

    gate_col = zif_ref[...].reshape(rows, 128) + brow_ref[...]
    bcum_col = _dot_ones(_log_sigmoid(gate_col), causal.astype(BF16), 3, ones_on_left=True)
    gate_row = zift_ref[...] + bcol_ref[...]
    bcum_row = _dot_ones(_log_sigmoid(gate_row), (same & (ri <= cj)).astype(BF16), 3)

    def per_row(x3):
        return jnp.broadcast_to(x3, (nb, L, x3.shape[-1])).reshape(rows, x3.shape[-1])

    def last(x):
        return x.reshape(nb, L, x.shape[-1])[:, L - 1:L, :]

    c_in, n_in, m_in = (c_ref, n_ref, m_ref) if carry else (c0_ref, n0_ref, m0_ref)
    c_out, n_out, m_out = (c_ref, n_ref, m_ref) if carry else (cout_ref, nout_ref, mout_ref)
    heads = range(ML_H)
    sls = [slice(h * ML_HD, (h + 1) * ML_HD) for h in heads]
    cmat = {(h, j): c_in[j, h] for h in heads for j in range(nb)}
    nvec = [n_in[:, h:h + 1, :] for h in heads]
    m_prev = [m_in[:, h:h + 1, 0:1] for h in heads]

    ch = [cc[:, sls[h]] for h in heads]
    q = [_dot(ch[h], wq_ref[h]) for h in heads]
    k = [_dot(ch[h], wk_ref[h]) * (ML_HD ** -0.5) for h in heads]
    v = [zv[:, sls[h]] for h in heads]
    qk = [_dot_nt(q[h], k[h]) for h in heads]
    qc = []
    for h in heads:
        parts = [_dot_nt(q[h][j * L:(j + 1) * L], cmat[(h, j)]) for j in range(nb)]
        qc.append(parts[0] if nb == 1 else jnp.concatenate(parts, axis=0))

    b_col = [bcum_col[:, ML_H + h:ML_H + h + 1] for h in heads]
    li_col = [gate_col[:, h:h + 1] for h in heads]
    m_t, s, sc = [], [], []
    for h in heads:
        log_d = jnp.where(causal, b_col[h] - bcum_row[ML_H + h:ML_H + h + 1, :] + gate_row[h:h + 1, :], -jnp.inf)
        inter = b_col[h] + per_row(m_prev[h])
        mt = jnp.maximum(jnp.max(log_d, axis=-1, keepdims=True), inter)
        m_t.append(mt)
        s.append(qk[h] * jnp.exp(log_d - mt))
        sc.append(jnp.exp(inter - mt))

    for h in heads:
        num = _dot(s[h], v[h]) + sc[h] * qc[h]
        den = (jnp.sum(s[h], axis=-1, keepdims=True)
               + sc[h] * jnp.sum(q[h] * per_row(nvec[h]), axis=-1, keepdims=True))
        hh = num / jnp.maximum(jnp.abs(den), jnp.exp(-m_t[h]))
        hc = hh - jnp.mean(hh, axis=-1, keepdims=True)
        hn = hc * lax.rsqrt(jnp.mean(hc * hc, axis=-1, keepdims=True) + 1e-6) * ng_ref[:, sls[h]]
        y = jax.nn.sigmoid(zo[:, sls[h]]) * (hn + skip_ref[:, sls[h]] * ch[h])
        y_ref[:, :, sls[h]] = y.reshape(nb, L, ML_HD).astype(BF16)

    for h in heads:
        m_new = last(m_t[h])
        b_last = last(b_col[h])
        wj = jnp.exp(per_row(b_last - m_new) - b_col[h] + li_col[h])
        dec = jnp.exp(b_last + m_prev[h] - m_new)
        wv = wj * v[h]
        for j in range(nb):
            js = slice(j * L, (j + 1) * L)
            c_out[j, h] = dec[j] * cmat[(h, j)] + _dot_tn(wv[js], k[h][js])
        n_out[:, h:h + 1, :] = dec * nvec[h] + jnp.sum((wj * k[h]).reshape(nb, L, ML_HD), axis=1, keepdims=True)
        m_out[:, h:h + 1, :] = jnp.broadcast_to(m_new, (nb, 1, ML_HD))

    if carry:
        @pl.when(ci == pl.num_programs(1) - 1)
        def _():
            cout_ref[...] = c_ref[...]
            nout_ref[...] = n_ref[...]
            mout_ref[...] = m_ref[...]


def _mlstm(z3, zift, buf, c_all, n_all, m_all, p, l, nb, L):
    bsz, tlen, _ = z3.shape
    c = MIX_W
    nc = tlen // L
    rows = nb * L
    carry = nc > 1
    bcol = jnp.broadcast_to(p["mlstm_bif"][l][:, None], (8, rows))
    vec = lambda: pl.BlockSpec((None, 1, c), lambda b, i: (l, 0, 0))
    zspec = lambda blk: pl.BlockSpec((nb, L, c), lambda b, i, blk=blk: (b, i, blk))
    hmat = lambda: pl.BlockSpec((None, ML_H, ML_HD, ML_HD), lambda b, i: (l, 0, 0, 0))
    cspec = lambda: pl.BlockSpec((None, nb, ML_H, ML_HD, ML_HD), lambda b, i: (l, b, 0, 0, 0))
    nspec = lambda: pl.BlockSpec((None, nb, ML_H, ML_HD), lambda b, i: (l, b, 0, 0))
    kern = functools.partial(_mlstm_kernel, nb=nb, L=L, carry=carry)
    tiny = (1, 8, 128)
    return pl.pallas_call(
        kern,
        grid=(bsz // nb, nc),
        in_specs=[zspec(ZB_MX), zspec(ZB_MV), zspec(ZB_MO),
                  pl.BlockSpec((nb, L, 128), lambda b, i: (b, i, Z_MIF // 128)),
                  pl.BlockSpec((None, 8, rows), lambda b, i: (b * nc + i, 0, 0)),
                  pl.BlockSpec((None, nb, CONV_W - 1, c), lambda b, i: (l, b, 0, 0)),
                  cspec(), nspec(), nspec(),
                  pl.BlockSpec((None, CONV_W, c), lambda b, i: (l, 0, 0)),
                  vec(), hmat(), hmat(),
                  pl.BlockSpec((None, 1, 128), lambda b, i: (l, 0, 0)),
                  pl.BlockSpec((8, rows), lambda b, i: (0, 0)),
                  vec(), vec()],
        out_specs=[pl.BlockSpec((nb, L, c), lambda b, i: (b, i, 0)), cspec(), nspec(), nspec()],
        out_shape=[jax.ShapeDtypeStruct((bsz, tlen, c), BF16),
                   jax.ShapeDtypeStruct(c_all.shape, F32),
                   jax.ShapeDtypeStruct(n_all.shape, F32),
                   jax.ShapeDtypeStruct(m_all.shape, F32)],
        input_output_aliases={6: 1, 7: 2, 8: 3},
        scratch_shapes=[pltpu.VMEM((nb, 8 + L, c), F32),
                        pltpu.VMEM((nb, ML_H, ML_HD, ML_HD) if carry else tiny, F32),
                        pltpu.VMEM((nb, ML_H, ML_HD) if carry else tiny, F32),
                        pltpu.VMEM((nb, ML_H, ML_HD) if carry else tiny, F32)],
        compiler_params=_cparams(("parallel", "arbitrary")),
        name="mlstm",
    )(z3, z3, z3, z3, zift, buf, c_all, n_all, m_all,
      p["mlstm_conv_w"], p["mlstm_conv_b"], p["mlstm_wq"], p["mlstm_wk"], p["mlstm_brow"], bcol,
      p["mlstm_norm_g"], p["mlstm_skip"])


def _block_diag(w):
    dp, nb, d, _ = w.shape
    eye = jnp.eye(nb, dtype=w.dtype)
    return jnp.einsum("lnij,nm->lnimj", w, eye).reshape(dp, nb * d, nb * d)


def _prepare(raw):
    p = {}
    wt = jnp.swapaxes(raw["w_in"], 1, 2)
    zeros = lambda n: jnp.zeros((DEPTH, n, D_MODEL), wt.dtype)
    p["w_mix"] = jnp.concatenate(
        [wt[:, 0:3584], wt[:, 3744:5280], wt[:, 3584:3744], zeros(RW_TAIL_PAD - RW_TAIL),
         wt[:, 5280:5288], zeros(Z_COLS - Z_MIF - 8)], axis=1).astype(BF16)
    p["w_gate"] = wt[:, P_SRC:].astype(BF16)
    for name in ("w_branch", "w_out", "w_ff_out"):
        p[name] = raw[name].astype(BF16)
    p["w_ff_in"] = raw["w_ff_in"]
    row = lambda a: a.reshape(DEPTH, 1, -1)
    for name in ("norm_pre_mix", "norm_post_mix", "norm_pre_ffn", "norm_post_ffn",
                 "lru_conv_b", "lru_ba", "lru_bx", "lru_lambda", "gmlp_ln_g", "gmlp_ln_b",
                 "rwkv_w0", "rwkv_a0", "rwkv_kk", "rwkv_ka", "rwkv_rk", "rwkv_ln_g", "rwkv_ln_b",
                 "mlstm_conv_b", "mlstm_norm_g", "mlstm_skip"):
        p[name] = row(raw[name])
    p["lru_conv_w"] = raw["lru_conv_w"]
    p["mlstm_conv_w"] = raw["mlstm_conv_w"]
    p["lru_wa"] = _block_diag(raw["lru_wa"]).astype(BF16)
    p["lru_wx"] = _block_diag(raw["lru_wx"]).astype(BF16)
    mu = raw["rwkv_mu"]
    p["rwkv_mu_rkv"] = row(mu[:, :RW_RKV])
    p["rwkv_mu_tail"] = row(jnp.pad(mu[:, RW_RKV:], ((0, 0), (0, RW_TAIL_PAD - RW_TAIL))))

    def lora(w, lo):
        return jnp.pad(w, ((0, 0), (lo, RW_TAIL_PAD - lo - w.shape[1]), (0, 0))).astype(BF16)

    p["rwkv_w2"] = lora(raw["rwkv_w2"], 0)
    p["rwkv_a2"] = lora(raw["rwkv_a2"], 32)
    p["rwkv_g2"] = lora(raw["rwkv_g2"], 64)
    p["mlstm_wq"] = raw["mlstm_wq"].astype(BF16)
    p["mlstm_wk"] = raw["mlstm_wk"].astype(BF16)
    bif = jnp.concatenate([raw["mlstm_bi"], raw["mlstm_bf"]], axis=-1)
    p["mlstm_bif"] = bif
    p["mlstm_brow"] = row(jnp.pad(bif, ((0, 0), (0, 128 - 8))))
    p["gmlp_ws"] = raw["gmlp_ws"]
    p["gmlp_bs"] = raw["gmlp_bs"]
    return p


def _gmlp_mix_weights(p, tlen):
    L = min(GM_CHUNK, tlen)
    rep = GM_CHUNK // L
    ws = jnp.tril(p["gmlp_ws"][:, :, :L, :L])
    eye = jnp.eye(rep, dtype=ws.dtype)
    ws_mix = jnp.einsum("lgps,ab->lgapbs", ws, eye).reshape(DEPTH, GM_GROUPS, GM_CHUNK, GM_CHUNK).astype(BF16)
    bias = jnp.swapaxes(p["gmlp_bs"][:, :, :L], 1, 2)
    bias = jnp.repeat(bias, MIX_W // GM_GROUPS, axis=2)
    bias = jnp.tile(bias, (1, rep, 1))
    return ws_mix, bias


def _group_forward(x3, states, p, is_start, depth=DEPTH):
    bsz, tlen, _ = x3.shape
    m = bsz * tlen
    x = x3.reshape(m, D_MODEL)
    lru_buf, lru_h, rw_shift, rw_s, ml_buf, ml_c, ml_n, ml_m = states
    tm = min(TM_DENSE, m)
    if is_start:
        lru_bb, lru_tt, rw_nb, rw_l, ml_nb, ml_l = 1, 512, min(RW_SEQS_P, bsz), RW_CHUNK, 1, ML_CHUNK_P
    else:
        lru_bb, lru_tt, rw_nb, rw_l, ml_nb, ml_l = 16, tlen, RW_ROWS // tlen, tlen, ML_ROWS // tlen, tlen
    ws_mix, gm_bias = _gmlp_mix_weights(p, tlen)
    nl = lru_h.shape[0]
    h_all = lru_h.reshape(nl, bsz, 1, MIX_W)
    sh_rkv = rw_shift[:, :, :RW_RKV].reshape(nl, bsz, 1, RW_RKV)
    sh_tail = jnp.pad(rw_shift[:, :, RW_RKV:], ((0, 0), (0, 0), (0, RW_TAIL_PAD - RW_TAIL)))
    sh_tail = sh_tail.reshape(nl, bsz, 1, RW_TAIL_PAD)
    s_all, c_all, n_all = rw_s, ml_c, ml_n
    m_all = jnp.broadcast_to(ml_m[:, :, :, None], (nl, bsz, ML_H, ML_HD))
    new_states = [[] for _ in range(8)]
    gm_vs = []
    for l in range(depth):
        z, hn = _inproj(x, p["norm_pre_mix"], p["w_mix"], l, tm)
        z3 = z.reshape(bsz, tlen, Z_COLS)

        y_a, h_all = _lru(z3, lru_buf, h_all, p, l, is_start, lru_bb, lru_tt)
        y_b, gm_v = _gmlp(z, p, l, ws_mix, gm_bias)
        y_c, s_all = _rwkv(z3, sh_rkv, sh_tail, s_all, p, l, rw_nb, rw_l)
        y_c = y_c.reshape(m, MIX_W)
        ml_rows = ml_nb * ml_l
        zift = jnp.swapaxes(z[:, Z_MIF:Z_MIF + 8].reshape(m // ml_rows, ml_rows, 8), 1, 2)
        y_d, c_all, n_all, m_all = _mlstm(z3, zift, ml_buf, c_all, n_all, m_all, p, l, ml_nb, ml_l)
        y_d = y_d.reshape(m, MIX_W)

        merged = _merge(hn, (y_a.reshape(m, MIX_W), y_b, y_c, y_d), p["w_gate"], p["w_branch"], l, tm)
        x1, hf = _outproj(merged, p["w_out"], x, p["norm_post_mix"], p["norm_pre_ffn"], l, min(TM_OUTPROJ, m))
        hmid = _ffup(hf, p["w_ff_in"], l, tm)
        x = _ffdown(hmid, p["w_ff_out"], x1, p["norm_post_ffn"], l, min(TM_FFDOWN, m))

        new_states[0].append(z3[:, tlen - (CONV_W - 1):, 0:MIX_W])
        new_states[2].append(jnp.concatenate(
            [z3[:, tlen - 1, ZB_R * MIX_W:ZB_R * MIX_W + RW_RKV], z3[:, tlen - 1, Z_TAIL:Z_TAIL + RW_TAIL]], axis=-1))
        new_states[4].append(z3[:, tlen - (CONV_W - 1):, ZB_MX * MIX_W:(ZB_MX + 1) * MIX_W])
        gm_vs.append(gm_v.reshape(bsz, tlen, MIX_W))
    out_states = [jnp.stack(new_states[0], axis=0), h_all.reshape(nl, bsz, MIX_W)[:depth],
                  jnp.stack(new_states[2], axis=0), s_all[:depth],
                  jnp.stack(new_states[4], axis=0), c_all[:depth], n_all[:depth], m_all[:depth, :, :, 0]]
    return x.reshape(bsz, tlen, D_MODEL), out_states, jnp.stack(gm_vs, axis=0)


def kernel(x_prompt, x_sample, state_lru_conv, state_lru_h, state_rwkv_shift, state_rwkv_wkv, state_mlstm_conv, state_mlstm_C, state_mlstm_n, state_mlstm_m, norm_pre_mix, norm_post_mix, norm_pre_ffn, norm_post_ffn, w_in, lru_conv_w, lru_conv_b, lru_wa, lru_ba, lru_wx, lru_bx, lru_lambda, gmlp_ln_g, gmlp_ln_b, gmlp_ws, gmlp_bs, rwkv_mu, rwkv_w0, rwkv_w2, rwkv_a0, rwkv_a2, rwkv_g2, rwkv_kk, rwkv_ka, rwkv_rk, rwkv_ln_g, rwkv_ln_b, mlstm_conv_w, mlstm_conv_b, mlstm_wq, mlstm_wk, mlstm_bi, mlstm_bf, mlstm_norm_g, mlstm_skip, w_branch, w_out, w_ff_in, w_ff_out):
    raw = dict(norm_pre_mix=norm_pre_mix, norm_post_mix=norm_post_mix, norm_pre_ffn=norm_pre_ffn,
               norm_post_ffn=norm_post_ffn, w_in=w_in, lru_conv_w=lru_conv_w, lru_conv_b=lru_conv_b,
               lru_wa=lru_wa, lru_ba=lru_ba, lru_wx=lru_wx, lru_bx=lru_bx, lru_lambda=lru_lambda,
               gmlp_ln_g=gmlp_ln_g, gmlp_ln_b=gmlp_ln_b, gmlp_ws=gmlp_ws, gmlp_bs=gmlp_bs,
               rwkv_mu=rwkv_mu, rwkv_w0=rwkv_w0, rwkv_w2=rwkv_w2, rwkv_a0=rwkv_a0, rwkv_a2=rwkv_a2,
               rwkv_g2=rwkv_g2, rwkv_kk=rwkv_kk, rwkv_ka=rwkv_ka, rwkv_rk=rwkv_rk, rwkv_ln_g=rwkv_ln_g,
               rwkv_ln_b=rwkv_ln_b, mlstm_conv_w=mlstm_conv_w, mlstm_conv_b=mlstm_conv_b, mlstm_wq=mlstm_wq,
               mlstm_wk=mlstm_wk, mlstm_bi=mlstm_bi, mlstm_bf=mlstm_bf, mlstm_norm_g=mlstm_norm_g,
               mlstm_skip=mlstm_skip, w_branch=w_branch, w_out=w_out, w_ff_in=w_ff_in, w_ff_out=w_ff_out)
    p = _prepare(raw)
    bp = x_prompt.shape[0]
    zero = lambda *s: jnp.zeros((DEPTH, bp) + s, F32)
    prompt_states = (zero(CONV_W - 1, MIX_W), zero(MIX_W), zero(RW_RKV + RW_TAIL), zero(RW_H, RW_HD, RW_HD),
                     zero(CONV_W - 1, MIX_W), zero(ML_H, ML_HD, ML_HD), zero(ML_H, ML_HD), zero(ML_H))
    sample_states = (state_lru_conv, state_lru_h, state_rwkv_shift, state_rwkv_wkv,
                     state_mlstm_conv, state_mlstm_C, state_mlstm_n, state_mlstm_m)
    yp, st_p, _ = _group_forward(x_prompt, prompt_states, p, True)
    ys, st_s, gm_v = _group_forward(x_sample, sample_states, p, False)
    return (yp, ys, *st_p, *st_s, gm_v)
```

```python
import functools

import numpy as np
import jax
import jax.numpy as jnp
from jax import lax
from jax.experimental import pallas as pl
from jax.experimental.pallas import tpu as pltpu

F32 = jnp.float32
BF16 = jnp.bfloat16

D_MODEL = 2048
DEPTH = 4
MIX_W = 512
CONV_W = 4
LRU_BLOCKS = 8
LRU_C = 8.0
GM_CHUNK = 128
GM_GROUPS = 4
RW_HD = 64
RW_H = 8
RW_RKV = 3 * MIX_W
RW_TAIL = 160
RW_TAIL_PAD = 256
RW_LN_EPS = 64e-5
ML_H = 4
ML_HD = 128
D_FF = 5632
N_GATE = 4 * D_MODEL
P_SRC = 5288
Z_COLS = 5632

ZB_LX, ZB_LG, ZB_GU, ZB_GV, ZB_R, ZB_K, ZB_V, ZB_MX, ZB_MV, ZB_MO = range(10)
Z_TAIL = 5120
Z_MIF = 5376

RW_CHUNK = 64
RW_SEQS_P = 2
ML_CHUNK_P = 256
GM_TILE_ROWS = 512
TM_DENSE = 1024
TM_OUTPROJ = 512
OUTPROJ_SUB_ROWS = 256
FFUP_SUB_ROWS = 256
FFDOWN_SUB_ROWS = 128
TM_FFDOWN = 512
TK_FFDOWN = 1408
VMEM_LIMIT = 56 * 1024 * 1024
HI = lax.Precision.HIGHEST


def _cparams(sem):
    return pltpu.CompilerParams(dimension_semantics=sem, vmem_limit_bytes=VMEM_LIMIT)


def _softplus(x):
    return jnp.maximum(x, 0.0) + jnp.log1p(jnp.exp(-jnp.abs(x)))


def _log_sigmoid(x):
    return -_softplus(-x)


def _rms(x, g):
    return x * lax.rsqrt(jnp.mean(x * x, axis=-1, keepdims=True) + 1e-6) * g


def _dot(a, b):
    return jnp.dot(a.astype(BF16), b.astype(BF16), preferred_element_type=F32)


def _nt(a, b):
    return lax.dot_general(a, b, (((1,), (1,)), ((), ())), preferred_element_type=F32)


def _dot_nt(a, b):
    return lax.dot_general(a.astype(BF16), b.astype(BF16), (((1,), (1,)), ((), ())),
                           preferred_element_type=F32)


def _dot_tn(a, b):
    return lax.dot_general(a.astype(BF16), b.astype(BF16), (((0,), (0,)), ((), ())),
                           preferred_element_type=F32)


def _inproj_kernel(x_ref, g_ref, w_ref, z_ref, hn_ref):
    @pl.when(pl.program_id(1) == 0)
    def _():
        hn_ref[...] = _rms(x_ref[...], g_ref[...]).astype(BF16)

    z_ref[...] = _nt(hn_ref[...], w_ref[...])


def _inproj(x, g, w, l, tm, tn=512):
    m = x.shape[0]
    return pl.pallas_call(
        _inproj_kernel,
        grid=(m // tm, Z_COLS // tn),
        in_specs=[pl.BlockSpec((tm, D_MODEL), lambda i, j: (i, 0)),
                  pl.BlockSpec((None, 1, D_MODEL), lambda i, j: (l, 0, 0)),
                  pl.BlockSpec((None, tn, D_MODEL), lambda i, j: (l, j, 0))],
        out_specs=[pl.BlockSpec((tm, tn), lambda i, j: (i, j)),
                   pl.BlockSpec((tm, D_MODEL), lambda i, j: (i, 0))],
        out_shape=[jax.ShapeDtypeStruct((m, Z_COLS), F32),
                   jax.ShapeDtypeStruct((m, D_MODEL), BF16)],
        compiler_params=_cparams(("parallel", "arbitrary")),
        name="inproj",
    )(x, g, w)


def _merge_kernel(hn_ref, ya_ref, yb_ref, yc_ref, yd_ref, g0_ref, g1_ref, g2_ref, g3_ref, wb_ref, o_ref):
    hn = hn_ref[...]
    acc = None
    for b, (y_ref, wg_ref) in enumerate(((ya_ref, g0_ref), (yb_ref, g1_ref), (yc_ref, g2_ref), (yd_ref, g3_ref))):
        zg = _nt(hn, wg_ref[...])
        br = jnp.dot(y_ref[...], wb_ref[b], preferred_element_type=F32)
        term = jax.nn.sigmoid(zg) * br
        acc = term if acc is None else acc + term
    o_ref[...] = acc.astype(BF16)


def _merge(hn, ys, wgate, wbranch, l, tm, tn=256):
    m = hn.shape[0]
    nb = D_MODEL // tn
    y_spec = pl.BlockSpec((tm, MIX_W), lambda j, i: (i, 0))
    g_specs = [pl.BlockSpec((None, tn, D_MODEL), lambda j, i, b=b: (l, b * nb + j, 0)) for b in range(4)]
    return pl.pallas_call(
        _merge_kernel,
        grid=(nb, m // tm),
        in_specs=[pl.BlockSpec((tm, D_MODEL), lambda j, i: (i, 0)), y_spec, y_spec, y_spec, y_spec,
                  *g_specs,
                  pl.BlockSpec((None, 4, MIX_W, tn), lambda j, i: (l, 0, 0, j))],
        out_specs=pl.BlockSpec((tm, tn), lambda j, i: (i, j)),
        out_shape=jax.ShapeDtypeStruct((m, D_MODEL), BF16),
        compiler_params=_cparams(("parallel", "arbitrary")),
        name="merge",
    )(hn, *ys, wgate, wgate, wgate, wgate, wbranch)


def _outproj_kernel(mg_ref, w_ref, x_ref, gpost_ref, gpre_ref, x1_ref, hf_ref):
    sub = OUTPROJ_SUB_ROWS
    for r0 in range(0, mg_ref.shape[0], sub):
        rs = slice(r0, r0 + sub)
        mix = jnp.dot(mg_ref[rs, :], w_ref[...], preferred_element_type=F32)
        x1 = x_ref[rs, :] + _rms(mix, gpost_ref[...])
        x1_ref[rs, :] = x1
        hf_ref[rs, :] = _rms(x1, gpre_ref[...]).astype(BF16)


def _outproj(merged, w_out, x, g_post, g_pre_ffn, l, tm):
    m = x.shape[0]
    g_spec = pl.BlockSpec((None, 1, D_MODEL), lambda i: (l, 0, 0))
    row_spec = pl.BlockSpec((tm, D_MODEL), lambda i: (i, 0))
    return pl.pallas_call(
        _outproj_kernel,
        grid=(m // tm,),
        in_specs=[row_spec, pl.BlockSpec((None, D_MODEL, D_MODEL), lambda i: (l, 0, 0)), row_spec, g_spec, g_spec],
        out_specs=[row_spec, row_spec],
        out_shape=[jax.ShapeDtypeStruct((m, D_MODEL), F32), jax.ShapeDtypeStruct((m, D_MODEL), BF16)],
        compiler_params=_cparams(("parallel",)),
        name="outproj",
    )(merged, w_out, x, g_post, g_pre_ffn)


def _ffup_kernel(hf_ref, wg_ref, wu_ref, o_ref, wgb_ref, wub_ref):
    @pl.when(pl.program_id(1) == 0)
    def _():
        wgb_ref[...] = wg_ref[...].astype(BF16)
        wub_ref[...] = wu_ref[...].astype(BF16)

    sub = min(FFUP_SUB_ROWS, hf_ref.shape[0])
    for r0 in range(0, hf_ref.shape[0], sub):
        rs = slice(r0, r0 + sub)
        hf = hf_ref[rs, :]
        g = jnp.dot(hf, wgb_ref[...], preferred_element_type=F32)
        u = jnp.dot(hf, wub_ref[...], preferred_element_type=F32)
        o_ref[rs, :] = (jax.nn.silu(g) * u).astype(BF16)


def _ffup(hf, w_ff_in, l, tm, tn=512):
    m = hf.shape[0]
    nb = D_FF // tn
    return pl.pallas_call(
        _ffup_kernel,
        grid=(nb, m // tm),
        in_specs=[pl.BlockSpec((tm, D_MODEL), lambda j, i: (i, 0)),
                  pl.BlockSpec((None, D_MODEL, tn), lambda j, i: (l, 0, j)),
                  pl.BlockSpec((None, D_MODEL, tn), lambda j, i: (l, 0, nb + j))],
        out_specs=pl.BlockSpec((tm, tn), lambda j, i: (i, j)),
        out_shape=jax.ShapeDtypeStruct((m, D_FF), BF16),
        scratch_shapes=[pltpu.VMEM((D_MODEL, tn), BF16), pltpu.VMEM((D_MODEL, tn), BF16)],
        compiler_params=_cparams(("parallel", "arbitrary")),
        name="ffup",
    )(hf, w_ff_in, w_ff_in)


def _ffdown_kernel(h_ref, w_ref, x1_ref, g_ref, o_ref, acc_ref):
    k = pl.program_id(1)
    last = pl.num_programs(1) - 1

    @pl.when(k == 0)
    def _():
        acc_ref[...] = jnp.dot(h_ref[...], w_ref[...], preferred_element_type=F32)

    @pl.when((k > 0) & (k < last))
    def _():
        acc_ref[...] += jnp.dot(h_ref[...], w_ref[...], preferred_element_type=F32)

    @pl.when(k == last)
    def _():
        sub = FFDOWN_SUB_ROWS
        for r0 in range(0, acc_ref.shape[0], sub):
            rs = slice(r0, r0 + sub)
            ff = acc_ref[rs, :] + jnp.dot(h_ref[rs, :], w_ref[...], preferred_element_type=F32)
            o_ref[rs, :] = x1_ref[rs, :] + _rms(ff, g_ref[...])


def _ffdown(h, w_ff_out, x1, g_post, l, tm, tk=TK_FFDOWN):
    m = x1.shape[0]
    row_spec = pl.BlockSpec((tm, D_MODEL), lambda i, k: (i, 0))
    return pl.pallas_call(
        _ffdown_kernel,
        grid=(m // tm, D_FF // tk),
        in_specs=[pl.BlockSpec((tm, tk), lambda i, k: (i, k)),
                  pl.BlockSpec((None, tk, D_MODEL), lambda i, k: (l, k, 0)),
                  row_spec,
                  pl.BlockSpec((None, 1, D_MODEL), lambda i, k: (l, 0, 0))],
        out_specs=row_spec,
        out_shape=jax.ShapeDtypeStruct((m, D_MODEL), F32),
        scratch_shapes=[pltpu.VMEM((tm, D_MODEL), F32)],
        compiler_params=_cparams(("parallel", "arbitrary")),
        name="ffdown",
    )(h, w_ff_out, x1, g_post)


def _lru_kernel(zx_ref, zg_ref, buf_ref, h0_ref, cw_ref, cb_ref, wa_ref, ba_ref, wx_ref, bx_ref, lam_ref,
                y_ref, hout_ref, xs_ref, a_ref, b_ref, h_ref, *, is_start, bb, tt):
    t = pl.program_id(1)
    c = MIX_W

    @pl.when(t == 0)
    def _():
        xs_ref[:, 5:8, :] = buf_ref[...]
        h_ref[...] = jnp.broadcast_to(h0_ref[...], (bb, 8, c))

    @pl.when(t > 0)
    def _():
        xs_ref[:, 5:8, :] = xs_ref[:, tt + 5:tt + 8, :]

    xs_ref[:, 8:8 + tt, :] = zx_ref[...]
    xc = cb_ref[...] + cw_ref[0:1, :] * xs_ref[:, 5:5 + tt, :]
    for j in range(1, CONV_W):
        xc = xc + cw_ref[j:j + 1, :] * xs_ref[:, 5 + j:5 + j + tt, :]
    xc2 = xc.reshape(bb * tt, c)
    r = jax.nn.sigmoid(_dot(xc2, wa_ref[...]) + ba_ref[...])
    i = jax.nn.sigmoid(_dot(xc2, wx_ref[...]) + bx_ref[...])
    log_a = LRU_C * r * _log_sigmoid(lam_ref[...])
    a = jnp.exp(log_a)
    mult = jnp.sqrt(1.0 - jnp.exp(2.0 * log_a))
    if is_start:
        tpos = lax.broadcasted_iota(jnp.int32, (bb, tt, c), 1).reshape(bb * tt, c) + t * tt
        mult = jnp.where(tpos == 0, 1.0, mult)
    a_ref[...] = a.reshape(bb, tt, c)
    b_ref[...] = (mult * i * xc2).reshape(bb, tt, c)

    row = lax.broadcasted_iota(jnp.int32, (bb, 8, c), 1).reshape(bb * 8, c)

    def group(gi, carry):
        off = pl.multiple_of(gi * 8, 8)
        av = a_ref[:, pl.ds(off, 8), :].reshape(bb * 8, c)
        bv = b_ref[:, pl.ds(off, 8), :].reshape(bb * 8, c)
        for s in (1, 2, 4):
            keep = row >= s
            a_sh = pltpu.roll(av, s, 0)
            b_sh = pltpu.roll(bv, s, 0)
            bv = jnp.where(keep, av * b_sh + bv, bv)
            av = jnp.where(keep, av * a_sh, av)
        hh = (av * h_ref[...].reshape(bb * 8, c) + bv).reshape(bb, 8, c)
        b_ref[:, pl.ds(off, 8), :] = hh
        h_ref[...] = jnp.broadcast_to(hh[:, 7:8, :], (bb, 8, c))
        return carry

    lax.fori_loop(0, tt // 8, group, 0)
    y_ref[...] = (b_ref[...] * jax.nn.gelu(zg_ref[...])).astype(BF16)

    @pl.when(t == pl.num_programs(1) - 1)
    def _():
        hout_ref[...] = h_ref[:, 7:8, :]


def _lru(z3, buf, h_all, p, l, is_start, bb, tt):
    bsz, tlen, _ = z3.shape
    c = MIX_W
    vec = lambda: pl.BlockSpec((None, 1, c), lambda b, t: (l, 0, 0))
    mat = lambda: pl.BlockSpec((None, c, c), lambda b, t: (l, 0, 0))
    hspec = lambda: pl.BlockSpec((None, bb, 1, c), lambda b, t: (l, b, 0, 0))
    kern = functools.partial(_lru_kernel, is_start=is_start, bb=bb, tt=tt)
    return pl.pallas_call(
        kern,
        grid=(bsz // bb, tlen // tt),
        in_specs=[pl.BlockSpec((bb, tt, c), lambda b, t: (b, t, ZB_LX)),
                  pl.BlockSpec((bb, tt, c), lambda b, t: (b, t, ZB_LG)),
                  pl.BlockSpec((None, bb, CONV_W - 1, c), lambda b, t: (l, b, 0, 0)),
                  hspec(),
                  pl.BlockSpec((None, CONV_W, c), lambda b, t: (l, 0, 0)),
                  vec(), mat(), vec(), mat(), vec(), vec()],
        out_specs=[pl.BlockSpec((bb, tt, c), lambda b, t: (b, t, 0)), hspec()],
        out_shape=[jax.ShapeDtypeStruct((bsz, tlen, c), BF16),
                   jax.ShapeDtypeStruct(h_all.shape, F32)],
        input_output_aliases={3: 1},
        scratch_shapes=[pltpu.VMEM((bb, 8 + tt, c), F32), pltpu.VMEM((bb, tt, c), F32),
                        pltpu.VMEM((bb, tt, c), F32), pltpu.VMEM((bb, 8, c), F32)],
        compiler_params=_cparams(("parallel", "arbitrary")),
        name="rglru",
    )(z3, z3, buf, h_all, p["lru_conv_w"], p["lru_conv_b"], p["lru_wa"], p["lru_ba"], p["lru_wx"], p["lru_bx"],
      p["lru_lambda"])


def _gmlp_kernel(zu_ref, zv_ref, lng_ref, lnb_ref, ws_ref, bias_ref, y_ref, v_ref):
    u = jax.nn.gelu(zu_ref[...])
    gv = jax.nn.gelu(zv_ref[...])
    vc = gv - jnp.mean(gv, axis=-1, keepdims=True)
    v = vc * lax.rsqrt(jnp.mean(vc * vc, axis=-1, keepdims=True) + 1e-5) * lng_ref[...] + lnb_ref[...]
    v_ref[...] = v
    gd = MIX_W // GM_GROUPS
    vb = v.astype(BF16)
    for ck in range(v.shape[0] // GM_CHUNK):
        rs = slice(ck * GM_CHUNK, (ck + 1) * GM_CHUNK)
        for g in range(GM_GROUPS):
            sl = slice(g * gd, (g + 1) * gd)
            s = jnp.dot(ws_ref[g], vb[rs, sl], preferred_element_type=F32) + bias_ref[:, sl]
            y_ref[rs, sl] = (u[rs, sl] * s).astype(BF16)


def _gmlp(z2, p, l, ws_mix, bias_tile):
    m = z2.shape[0]
    c = MIX_W
    rows = GM_TILE_ROWS
    vec = lambda: pl.BlockSpec((None, 1, c), lambda i: (l, 0, 0))
    return pl.pallas_call(
        _gmlp_kernel,
        grid=(m // rows,),
        in_specs=[pl.BlockSpec((rows, c), lambda i: (i, ZB_GU)),
                  pl.BlockSpec((rows, c), lambda i: (i, ZB_GV)),
                  vec(), vec(),
                  pl.BlockSpec((None, GM_GROUPS, GM_CHUNK, GM_CHUNK), lambda i: (l, 0, 0, 0)),
                  pl.BlockSpec((None, GM_CHUNK, c), lambda i: (l, 0, 0))],
        out_specs=[pl.BlockSpec((rows, c), lambda i: (i, 0)),
                   pl.BlockSpec((rows, c), lambda i: (i, 0))],
        out_shape=[jax.ShapeDtypeStruct((m, c), BF16), jax.ShapeDtypeStruct((m, c), F32)],
        compiler_params=_cparams(("parallel",)),
        name="gmlp",
    )(z2, z2, p["gmlp_ln_g"], p["gmlp_ln_b"], ws_mix, bias_tile)


RW_ROWS = 64
RW_PACK = 4


def _dot_ones(x, ones, pieces, ones_on_left=False):
    acc = None
    rem = x
    for i in range(pieces):
        part = rem.astype(BF16)
        if i + 1 < pieces:
            rem = rem - part.astype(F32)
        term = (jnp.dot(ones, part, preferred_element_type=F32) if ones_on_left
                else jnp.dot(part, ones, preferred_element_type=F32))
        acc = term if acc is None else acc + term
    return acc


def _head_sums(x, ones):
    half = MIX_W // 2
    rows = x.shape[0]
    stacked = jnp.concatenate([x[:, :half], x[:, half:]], axis=0)
    s = _dot_ones(stacked, ones, 2)
    return jnp.concatenate([s[:rows], s[rows:]], axis=1)


def _rwkv_kernel(zr_ref, zk_ref, zv_ref, zt_ref, shr_ref, sht_ref, s0_ref,
                 mur_ref, mut_ref, w0_ref, a0_ref, kkp_ref, kap_ref, rk_ref, lng_ref, lnb_ref,
                 w2_ref, a2_ref, g2_ref, hsum_ref,
                 y_ref, sout_ref, prev_ref, s_ref, *, nb, L, carry):
    ci = pl.program_id(1)
    c = MIX_W
    rows = nb * L
    bb = RW_ROWS // L
    ngrp = rows // RW_ROWS
    log_l = L.bit_length() - 1

    @pl.when(ci == 0)
    def _():
        prev_ref[:, :, 0:RW_RKV] = shr_ref[...]
        prev_ref[:, :, RW_RKV:RW_RKV + RW_TAIL_PAD] = sht_ref[...]

    first = (lax.broadcasted_iota(jnp.int32, (rows, 1), 0) & (L - 1)) == 0

    def shift(z_ref, lo, hi, mu):
        w = hi - lo
        z3 = z_ref[...]
        z = z3.reshape(rows, w)
        prev = jnp.broadcast_to(prev_ref[:, :, lo:hi], (nb, L, w)).reshape(rows, w)
        zp = jnp.where(first, prev, pltpu.roll(z, 1, 0))
        prev_ref[:, :, lo:hi] = z3[:, L - 1:L, :]
        return z + (zp - z) * mu

    r = shift(zr_ref, 0, c, mur_ref[:, 0:c])
    k = shift(zk_ref, c, 2 * c, mur_ref[:, c:2 * c])
    v = shift(zv_ref, 2 * c, 3 * c, mur_ref[:, 2 * c:3 * c])
    tl = shift(zt_ref, RW_RKV, RW_RKV + RW_TAIL_PAD, mut_ref[...])

    wlin = w0_ref[...] + _dot(jnp.tanh(tl), w2_ref[...])
    logw = -jnp.exp(-_softplus(-wlin) - 0.5)
    a = jax.nn.sigmoid(a0_ref[...] + _dot(tl, a2_ref[...]))
    gate = _dot(jax.nn.sigmoid(tl), g2_ref[...])
    kkf = k * kkp_ref[...]
    kmod = k * (1.0 + (a - 1.0) * kap_ref[...])
    hsum = hsum_ref[...]
    kkn = kkf / jnp.maximum(jnp.sqrt(_head_sums(kkf * kkf, hsum)), 1e-12)
    beta = kkn * a
    bonus = _head_sums(r * kmod * rk_ref[...], hsum) * v

    ri_f = lax.broadcasted_iota(jnp.int32, (rows, rows), 0)
    cj_f = lax.broadcasted_iota(jnp.int32, (rows, rows), 1)
    cum_op = (((ri_f >> log_l) == (cj_f >> log_l)) & (cj_f <= ri_f)).astype(BF16)
    cum = _dot_ones(logw, cum_op, 3, ones_on_left=True)
    c_last = cum.reshape(nb, L, c)[:, L - 1:L, :]
    e_last = jnp.exp(c_last)
    e_rest = jnp.exp(jnp.broadcast_to(c_last, (nb, L, c)).reshape(rows, c) - cum)
    e_mc = jnp.exp(-cum)
    a_t = -kkn * jnp.exp(cum - logw)
    r_t = r * jnp.exp(cum)
    b_t = beta * e_mc
    k_t = kmod * e_mc
    b_hat = beta * e_rest
    k_hat = kmod * e_rest

    n = RW_ROWS
    hp = RW_PACK
    w4 = hp * RW_HD
    ngq = RW_H // hp
    ti = lax.broadcasted_iota(jnp.int32, (n, w4), 0)
    jl = lax.broadcasted_iota(jnp.int32, (n, w4), 1) & (RW_HD - 1)
    same = (ti >> log_l) == (jl >> log_l)
    strict = same & (jl < ti)
    incl = same & (jl <= ti)
    eye = (jl == ti).astype(F32)
    lvl = [((ti >> (s + 1)) == (jl >> (s + 1))) & ((ti >> s) != (jl >> s)) & (jl < ti) for s in range(log_l)]
    log_hd = RW_HD.bit_length() - 1
    bmask = ((lax.broadcasted_iota(jnp.int32, (w4, w4), 0) >> log_hd)
             == (lax.broadcasted_iota(jnp.int32, (w4, w4), 1) >> log_hd))

    def bd(xb):
        return jnp.where(bmask, jnp.concatenate([xb] * hp, axis=0), jnp.zeros((), BF16))

    def pdot(a4, b4):
        return jnp.dot(a4.astype(BF16), bd(b4.astype(BF16)), preferred_element_type=F32)

    def to_bd(blocks):
        rows_ = []
        for i, blk in enumerate(blocks):
            parts = [blk if j == i else jnp.zeros((RW_HD, RW_HD), F32) for j in range(hp)]
            rows_.append(jnp.concatenate(parts, axis=1))
        return jnp.concatenate(rows_, axis=0)

    probs = [(g, q) for g in range(ngrp) for q in range(ngq)]

    def cut(x, g, q):
        return x[g * n:(g + 1) * n, q * w4:(q + 1) * w4]

    if carry:
        @pl.when(ci == 0)
        def _():
            for sq in range(nb):
                for q in range(ngq):
                    s_ref[sq, q] = to_bd([s0_ref[sq, q * hp + i] for i in range(hp)])
        s0 = {(g, q, j): s_ref[g * bb + j, q] for (g, q) in probs for j in range(bb)}
    else:
        s0 = {(g, q, j): to_bd([s0_ref[g * bb + j, q * hp + i] for i in range(hp)])
              for (g, q) in probs for j in range(bb)}

    ar, m_ba, m_ka, m_br, m_kr = {}, {}, {}, {}, {}
    for p in probs:
        ar[p] = jnp.concatenate([cut(a_t, *p), cut(r_t, *p)], axis=0).astype(BF16)
        mb = _nt(ar[p], bd(cut(b_t, *p).astype(BF16)))
        mk = _nt(ar[p], bd(cut(k_t, *p).astype(BF16)))
        m_ba[p] = jnp.where(strict, mb[0:n], 0.0)
        m_br[p] = jnp.where(incl, mb[n:2 * n], 0.0)
        m_ka[p] = jnp.where(strict, mk[0:n], 0.0)
        m_kr[p] = jnp.where(incl, mk[n:2 * n], 0.0)

    inv = {p: eye + jnp.where(lvl[0], m_ba[p], 0.0) for p in probs}
    for s in range(1, log_l):
        half = {p: pdot(inv[p], jnp.where(lvl[s], m_ba[p], 0.0)) for p in probs}
        inv = {p: inv[p] + pdot(half[p], inv[p]) for p in probs}

    xa, xr = {}, {}
    for p in probs:
        pa, pr = [], []
        for j in range(bb):
            arj = ar[p] if bb == 1 else jnp.concatenate(
                [ar[p][j * L:(j + 1) * L], ar[p][n + j * L:n + (j + 1) * L]], axis=0)
            as0 = _nt(arj, s0[p + (j,)].astype(BF16))
            pa.append(as0[0:L])
            pr.append(as0[L:2 * L])
        xa[p] = pa[0] if bb == 1 else jnp.concatenate(pa, axis=0)
        xr[p] = pr[0] if bb == 1 else jnp.concatenate(pr, axis=0)

    vv = {p: cut(v, *p) for p in probs}
    vbd = {p: bd(vv[p].astype(BF16)) for p in probs}
    u = {p: pdot(inv[p], xa[p] + jnp.dot(m_ka[p].astype(BF16), vbd[p], preferred_element_type=F32)) for p in probs}
    o = {p: xr[p] + pdot(m_br[p], u[p]) + jnp.dot(m_kr[p].astype(BF16), vbd[p], preferred_element_type=F32)
         for p in probs}

    s_new = {}
    for (g, q) in probs:
        bh, kh = cut(b_hat, g, q), cut(k_hat, g, q)
        for j in range(bb):
            js = slice(j * L, (j + 1) * L)
            uv = jnp.concatenate([u[(g, q)][js], vv[(g, q)][js]], axis=0)
            bk = jnp.concatenate([bh[js], kh[js]], axis=0)
            seq = g * bb + j
            decay = e_last[seq][:, q * w4:(q + 1) * w4]
            s_new[(g, q, j)] = jnp.where(bmask, s0[(g, q, j)] * decay + _dot_tn(uv, bk), 0.0)

    o_rows = [jnp.concatenate([o[(g, q)] for q in range(ngq)], axis=1) for g in range(ngrp)]
    o_all = o_rows[0] if ngrp == 1 else jnp.concatenate(o_rows, axis=0)
    oc = o_all - _head_sums(o_all, hsum) * (1.0 / RW_HD)
    on = oc * lax.rsqrt(_head_sums(oc * oc, hsum) * (1.0 / RW_HD) + RW_LN_EPS)
    y = (on * lng_ref[...] + lnb_ref[...] + bonus) * gate
    y_ref[...] = y.reshape(nb, L, c).astype(BF16)

    def diag_block(m, i):
        return m[i * RW_HD:(i + 1) * RW_HD, i * RW_HD:(i + 1) * RW_HD]

    if carry:
        for (g, q, j), val in s_new.items():
            s_ref[g * bb + j, q] = val

        @pl.when(ci == pl.num_programs(1) - 1)
        def _():
            for sq in range(nb):
                for q in range(ngq):
                    for i in range(hp):
                        sout_ref[sq, q * hp + i] = diag_block(s_ref[sq, q], i)
    else:
        for (g, q, j), val in s_new.items():
            for i in range(hp):
                sout_ref[g * bb + j, q * hp + i] = diag_block(val, i)


def _rwkv(z3, sh_rkv, sh_tail, s_all, p, l, nb, L):
    bsz, tlen, _ = z3.shape
    c = MIX_W
    nc = tlen // L
    carry = nc > 1
    half = c // 2
    hsum = jnp.asarray(np.kron(np.eye(half // RW_HD, dtype=np.float32), np.ones((RW_HD, RW_HD), np.float32)), BF16)
    vec = lambda w=c: pl.BlockSpec((None, 1, w), lambda b, i: (l, 0, 0))
    lora = lambda: pl.BlockSpec((None, RW_TAIL_PAD, c), lambda b, i: (l, 0, 0))
    zspec = lambda blk: pl.BlockSpec((nb, L, c), lambda b, i, blk=blk: (b, i, blk))
    sspec = lambda: pl.BlockSpec((None, nb, RW_H, RW_HD, RW_HD), lambda b, i: (l, b, 0, 0, 0))
    kern = functools.partial(_rwkv_kernel, nb=nb, L=L, carry=carry)
    return pl.pallas_call(
        kern,
        grid=(bsz // nb, nc),
        in_specs=[zspec(ZB_R), zspec(ZB_K), zspec(ZB_V),
                  pl.BlockSpec((nb, L, RW_TAIL_PAD), lambda b, i: (b, i, Z_TAIL // RW_TAIL_PAD)),
                  pl.BlockSpec((None, nb, 1, RW_RKV), lambda b, i: (l, b, 0, 0)),
                  pl.BlockSpec((None, nb, 1, RW_TAIL_PAD), lambda b, i: (l, b, 0, 0)),
                  sspec(),
                  vec(RW_RKV), vec(RW_TAIL_PAD), vec(), vec(), vec(), vec(), vec(), vec(), vec(),
                  lora(), lora(), lora(),
                  pl.BlockSpec((half, half), lambda b, i: (0, 0))],
        out_specs=[pl.BlockSpec((nb, L, c), lambda b, i: (b, i, 0)), sspec()],
        out_shape=[jax.ShapeDtypeStruct((bsz, tlen, c), BF16),
                   jax.ShapeDtypeStruct(s_all.shape, F32)],
        input_output_aliases={6: 1},
        scratch_shapes=[pltpu.VMEM((nb, 1, RW_RKV + RW_TAIL_PAD), F32),
                        pltpu.VMEM((nb, RW_H // RW_PACK, RW_PACK * RW_HD, RW_PACK * RW_HD) if carry
                                   else (1, 1, 8, 128), F32)],
        compiler_params=_cparams(("parallel", "arbitrary")),
        name="rwkv7",
    )(z3, z3, z3, z3, sh_rkv, sh_tail, s_all,
      p["rwkv_mu_rkv"], p["rwkv_mu_tail"], p["rwkv_w0"], p["rwkv_a0"], p["rwkv_kk"], p["rwkv_ka"], p["rwkv_rk"],
      p["rwkv_ln_g"], p["rwkv_ln_b"], p["rwkv_w2"], p["rwkv_a2"], p["rwkv_g2"], hsum)


ML_ROWS = 128


def _mlstm_kernel(zx_ref, zv_ref, zo_ref, zif_ref, zift_ref, buf_ref, c0_ref, n0_ref, m0_ref,
                  cw_ref, cb_ref, wq_ref, wk_ref, brow_ref, bcol_ref, ng_ref, skip_ref,
                  y_ref, cout_ref, nout_ref, mout_ref, xs_ref, c_ref, n_ref, m_ref, *, nb, L, carry):
    ci = pl.program_id(1)
    c = MIX_W
    rows = nb * L
    log_l = L.bit_length() - 1

    @pl.when(ci == 0)
    def _():
        xs_ref[:, 5:8, :] = buf_ref[...]
        if carry:
            c_ref[...] = c0_ref[...]
            n_ref[...] = n0_ref[...]
            m_ref[...] = m0_ref[...]

    if carry:
        @pl.when(ci > 0)
        def _():
            xs_ref[:, 5:8, :] = xs_ref[:, L + 5:L + 8, :]

    xs_ref[:, 8:8 + L, :] = zx_ref[...]
    conv = cb_ref[...] + cw_ref[0:1, :] * xs_ref[:, 5:5 + L, :]
    for j in range(1, CONV_W):
        conv = conv + cw_ref[j:j + 1, :] * xs_ref[:, 5 + j:5 + j + L, :]
    cc = jax.nn.silu(conv).reshape(rows, c)
    zv = zv_ref[...].reshape(rows, c)
    zo = zo_ref[...].reshape(rows, c)

    ri = lax.broadcasted_iota(jnp.int32, (rows, rows), 0)
    cj = lax.broadcasted_iota(jnp.int32, (rows, rows), 1)
    same = (ri >> log_l) == (cj >> log_l)
    causal = same & (cj <= ri)

    gate_col = zif_ref[...].reshape(rows, 128) + brow_ref[...]
    bcum_col = _dot_ones(_log_sigmoid(gate_col), causal.astype(BF16), 3, ones_on_left=True)
    gate_row = zift_ref[...] + bcol_ref[...]
    bcum_row = _dot_ones(_log_sigmoid(gate_row), (same & (ri <= cj)).astype(BF16), 3)

    def per_row(x3):
        return jnp.broadcast_to(x3, (nb, L, x3.shape[-1])).reshape(rows, x3.shape[-1])

    def last(x):
        return x.reshape(nb, L, x.shape[-1])[:, L - 1:L, :]

    c_in, n_in, m_in = (c_ref, n_ref, m_ref) if carry else (c0_ref, n0_ref, m0_ref)
    c_out, n_out, m_out = (c_ref, n_ref, m_ref) if carry else (cout_ref, nout_ref, mout_ref)
    heads = range(ML_H)
    sls = [slice(h * ML_HD, (h + 1) * ML_HD) for h in heads]
    cmat = {(h, j): c_in[j, h] for h in heads for j in range(nb)}
    nvec = [n_in[:, h:h + 1, :] for h in heads]
    m_prev = [m_in[:, h:h + 1, 0:1] for h in heads]

    ch = [cc[:, sls[h]] for h in heads]
    q = [_dot(ch[h], wq_ref[h]) for h in heads]
    k = [_dot(ch[h], wk_ref[h]) * (ML_HD ** -0.5) for h in heads]
    v = [zv[:, sls[h]] for h in heads]
    qk = [_dot_nt(q[h], k[h]) for h in heads]
    qc = []
    for h in heads:
        parts = [_dot_nt(q[h][j * L:(j + 1) * L], cmat[(h, j)]) for j in range(nb)]
        qc.append(parts[0] if nb == 1 else jnp.concatenate(parts, axis=0))

    b_col = [bcum_col[:, ML_H + h:ML_H + h + 1] for h in heads]
    li_col = [gate_col[:, h:h + 1] for h in heads]
    m_t, s, sc = [], [], []
    for h in heads:
        log_d = jnp.where(causal, b_col[h] - bcum_row[ML_H + h:ML_H + h + 1, :] + gate_row[h:h + 1, :], -jnp.inf)
        inter = b_col[h] + per_row(m_prev[h])
        mt = jnp.maximum(jnp.max(log_d, axis=-1, keepdims=True), inter)
        m_t.append(mt)
        s.append(qk[h] * jnp.exp(log_d - mt))
        sc.append(jnp.exp(inter - mt))

    for h in heads:
        num = _dot(s[h], v[h]) + sc[h] * qc[h]
        den = (jnp.sum(s[h], axis=-1, keepdims=True)
               + sc[h] * jnp.sum(q[h] * per_row(nvec[h]), axis=-1, keepdims=True))
        hh = num / jnp.maximum(jnp.abs(den), jnp.exp(-m_t[h]))
        hc = hh - jnp.mean(hh, axis=-1, keepdims=True)
        hn = hc * lax.rsqrt(jnp.mean(hc * hc, axis=-1, keepdims=True) + 1e-6) * ng_ref[:, sls[h]]
        y = jax.nn.sigmoid(zo[:, sls[h]]) * (hn + skip_ref[:, sls[h]] * ch[h])
        y_ref[:, :, sls[h]] = y.reshape(nb, L, ML_HD).astype(BF16)

    for h in heads:
        m_new = last(m_t[h])
        b_last = last(b_col[h])
        wj = jnp.exp(per_row(b_last - m_new) - b_col[h] + li_col[h])
        dec = jnp.exp(b_last + m_prev[h] - m_new)
        wv = wj * v[h]
        for j in range(nb):
            js = slice(j * L, (j + 1) * L)
            c_out[j, h] = dec[j] * cmat[(h, j)] + _dot_tn(wv[js], k[h][js])
        n_out[:, h:h + 1, :] = dec * nvec[h] + jnp.sum((wj * k[h]).reshape(nb, L, ML_HD), axis=1, keepdims=True)
        m_out[:, h:h + 1, :] = jnp.broadcast_to(m_new, (nb, 1, ML_HD))

    if carry:
        @pl.when(ci == pl.num_programs(1) - 1)
        def _():
            cout_ref[...] = c_ref[...]
            nout_ref[...] = n_ref[...]
            mout_ref[...] = m_ref[...]


def _mlstm(z3, zift, buf, c_all, n_all, m_all, p, l, nb, L):
    bsz, tlen, _ = z3.shape
    c = MIX_W
    nc = tlen // L
    rows = nb * L
    carry = nc > 1
    bcol = jnp.broadcast_to(p["mlstm_bif"][l][:, None], (8, rows))
    vec = lambda: pl.BlockSpec((None, 1, c), lambda b, i: (l, 0, 0))
    zspec = lambda blk: pl.BlockSpec((nb, L, c), lambda b, i, blk=blk: (b, i, blk))
    hmat = lambda: pl.BlockSpec((None, ML_H, ML_HD, ML_HD), lambda b, i: (l, 0, 0, 0))
    cspec = lambda: pl.BlockSpec((None, nb, ML_H, ML_HD, ML_HD), lambda b, i: (l, b, 0, 0, 0))
    nspec = lambda: pl.BlockSpec((None, nb, ML_H, ML_HD), lambda b, i: (l, b, 0, 0))
    kern = functools.partial(_mlstm_kernel, nb=nb, L=L, carry=carry)
    tiny = (1, 8, 128)
    return pl.pallas_call(
        kern,
        grid=(bsz // nb, nc),
        in_specs=[zspec(ZB_MX), zspec(ZB_MV), zspec(ZB_MO),
                  pl.BlockSpec((nb, L, 128), lambda b, i: (b, i, Z_MIF // 128)),
                  pl.BlockSpec((None, 8, rows), lambda b, i: (b * nc + i, 0, 0)),
                  pl.BlockSpec((None, nb, CONV_W - 1, c), lambda b, i: (l, b, 0, 0)),
                  cspec(), nspec(), nspec(),
                  pl.BlockSpec((None, CONV_W, c), lambda b, i: (l, 0, 0)),
                  vec(), hmat(), hmat(),
                  pl.BlockSpec((None, 1, 128), lambda b, i: (l, 0, 0)),
                  pl.BlockSpec((8, rows), lambda b, i: (0, 0)),
                  vec(), vec()],
        out_specs=[pl.BlockSpec((nb, L, c), lambda b, i: (b, i, 0)), cspec(), nspec(), nspec()],
        out_shape=[jax.ShapeDtypeStruct((bsz, tlen, c), BF16),
                   jax.ShapeDtypeStruct(c_all.shape, F32),
                   jax.ShapeDtypeStruct(n_all.shape, F32),
                   jax.ShapeDtypeStruct(m_all.shape, F32)],
        input_output_aliases={6: 1, 7: 2, 8: 3},
        scratch_shapes=[pltpu.VMEM((nb, 8 + L, c), F32),
                        pltpu.VMEM((nb, ML_H, ML_HD, ML_HD) if carry else tiny, F32),
                        pltpu.VMEM((nb, ML_H, ML_HD) if carry else tiny, F32),
                        pltpu.VMEM((nb, ML_H, ML_HD) if carry else tiny, F32)],
        compiler_params=_cparams(("parallel", "arbitrary")),
        name="mlstm",
    )(z3, z3, z3, z3, zift, buf, c_all, n_all, m_all,
      p["mlstm_conv_w"], p["mlstm_conv_b"], p["mlstm_wq"], p["mlstm_wk"], p["mlstm_brow"], bcol,
      p["mlstm_norm_g"], p["mlstm_skip"])


def _block_diag(w):
    dp, nb, d, _ = w.shape
    eye = jnp.eye(nb, dtype=w.dtype)
    return jnp.einsum("lnij,nm->lnimj", w, eye).reshape(dp, nb * d, nb * d)


def _prepare(raw):
    p = {}
    wt = jnp.swapaxes(raw["w_in"], 1, 2)
    zeros = lambda n: jnp.zeros((DEPTH, n, D_MODEL), wt.dtype)
    p["w_mix"] = jnp.concatenate(
        [wt[:, 0:3584], wt[:, 3744:5280], wt[:, 3584:3744], zeros(RW_TAIL_PAD - RW_TAIL),
         wt[:, 5280:5288], zeros(Z_COLS - Z_MIF - 8)], axis=1).astype(BF16)
    p["w_gate"] = wt[:, P_SRC:].astype(BF16)
    for name in ("w_branch", "w_out", "w_ff_out"):
        p[name] = raw[name].astype(BF16)
    p["w_ff_in"] = raw["w_ff_in"]
    row = lambda a: a.reshape(DEPTH, 1, -1)
    for name in ("norm_pre_mix", "norm_post_mix", "norm_pre_ffn", "norm_post_ffn",
                 "lru_conv_b", "lru_ba", "lru_bx", "lru_lambda", "gmlp_ln_g", "gmlp_ln_b",
                 "rwkv_w0", "rwkv_a0", "rwkv_kk", "rwkv_ka", "rwkv_rk", "rwkv_ln_g", "rwkv_ln_b",
                 "mlstm_conv_b", "mlstm_norm_g", "mlstm_skip"):
        p[name] = row(raw[name])
    p["lru_conv_w"] = raw["lru_conv_w"]
    p["mlstm_conv_w"] = raw["mlstm_conv_w"]
    p["lru_wa"] = _block_diag(raw["lru_wa"]).astype(BF16)
    p["lru_wx"] = _block_diag(raw["lru_wx"]).astype(BF16)
    mu = raw["rwkv_mu"]
    p["rwkv_mu_rkv"] = row(mu[:, :RW_RKV])
    p["rwkv_mu_tail"] = row(jnp.pad(mu[:, RW_RKV:], ((0, 0), (0, RW_TAIL_PAD - RW_TAIL))))

    def lora(w, lo):
        return jnp.pad(w, ((0, 0), (lo, RW_TAIL_PAD - lo - w.shape[1]), (0, 0))).astype(BF16)

    p["rwkv_w2"] = lora(raw["rwkv_w2"], 0)
    p["rwkv_a2"] = lora(raw["rwkv_a2"], 32)
    p["rwkv_g2"] = lora(raw["rwkv_g2"], 64)
    p["mlstm_wq"] = raw["mlstm_wq"].astype(BF16)
    p["mlstm_wk"] = raw["mlstm_wk"].astype(BF16)
    bif = jnp.concatenate([raw["mlstm_bi"], raw["mlstm_bf"]], axis=-1)
    p["mlstm_bif"] = bif
    p["mlstm_brow"] = row(jnp.pad(bif, ((0, 0), (0, 128 - 8))))
    p["gmlp_ws"] = raw["gmlp_ws"]
    p["gmlp_bs"] = raw["gmlp_bs"]
    return p


def _gmlp_mix_weights(p, tlen):
    L = min(GM_CHUNK, tlen)
    rep = GM_CHUNK // L
    ws = jnp.tril(p["gmlp_ws"][:, :, :L, :L])
    eye = jnp.eye(rep, dtype=ws.dtype)
    ws_mix = jnp.einsum("lgps,ab->lgapbs", ws, eye).reshape(DEPTH, GM_GROUPS, GM_CHUNK, GM_CHUNK).astype(BF16)
    bias = jnp.swapaxes(p["gmlp_bs"][:, :, :L], 1, 2)
    bias = jnp.repeat(bias, MIX_W // GM_GROUPS, axis=2)
    bias = jnp.tile(bias, (1, rep, 1))
    return ws_mix, bias


def _group_forward(x3, states, p, is_start, depth=DEPTH):
    bsz, tlen, _ = x3.shape
    m = bsz * tlen
    x = x3.reshape(m, D_MODEL)
    lru_buf, lru_h, rw_shift, rw_s, ml_buf, ml_c, ml_n, ml_m = states
    tm = min(TM_DENSE, m)
    if is_start:
        lru_bb, lru_tt, rw_nb, rw_l, ml_nb, ml_l = 1, 512, min(RW_SEQS_P, bsz), RW_CHUNK, 1, ML_CHUNK_P
    else:
        lru_bb, lru_tt, rw_nb, rw_l, ml_nb, ml_l = 16, tlen, RW_ROWS // tlen, tlen, ML_ROWS // tlen, tlen
    ws_mix, gm_bias = _gmlp_mix_weights(p, tlen)
    nl = lru_h.shape[0]
    h_all = lru_h.reshape(nl, bsz, 1, MIX_W)
    sh_rkv = rw_shift[:, :, :RW_RKV].reshape(nl, bsz, 1, RW_RKV)
    sh_tail = jnp.pad(rw_shift[:, :, RW_RKV:], ((0, 0), (0, 0), (0, RW_TAIL_PAD - RW_TAIL)))
    sh_tail = sh_tail.reshape(nl, bsz, 1, RW_TAIL_PAD)
    s_all, c_all, n_all = rw_s, ml_c, ml_n
    m_all = jnp.broadcast_to(ml_m[:, :, :, None], (nl, bsz, ML_H, ML_HD))
    new_states = [[] for _ in range(8)]
    gm_vs = []
    for l in range(depth):
        z, hn = _inproj(x, p["norm_pre_mix"], p["w_mix"], l, tm)
        z3 = z.reshape(bsz, tlen, Z_COLS)

        y_a, h_all = _lru(z3, lru_buf, h_all, p, l, is_start, lru_bb, lru_tt)
        y_b, gm_v = _gmlp(z, p, l, ws_mix, gm_bias)
        y_c, s_all = _rwkv(z3, sh_rkv, sh_tail, s_all, p, l, rw_nb, rw_l)
        y_c = y_c.reshape(m, MIX_W)
        ml_rows = ml_nb * ml_l
        zift = jnp.swapaxes(z[:, Z_MIF:Z_MIF + 8].reshape(m // ml_rows, ml_rows, 8), 1, 2)
        y_d, c_all, n_all, m_all = _mlstm(z3, zift, ml_buf, c_all, n_all, m_all, p, l, ml_nb, ml_l)
        y_d = y_d.reshape(m, MIX_W)

        merged = _merge(hn, (y_a.reshape(m, MIX_W), y_b, y_c, y_d), p["w_gate"], p["w_branch"], l, tm)
        x1, hf = _outproj(merged, p["w_out"], x, p["norm_post_mix"], p["norm_pre_ffn"], l, min(TM_OUTPROJ, m))
        hmid = _ffup(hf, p["w_ff_in"], l, tm)
        x = _ffdown(hmid, p["w_ff_out"], x1, p["norm_post_ffn"], l, min(TM_FFDOWN, m))

        new_states[0].append(z3[:, tlen - (CONV_W - 1):, 0:MIX_W])
        new_states[2].append(jnp.concatenate(
            [z3[:, tlen - 1, ZB_R * MIX_W:ZB_R * MIX_W + RW_RKV], z3[:, tlen - 1, Z_TAIL:Z_TAIL + RW_TAIL]], axis=-1))
        new_states[4].append(z3[:, tlen - (CONV_W - 1):, ZB_MX * MIX_W:(ZB_MX + 1) * MIX_W])
        gm_vs.append(gm_v.reshape(bsz, tlen, MIX_W))
    out_states = [jnp.stack(new_states[0], axis=0), h_all.reshape(nl, bsz, MIX_W)[:depth],
                  jnp.stack(new_states[2], axis=0), s_all[:depth],
                  jnp.stack(new_states[4], axis=0), c_all[:depth], n_all[:depth], m_all[:depth, :, :, 0]]
    return x.reshape(bsz, tlen, D_MODEL), out_states, jnp.stack(gm_vs, axis=0)


def kernel(x_prompt, x_sample, state_lru_conv, state_lru_h, state_rwkv_shift, state_rwkv_wkv, state_mlstm_conv, state_mlstm_C, state_mlstm_n, state_mlstm_m, norm_pre_mix, norm_post_mix, norm_pre_ffn, norm_post_ffn, w_in, lru_conv_w, lru_conv_b, lru_wa, lru_ba, lru_wx, lru_bx, lru_lambda, gmlp_ln_g, gmlp_ln_b, gmlp_ws, gmlp_bs, rwkv_mu, rwkv_w0, rwkv_w2, rwkv_a0, rwkv_a2, rwkv_g2, rwkv_kk, rwkv_ka, rwkv_rk, rwkv_ln_g, rwkv_ln_b, mlstm_conv_w, mlstm_conv_b, mlstm_wq, mlstm_wk, mlstm_bi, mlstm_bf, mlstm_norm_g, mlstm_skip, w_branch, w_out, w_ff_in, w_ff_out):
    raw = dict(norm_pre_mix=norm_pre_mix, norm_post_mix=norm_post_mix, norm_pre_ffn=norm_pre_ffn,
               norm_post_ffn=norm_post_ffn, w_in=w_in, lru_conv_w=lru_conv_w, lru_conv_b=lru_conv_b,
               lru_wa=lru_wa, lru_ba=lru_ba, lru_wx=lru_wx, lru_bx=lru_bx, lru_lambda=lru_lambda,
               gmlp_ln_g=gmlp_ln_g, gmlp_ln_b=gmlp_ln_b, gmlp_ws=gmlp_ws, gmlp_bs=gmlp_bs,
               rwkv_mu=rwkv_mu, rwkv_w0=rwkv_w0, rwkv_w2=rwkv_w2, rwkv_a0=rwkv_a0, rwkv_a2=rwkv_a2,
               rwkv_g2=rwkv_g2, rwkv_kk=rwkv_kk, rwkv_ka=rwkv_ka, rwkv_rk=rwkv_rk, rwkv_ln_g=rwkv_ln_g,
               rwkv_ln_b=rwkv_ln_b, mlstm_conv_w=mlstm_conv_w, mlstm_conv_b=mlstm_conv_b, mlstm_wq=mlstm_wq,
               mlstm_wk=mlstm_wk, mlstm_bi=mlstm_bi, mlstm_bf=mlstm_bf, mlstm_norm_g=mlstm_norm_g,
               mlstm_skip=mlstm_skip, w_branch=w_branch, w_out=w_out, w_ff_in=w_ff_in, w_ff_out=w_ff_out)
    p = _prepare(raw)
    bp = x_prompt.shape[0]
    zero = lambda *s: jnp.zeros((DEPTH, bp) + s, F32)
    prompt_states = (zero(CONV_W - 1, MIX_W), zero(MIX_W), zero(RW_RKV + RW_TAIL), zero(RW_H, RW_HD, RW_HD),
                     zero(CONV_W - 1, MIX_W), zero(ML_H, ML_HD, ML_HD), zero(ML_H, ML_HD), zero(ML_H))
    sample_states = (state_lru_conv, state_lru_h, state_rwkv_shift, state_rwkv_wkv,
                     state_mlstm_conv, state_mlstm_C, state_mlstm_n, state_mlstm_m)
    yp, st_p, _ = _group_forward(x_prompt, prompt_states, p, True)
    ys, st_s, gm_v = _group_forward(x_sample, sample_states, p, False)
    return (yp, ys, *st_p, *st_s, gm_v)
```

```python
import functools

import numpy as np
import jax
import jax.numpy as jnp
from jax import lax
from jax.experimental import pallas as pl
from jax.experimental.pallas import tpu as pltpu

F32 = jnp.float32
BF16 = jnp.bfloat16

D_MODEL = 2048
DEPTH = 4
MIX_W = 512
CONV_W = 4
LRU_BLOCKS = 8
LRU_C = 8.0
GM_CHUNK = 128
GM_GROUPS = 4
RW_HD = 64
RW_H = 8
RW_RKV = 3 * MIX_W
RW_TAIL = 160
RW_TAIL_PAD = 256
RW_LN_EPS = 64e-5
ML_H = 4
ML_HD = 128
D_FF = 5632
N_GATE = 4 * D_MODEL
P_SRC = 5288
Z_COLS = 5632

ZB_LX, ZB_LG, ZB_GU, ZB_GV, ZB_R, ZB_K, ZB_V, ZB_MX, ZB_MV, ZB_MO = range(10)
Z_TAIL = 5120
Z_MIF = 5376

RW_CHUNK = 64
RW_SEQS_P = 2
ML_CHUNK_P = 256
GM_TILE_ROWS = 512
TM_DENSE = 1024
TM_OUTPROJ = 512
OUTPROJ_SUB_ROWS = 256
FFUP_SUB_ROWS = 256
FFDOWN_SUB_ROWS = 128
TM_FFDOWN = 512
TK_FFDOWN = 2816
TN_MERGE = 256
LRU_SEQS_P = 4
LRU_TILE_P = 256
VMEM_LIMIT = 56 * 1024 * 1024
HI = lax.Precision.HIGHEST


def _cparams(sem):
    return pltpu.CompilerParams(dimension_semantics=sem, vmem_limit_bytes=VMEM_LIMIT)


def _softplus(x):
    return jnp.maximum(x, 0.0) + jnp.log1p(jnp.exp(-jnp.abs(x)))


def _log_sigmoid(x):
    return -_softplus(-x)


def _rms(x, g):
    return x * lax.rsqrt(jnp.mean(x * x, axis=-1, keepdims=True) + 1e-6) * g


def _dot(a, b):
    return jnp.dot(a.astype(BF16), b.astype(BF16), preferred_element_type=F32)


def _nt(a, b):
    return lax.dot_general(a, b, (((1,), (1,)), ((), ())), preferred_element_type=F32)


def _dot_nt(a, b):
    return lax.dot_general(a.astype(BF16), b.astype(BF16), (((1,), (1,)), ((), ())),
                           preferred_element_type=F32)


def _dot_tn(a, b):
    return lax.dot_general(a.astype(BF16), b.astype(BF16), (((0,), (0,)), ((), ())),
                           preferred_element_type=F32)


def _inproj_kernel(x_ref, g_ref, w_ref, z_ref, hn_ref):
    @pl.when(pl.program_id(1) == 0)
    def _():
        hn_ref[...] = _rms(x_ref[...], g_ref[...]).astype(BF16)

    z_ref[...] = _nt(hn_ref[...], w_ref[...])


def _inproj(x, g, w, l, tm, tn=512):
    m = x.shape[0]
    return pl.pallas_call(
        _inproj_kernel,
        grid=(m // tm, Z_COLS // tn),
        in_specs=[pl.BlockSpec((tm, D_MODEL), lambda i, j: (i, 0)),
                  pl.BlockSpec((None, 1, D_MODEL), lambda i, j: (l, 0, 0)),
                  pl.BlockSpec((None, tn, D_MODEL), lambda i, j: (l, j, 0))],
        out_specs=[pl.BlockSpec((tm, tn), lambda i, j: (i, j)),
                   pl.BlockSpec((tm, D_MODEL), lambda i, j: (i, 0))],
        out_shape=[jax.ShapeDtypeStruct((m, Z_COLS), F32),
                   jax.ShapeDtypeStruct((m, D_MODEL), BF16)],
        compiler_params=_cparams(("parallel", "arbitrary")),
        name="inproj",
    )(x, g, w)


def _merge_kernel(hn_ref, ya_ref, yb_ref, yc_ref, yd_ref, g0_ref, g1_ref, g2_ref, g3_ref, wb_ref, o_ref, gb_ref):
    @pl.when(pl.program_id(1) == 0)
    def _():
        for b, g_ref in enumerate((g0_ref, g1_ref, g2_ref, g3_ref)):
            gb_ref[b] = g_ref[0].astype(BF16)

    hn = hn_ref[...]
    acc = None
    for b, y_ref in enumerate((ya_ref, yb_ref, yc_ref, yd_ref)):
        zg = _nt(hn, gb_ref[b])
        br = jnp.dot(y_ref[...], wb_ref[b], preferred_element_type=F32)
        term = jax.nn.sigmoid(zg) * br
        acc = term if acc is None else acc + term
    o_ref[...] = acc.astype(BF16)


def _merge(hn, ys, w_in_t, wbranch, l, tm, tn=TN_MERGE):
    m = hn.shape[0]
    nb = D_MODEL // tn
    y_spec = pl.BlockSpec((tm, MIX_W), lambda j, i: (i, 0))
    g_specs = [pl.BlockSpec((pl.Element(1), pl.Element(tn), pl.Element(D_MODEL)),
                            lambda j, i, b=b: (l, pl.multiple_of(P_SRC + b * D_MODEL + j * tn, 8), 0))
               for b in range(4)]
    return pl.pallas_call(
        _merge_kernel,
        grid=(nb, m // tm),
        in_specs=[pl.BlockSpec((tm, D_MODEL), lambda j, i: (i, 0)), y_spec, y_spec, y_spec, y_spec,
                  *g_specs,
                  pl.BlockSpec((None, 4, MIX_W, tn), lambda j, i: (l, 0, 0, j))],
        out_specs=pl.BlockSpec((tm, tn), lambda j, i: (i, j)),
        out_shape=jax.ShapeDtypeStruct((m, D_MODEL), BF16),
        scratch_shapes=[pltpu.VMEM((4, tn, D_MODEL), BF16)],
        compiler_params=_cparams(("parallel", "arbitrary")),
        name="merge",
    )(hn, *ys, w_in_t, w_in_t, w_in_t, w_in_t, wbranch)


def _outproj_kernel(mg_ref, w_ref, x_ref, gpost_ref, gpre_ref, x1_ref, hf_ref):
    sub = OUTPROJ_SUB_ROWS
    for r0 in range(0, mg_ref.shape[0], sub):
        rs = slice(r0, r0 + sub)
        mix = jnp.dot(mg_ref[rs, :], w_ref[...], preferred_element_type=F32)
        x1 = x_ref[rs, :] + _rms(mix, gpost_ref[...])
        x1_ref[rs, :] = x1
        hf_ref[rs, :] = _rms(x1, gpre_ref[...]).astype(BF16)


def _outproj(merged, w_out, x, g_post, g_pre_ffn, l, tm):
    m = x.shape[0]
    g_spec = pl.BlockSpec((None, 1, D_MODEL), lambda i: (l, 0, 0))
    row_spec = pl.BlockSpec((tm, D_MODEL), lambda i: (i, 0))
    return pl.pallas_call(
        _outproj_kernel,
        grid=(m // tm,),
        in_specs=[row_spec, pl.BlockSpec((None, D_MODEL, D_MODEL), lambda i: (l, 0, 0)), row_spec, g_spec, g_spec],
        out_specs=[row_spec, row_spec],
        out_shape=[jax.ShapeDtypeStruct((m, D_MODEL), F32), jax.ShapeDtypeStruct((m, D_MODEL), BF16)],
        compiler_params=_cparams(("parallel",)),
        name="outproj",
    )(merged, w_out, x, g_post, g_pre_ffn)


def _ffup_kernel(hf_ref, wg_ref, wu_ref, o_ref, wgb_ref, wub_ref):
    @pl.when(pl.program_id(1) == 0)
    def _():
        wgb_ref[...] = wg_ref[...].astype(BF16)
        wub_ref[...] = wu_ref[...].astype(BF16)

    sub = min(FFUP_SUB_ROWS, hf_ref.shape[0])
    for r0 in range(0, hf_ref.shape[0], sub):
        rs = slice(r0, r0 + sub)
        hf = hf_ref[rs, :]
        g = jnp.dot(hf, wgb_ref[...], preferred_element_type=F32)
        u = jnp.dot(hf, wub_ref[...], preferred_element_type=F32)
        o_ref[rs, :] = (jax.nn.silu(g) * u).astype(BF16)


def _ffup(hf, w_ff_in, l, tm, tn=512):
    m = hf.shape[0]
    nb = D_FF // tn
    return pl.pallas_call(
        _ffup_kernel,
        grid=(nb, m // tm),
        in_specs=[pl.BlockSpec((tm, D_MODEL), lambda j, i: (i, 0)),
                  pl.BlockSpec((None, D_MODEL, tn), lambda j, i: (l, 0, j)),
                  pl.BlockSpec((None, D_MODEL, tn), lambda j, i: (l, 0, nb + j))],
        out_specs=pl.BlockSpec((tm, tn), lambda j, i: (i, j)),
        out_shape=jax.ShapeDtypeStruct((m, D_FF), BF16),
        scratch_shapes=[pltpu.VMEM((D_MODEL, tn), BF16), pltpu.VMEM((D_MODEL, tn), BF16)],
        compiler_params=_cparams(("parallel", "arbitrary")),
        name="ffup",
    )(hf, w_ff_in, w_ff_in)


def _ffdown_kernel(h_ref, w_ref, x1_ref, g_ref, o_ref, acc_ref):
    k = pl.program_id(1)
    last = pl.num_programs(1) - 1

    @pl.when(k == 0)
    def _():
        acc_ref[...] = jnp.dot(h_ref[...], w_ref[...], preferred_element_type=F32)

    @pl.when((k > 0) & (k < last))
    def _():
        acc_ref[...] += jnp.dot(h_ref[...], w_ref[...], preferred_element_type=F32)

    @pl.when(k == last)
    def _():
        sub = FFDOWN_SUB_ROWS
        for r0 in range(0, acc_ref.shape[0], sub):
            rs = slice(r0, r0 + sub)
            ff = acc_ref[rs, :] + jnp.dot(h_ref[rs, :], w_ref[...], preferred_element_type=F32)
            o_ref[rs, :] = x1_ref[rs, :] + _rms(ff, g_ref[...])


def _ffdown(h, w_ff_out, x1, g_post, l, tm, tk=TK_FFDOWN):
    m = x1.shape[0]
    row_spec = pl.BlockSpec((tm, D_MODEL), lambda i, k: (i, 0))
    return pl.pallas_call(
        _ffdown_kernel,
        grid=(m // tm, D_FF // tk),
        in_specs=[pl.BlockSpec((tm, tk), lambda i, k: (i, k)),
                  pl.BlockSpec((None, tk, D_MODEL), lambda i, k: (l, k, 0)),
                  row_spec,
                  pl.BlockSpec((None, 1, D_MODEL), lambda i, k: (l, 0, 0))],
        out_specs=row_spec,
        out_shape=jax.ShapeDtypeStruct((m, D_MODEL), F32),
        scratch_shapes=[pltpu.VMEM((tm, D_MODEL), F32)],
        compiler_params=_cparams(("parallel", "arbitrary")),
        name="ffdown",
    )(h, w_ff_out, x1, g_post)


def _lru_kernel(zx_ref, zg_ref, buf_ref, h0_ref, cw_ref, cb_ref, wa_ref, ba_ref, wx_ref, bx_ref, lam_ref,
                y_ref, hout_ref, xs_ref, a_ref, b_ref, h_ref, *, is_start, bb, tt):
    t = pl.program_id(1)
    c = MIX_W

    @pl.when(t == 0)
    def _():
        xs_ref[:, 5:8, :] = buf_ref[...]
        h_ref[...] = jnp.broadcast_to(h0_ref[...], (bb, 8, c))

    @pl.when(t > 0)
    def _():
        xs_ref[:, 5:8, :] = xs_ref[:, tt + 5:tt + 8, :]

    xs_ref[:, 8:8 + tt, :] = zx_ref[...]
    xc = cb_ref[...] + cw_ref[0:1, :] * xs_ref[:, 5:5 + tt, :]
    for j in range(1, CONV_W):
        xc = xc + cw_ref[j:j + 1, :] * xs_ref[:, 5 + j:5 + j + tt, :]
    xc2 = xc.reshape(bb * tt, c)
    r = jax.nn.sigmoid(_dot(xc2, wa_ref[...]) + ba_ref[...])
    i = jax.nn.sigmoid(_dot(xc2, wx_ref[...]) + bx_ref[...])
    log_a = LRU_C * r * _log_sigmoid(lam_ref[...])
    a = jnp.exp(log_a)
    mult = jnp.sqrt(1.0 - jnp.exp(2.0 * log_a))
    if is_start:
        tpos = lax.broadcasted_iota(jnp.int32, (bb, tt, c), 1).reshape(bb * tt, c) + t * tt
        mult = jnp.where(tpos == 0, 1.0, mult)
    a_ref[...] = a.reshape(bb, tt, c)
    b_ref[...] = (mult * i * xc2).reshape(bb, tt, c)

    row = lax.broadcasted_iota(jnp.int32, (bb, 8, c), 1).reshape(bb * 8, c)

    def group(gi, carry):
        off = pl.multiple_of(gi * 8, 8)
        av = a_ref[:, pl.ds(off, 8), :].reshape(bb * 8, c)
        bv = b_ref[:, pl.ds(off, 8), :].reshape(bb * 8, c)
        for s in (1, 2, 4):
            keep = row >= s
            a_sh = pltpu.roll(av, s, 0)
            b_sh = pltpu.roll(bv, s, 0)
            bv = jnp.where(keep, av * b_sh + bv, bv)
            av = jnp.where(keep, av * a_sh, av)
        hh = (av * h_ref[...].reshape(bb * 8, c) + bv).reshape(bb, 8, c)
        b_ref[:, pl.ds(off, 8), :] = hh
        h_ref[...] = jnp.broadcast_to(hh[:, 7:8, :], (bb, 8, c))
        return carry

    lax.fori_loop(0, tt // 8, group, 0)
    y_ref[...] = (b_ref[...] * jax.nn.gelu(zg_ref[...])).astype(BF16)

    @pl.when(t == pl.num_programs(1) - 1)
    def _():
        hout_ref[...] = h_ref[:, 7:8, :]


def _lru(z3, buf, h_all, p, l, is_start, bb, tt):
    bsz, tlen, _ = z3.shape
    c = MIX_W
    vec = lambda: pl.BlockSpec((None, 1, c), lambda b, t: (l, 0, 0))
    mat = lambda: pl.BlockSpec((None, c, c), lambda b, t: (l, 0, 0))
    hspec = lambda: pl.BlockSpec((None, bb, 1, c), lambda b, t: (l, b, 0, 0))
    kern = functools.partial(_lru_kernel, is_start=is_start, bb=bb, tt=tt)
    return pl.pallas_call(
        kern,
        grid=(bsz // bb, tlen // tt),
        in_specs=[pl.BlockSpec((bb, tt, c), lambda b, t: (b, t, ZB_LX)),
                  pl.BlockSpec((bb, tt, c), lambda b, t: (b, t, ZB_LG)),
                  pl.BlockSpec((None, bb, CONV_W - 1, c), lambda b, t: (l, b, 0, 0)),
                  hspec(),
                  pl.BlockSpec((None, CONV_W, c), lambda b, t: (l, 0, 0)),
                  vec(), mat(), vec(), mat(), vec(), vec()],
        out_specs=[pl.BlockSpec((bb, tt, c), lambda b, t: (b, t, 0)), hspec()],
        out_shape=[jax.ShapeDtypeStruct((bsz, tlen, c), BF16),
                   jax.ShapeDtypeStruct(h_all.shape, F32)],
        input_output_aliases={3: 1},
        scratch_shapes=[pltpu.VMEM((bb, 8 + tt, c), F32), pltpu.VMEM((bb, tt, c), F32),
                        pltpu.VMEM((bb, tt, c), F32), pltpu.VMEM((bb, 8, c), F32)],
        compiler_params=_cparams(("parallel", "arbitrary")),
        name="rglru",
    )(z3, z3, buf, h_all, p["lru_conv_w"], p["lru_conv_b"], p["lru_wa"], p["lru_ba"], p["lru_wx"], p["lru_bx"],
      p["lru_lambda"])


def _gmlp_kernel(zu_ref, zv_ref, lng_ref, lnb_ref, ws_ref, bias_ref, y_ref, v_ref):
    u = jax.nn.gelu(zu_ref[...])
    gv = jax.nn.gelu(zv_ref[...])
    vc = gv - jnp.mean(gv, axis=-1, keepdims=True)
    v = vc * lax.rsqrt(jnp.mean(vc * vc, axis=-1, keepdims=True) + 1e-5) * lng_ref[...] + lnb_ref[...]
    v_ref[...] = v
    gd = MIX_W // GM_GROUPS
    vb = v.astype(BF16)
    for ck in range(v.shape[0] // GM_CHUNK):
        rs = slice(ck * GM_CHUNK, (ck + 1) * GM_CHUNK)
        for g in range(GM_GROUPS):
            sl = slice(g * gd, (g + 1) * gd)
            s = jnp.dot(ws_ref[g], vb[rs, sl], preferred_element_type=F32) + bias_ref[:, sl]
            y_ref[rs, sl] = (u[rs, sl] * s).astype(BF16)


def _gmlp(z2, p, l, ws_mix, bias_tile):
    m = z2.shape[0]
    c = MIX_W
    rows = GM_TILE_ROWS
    vec = lambda: pl.BlockSpec((None, 1, c), lambda i: (l, 0, 0))
    return pl.pallas_call(
        _gmlp_kernel,
        grid=(m // rows,),
        in_specs=[pl.BlockSpec((rows, c), lambda i: (i, ZB_GU)),
                  pl.BlockSpec((rows, c), lambda i: (i, ZB_GV)),
                  vec(), vec(),
                  pl.BlockSpec((None, GM_GROUPS, GM_CHUNK, GM_CHUNK), lambda i: (l, 0, 0, 0)),
                  pl.BlockSpec((None, GM_CHUNK, c), lambda i: (l, 0, 0))],
        out_specs=[pl.BlockSpec((rows, c), lambda i: (i, 0)),
                   pl.BlockSpec((rows, c), lambda i: (i, 0))],
        out_shape=[jax.ShapeDtypeStruct((m, c), BF16), jax.ShapeDtypeStruct((m, c), F32)],
        compiler_params=_cparams(("parallel",)),
        name="gmlp",
    )(z2, z2, p["gmlp_ln_g"], p["gmlp_ln_b"], ws_mix, bias_tile)


RW_ROWS = 64
RW_PACK = 4


def _dot_ones(x, ones, pieces, ones_on_left=False):
    acc = None
    rem = x
    for i in range(pieces):
        part = rem.astype(BF16)
        if i + 1 < pieces:
            rem = rem - part.astype(F32)
        term = (jnp.dot(ones, part, preferred_element_type=F32) if ones_on_left
                else jnp.dot(part, ones, preferred_element_type=F32))
        acc = term if acc is None else acc + term
    return acc


def _head_sums(x, ones):
    half = MIX_W // 2
    rows = x.shape[0]
    stacked = jnp.concatenate([x[:, :half], x[:, half:]], axis=0)
    s = _dot_ones(stacked, ones, 2)
    return jnp.concatenate([s[:rows], s[rows:]], axis=1)


def _rwkv_kernel(zr_ref, zk_ref, zv_ref, zt_ref, shr_ref, sht_ref, s0_ref,
                 mur_ref, mut_ref, w0_ref, a0_ref, kkp_ref, kap_ref, rk_ref, lng_ref, lnb_ref,
                 w2_ref, a2_ref, g2_ref, hsum_ref,
                 y_ref, sout_ref, prev_ref, s_ref, *, nb, L, carry):
    ci = pl.program_id(1)
    c = MIX_W
    rows = nb * L
    bb = RW_ROWS // L
    ngrp = rows // RW_ROWS
    log_l = L.bit_length() - 1

    @pl.when(ci == 0)
    def _():
        prev_ref[:, :, 0:RW_RKV] = shr_ref[...]
        prev_ref[:, :, RW_RKV:RW_RKV + RW_TAIL_PAD] = sht_ref[...]

    first = (lax.broadcasted_iota(jnp.int32, (rows, 1), 0) & (L - 1)) == 0

    def shift(z_ref, lo, hi, mu):
        w = hi - lo
        z3 = z_ref[...]
        z = z3.reshape(rows, w)
        prev = jnp.broadcast_to(prev_ref[:, :, lo:hi], (nb, L, w)).reshape(rows, w)
        zp = jnp.where(first, prev, pltpu.roll(z, 1, 0))
        prev_ref[:, :, lo:hi] = z3[:, L - 1:L, :]
        return z + (zp - z) * mu

    r = shift(zr_ref, 0, c, mur_ref[:, 0:c])
    k = shift(zk_ref, c, 2 * c, mur_ref[:, c:2 * c])
    v = shift(zv_ref, 2 * c, 3 * c, mur_ref[:, 2 * c:3 * c])
    tl = shift(zt_ref, RW_RKV, RW_RKV + RW_TAIL_PAD, mut_ref[...])

    wlin = w0_ref[...] + _dot(jnp.tanh(tl), w2_ref[...])
    logw = -jnp.exp(-_softplus(-wlin) - 0.5)
    a = jax.nn.sigmoid(a0_ref[...] + _dot(tl, a2_ref[...]))
    gate = _dot(jax.nn.sigmoid(tl), g2_ref[...])
    kkf = k * kkp_ref[...]
    kmod = k * (1.0 + (a - 1.0) * kap_ref[...])
    hsum = hsum_ref[...]
    kkn = kkf / jnp.maximum(jnp.sqrt(_head_sums(kkf * kkf, hsum)), 1e-12)
    beta = kkn * a
    bonus = _head_sums(r * kmod * rk_ref[...], hsum) * v

    ri_f = lax.broadcasted_iota(jnp.int32, (rows, rows), 0)
    cj_f = lax.broadcasted_iota(jnp.int32, (rows, rows), 1)
    cum_op = (((ri_f >> log_l) == (cj_f >> log_l)) & (cj_f <= ri_f)).astype(BF16)
    cum = _dot_ones(logw, cum_op, 3, ones_on_left=True)
    c_last = cum.reshape(nb, L, c)[:, L - 1:L, :]
    e_last = jnp.exp(c_last)
    e_rest = jnp.exp(jnp.broadcast_to(c_last, (nb, L, c)).reshape(rows, c) - cum)
    e_mc = jnp.exp(-cum)
    a_t = -kkn * jnp.exp(cum - logw)
    r_t = r * jnp.exp(cum)
    b_t = beta * e_mc
    k_t = kmod * e_mc
    b_hat = beta * e_rest
    k_hat = kmod * e_rest

    n = RW_ROWS
    hp = RW_PACK
    w4 = hp * RW_HD
    ngq = RW_H // hp
    ti = lax.broadcasted_iota(jnp.int32, (n, w4), 0)
    jl = lax.broadcasted_iota(jnp.int32, (n, w4), 1) & (RW_HD - 1)
    same = (ti >> log_l) == (jl >> log_l)
    strict = same & (jl < ti)
    incl = same & (jl <= ti)
    eye = (jl == ti).astype(F32)
    lvl = [((ti >> (s + 1)) == (jl >> (s + 1))) & ((ti >> s) != (jl >> s)) & (jl < ti) for s in range(log_l)]
    log_hd = RW_HD.bit_length() - 1
    bmask = ((lax.broadcasted_iota(jnp.int32, (w4, w4), 0) >> log_hd)
             == (lax.broadcasted_iota(jnp.int32, (w4, w4), 1) >> log_hd))

    def bd(xb):
        return jnp.where(bmask, jnp.concatenate([xb] * hp, axis=0), jnp.zeros((), BF16))

    def pdot(a4, b4):
        return jnp.dot(a4.astype(BF16), bd(b4.astype(BF16)), preferred_element_type=F32)

    def to_bd(blocks):
        rows_ = []
        for i, blk in enumerate(blocks):
            parts = [blk if j == i else jnp.zeros((RW_HD, RW_HD), F32) for j in range(hp)]
            rows_.append(jnp.concatenate(parts, axis=1))
        return jnp.concatenate(rows_, axis=0)

    probs = [(g, q) for g in range(ngrp) for q in range(ngq)]

    def cut(x, g, q):
        return x[g * n:(g + 1) * n, q * w4:(q + 1) * w4]

    if carry:
        @pl.when(ci == 0)
        def _():
            for sq in range(nb):
                for q in range(ngq):
                    s_ref[sq, q] = to_bd([s0_ref[sq, q * hp + i] for i in range(hp)])
        s0 = {(g, q, j): s_ref[g * bb + j, q] for (g, q) in probs for j in range(bb)}
    else:
        s0 = {(g, q, j): to_bd([s0_ref[g * bb + j, q * hp + i] for i in range(hp)])
              for (g, q) in probs for j in range(bb)}

    ar, m_ba, m_ka, m_br, m_kr = {}, {}, {}, {}, {}
    for p in probs:
        ar[p] = jnp.concatenate([cut(a_t, *p), cut(r_t, *p)], axis=0).astype(BF16)
        mb = _nt(ar[p], bd(cut(b_t, *p).astype(BF16)))
        mk = _nt(ar[p], bd(cut(k_t, *p).astype(BF16)))
        m_ba[p] = jnp.where(strict, mb[0:n], 0.0)
        m_br[p] = jnp.where(incl, mb[n:2 * n], 0.0)
        m_ka[p] = jnp.where(strict, mk[0:n], 0.0)
        m_kr[p] = jnp.where(incl, mk[n:2 * n], 0.0)

    inv = {p: eye + jnp.where(lvl[0], m_ba[p], 0.0) for p in probs}
    for s in range(1, log_l):
        half = {p: pdot(inv[p], jnp.where(lvl[s], m_ba[p], 0.0)) for p in probs}
        inv = {p: inv[p] + pdot(half[p], inv[p]) for p in probs}

    xa, xr = {}, {}
    for p in probs:
        pa, pr = [], []
        for j in range(bb):
            arj = ar[p] if bb == 1 else jnp.concatenate(
                [ar[p][j * L:(j + 1) * L], ar[p][n + j * L:n + (j + 1) * L]], axis=0)
            as0 = _nt(arj, s0[p + (j,)].astype(BF16))
            pa.append(as0[0:L])
            pr.append(as0[L:2 * L])
        xa[p] = pa[0] if bb == 1 else jnp.concatenate(pa, axis=0)
        xr[p] = pr[0] if bb == 1 else jnp.concatenate(pr, axis=0)

    vv = {p: cut(v, *p) for p in probs}
    vbd = {p: bd(vv[p].astype(BF16)) for p in probs}
    u = {p: pdot(inv[p], xa[p] + jnp.dot(m_ka[p].astype(BF16), vbd[p], preferred_element_type=F32)) for p in probs}
    o = {p: xr[p] + pdot(m_br[p], u[p]) + jnp.dot(m_kr[p].astype(BF16), vbd[p], preferred_element_type=F32)
         for p in probs}

    s_new = {}
    for (g, q) in probs:
        bh, kh = cut(b_hat, g, q), cut(k_hat, g, q)
        for j in range(bb):
            js = slice(j * L, (j + 1) * L)
            uv = jnp.concatenate([u[(g, q)][js], vv[(g, q)][js]], axis=0)
            bk = jnp.concatenate([bh[js], kh[js]], axis=0)
            seq = g * bb + j
            decay = e_last[seq][:, q * w4:(q + 1) * w4]
            s_new[(g, q, j)] = jnp.where(bmask, s0[(g, q, j)] * decay + _dot_tn(uv, bk), 0.0)

    o_rows = [jnp.concatenate([o[(g, q)] for q in range(ngq)], axis=1) for g in range(ngrp)]
    o_all = o_rows[0] if ngrp == 1 else jnp.concatenate(o_rows, axis=0)
    oc = o_all - _head_sums(o_all, hsum) * (1.0 / RW_HD)
    on = oc * lax.rsqrt(_head_sums(oc * oc, hsum) * (1.0 / RW_HD) + RW_LN_EPS)
    y = (on * lng_ref[...] + lnb_ref[...] + bonus) * gate
    y_ref[...] = y.reshape(nb, L, c).astype(BF16)

    def diag_block(m, i):
        return m[i * RW_HD:(i + 1) * RW_HD, i * RW_HD:(i + 1) * RW_HD]

    if carry:
        for (g, q, j), val in s_new.items():
            s_ref[g * bb + j, q] = val

        @pl.when(ci == pl.num_programs(1) - 1)
        def _():
            for sq in range(nb):
                for q in range(ngq):
                    for i in range(hp):
                        sout_ref[sq, q * hp + i] = diag_block(s_ref[sq, q], i)
    else:
        for (g, q, j), val in s_new.items():
            for i in range(hp):
                sout_ref[g * bb + j, q * hp + i] = diag_block(val, i)


def _rwkv(z3, sh_rkv, sh_tail, s_all, p, l, nb, L):
    bsz, tlen, _ = z3.shape
    c = MIX_W
    nc = tlen // L
    carry = nc > 1
    half = c // 2
    hsum = jnp.asarray(np.kron(np.eye(half // RW_HD, dtype=np.float32), np.ones((RW_HD, RW_HD), np.float32)), BF16)
    vec = lambda w=c: pl.BlockSpec((None, 1, w), lambda b, i: (l, 0, 0))
    lora = lambda: pl.BlockSpec((None, RW_TAIL_PAD, c), lambda b, i: (l, 0, 0))
    zspec = lambda blk: pl.BlockSpec((nb, L, c), lambda b, i, blk=blk: (b, i, blk))
    sspec = lambda: pl.BlockSpec((None, nb, RW_H, RW_HD, RW_HD), lambda b, i: (l, b, 0, 0, 0))
    kern = functools.partial(_rwkv_kernel, nb=nb, L=L, carry=carry)
    return pl.pallas_call(
        kern,
        grid=(bsz // nb, nc),
        in_specs=[zspec(ZB_R), zspec(ZB_K), zspec(ZB_V),
                  pl.BlockSpec((nb, L, RW_TAIL_PAD), lambda b, i: (b, i, Z_TAIL // RW_TAIL_PAD)),
                  pl.BlockSpec((None, nb, 1, RW_RKV), lambda b, i: (l, b, 0, 0)),
                  pl.BlockSpec((None, nb, 1, RW_TAIL_PAD), lambda b, i: (l, b, 0, 0)),
                  sspec(),
                  vec(RW_RKV), vec(RW_TAIL_PAD), vec(), vec(), vec(), vec(), vec(), vec(), vec(),
                  lora(), lora(), lora(),
                  pl.BlockSpec((half, half), lambda b, i: (0, 0))],
        out_specs=[pl.BlockSpec((nb, L, c), lambda b, i: (b, i, 0)), sspec()],
        out_shape=[jax.ShapeDtypeStruct((bsz, tlen, c), BF16),
                   jax.ShapeDtypeStruct(s_all.shape, F32)],
        input_output_aliases={6: 1},
        scratch_shapes=[pltpu.VMEM((nb, 1, RW_RKV + RW_TAIL_PAD), F32),
                        pltpu.VMEM((nb, RW_H // RW_PACK, RW_PACK * RW_HD, RW_PACK * RW_HD) if carry
                                   else (1, 1, 8, 128), F32)],
        compiler_params=_cparams(("parallel", "arbitrary")),
        name="rwkv7",
    )(z3, z3, z3, z3, sh_rkv, sh_tail, s_all,
      p["rwkv_mu_rkv"], p["rwkv_mu_tail"], p["rwkv_w0"], p["rwkv_a0"], p["rwkv_kk"], p["rwkv_ka"], p["rwkv_rk"],
      p["rwkv_ln_g"], p["rwkv_ln_b"], p["rwkv_w2"], p["rwkv_a2"], p["rwkv_g2"], hsum)


ML_ROWS = 128


def _mlstm_kernel(zx_ref, zv_ref, zo_ref, zif_ref, zift_ref, buf_ref, c0_ref, n0_ref, m0_ref,
                  cw_ref, cb_ref, wq_ref, wk_ref, brow_ref, bcol_ref, ng_ref, skip_ref,
                  y_ref, cout_ref, nout_ref, mout_ref, xs_ref, c_ref, n_ref, m_ref, *, nb, L, carry):
    ci = pl.program_id(1)
    c = MIX_W
    rows = nb * L
    log_l = L.bit_length() - 1

    @pl.when(ci == 0)
    def _():
        xs_ref[:, 5:8, :] = buf_ref[...]
        if carry:
            c_ref[...] = c0_ref[...]
            n_ref[...] = n0_ref[...]
            m_ref[...] = m0_ref[...]

    if carry:
        @pl.when(ci > 0)
        def _():
            xs_ref[:, 5:8, :] = xs_ref[:, L + 5:L + 8, :]

    xs_ref[:, 8:8 + L, :] = zx_ref[...]
    conv = cb_ref[...] + cw_ref[0:1, :] * xs_ref[:, 5:5 + L, :]
    for j in range(1, CONV_W):
        conv = conv + cw_ref[j:j + 1, :] * xs_ref[:, 5 + j:5 + j + L, :]
    cc = jax.nn.silu(conv).reshape(rows, c)
    zv = zv_ref[...].reshape(rows, c)
    zo = zo_ref[...].reshape(rows, c)

    ri = lax.broadcasted_iota(jnp.int32, (rows, rows), 0)
    cj = lax.broadcasted_iota(jnp.int32, (rows, rows), 1)
    same = (ri >> log_l) == (cj >> log_l)
    causal = same & (cj <= ri)

    gate_col = zif_ref[...].reshape(rows, 128) + brow_ref[...]
    bcum_col = _dot_ones(_log_sigmoid(gate_col), causal.astype(BF16), 3, ones_on_left=True)
    gate_row = zift_ref[...] + bcol_ref[...]
    bcum_row = _dot_ones(_log_sigmoid(gate_row), (same & (ri <= cj)).astype(BF16), 3)

    def per_row(x3):
        return jnp.broadcast_to(x3, (nb, L, x3.shape[-1])).reshape(rows, x3.shape[-1])

    def last(x):
        return x.reshape(nb, L, x.shape[-1])[:, L - 1:L, :]

    c_in, n_in, m_in = (c_ref, n_ref, m_ref) if carry else (c0_ref, n0_ref, m0_ref)
    c_out, n_out, m_out = (c_ref, n_ref, m_ref) if carry else (cout_ref, nout_ref, mout_ref)
    heads = range(ML_H)
    sls = [slice(h * ML_HD, (h + 1) * ML_HD) for h in heads]
    cmat = {(h, j): c_in[j, h] for h in heads for j in range(nb)}
    nvec = [n_in[:, h:h + 1, :] for h in heads]
    m_prev = [m_in[:, h:h + 1, 0:1] for h in heads]

    ch = [cc[:, sls[h]] for h in heads]
    q = [_dot(ch[h], wq_ref[h]) for h in heads]
    k = [_dot(ch[h], wk_ref[h]) * (ML_HD ** -0.5) for h in heads]
    v = [zv[:, sls[h]] for h in heads]
    qk = [_dot_nt(q[h], k[h]) for h in heads]
    qc = []
    for h in heads:
        parts = [_dot_nt(q[h][j * L:(j + 1) * L], cmat[(h, j)]) for j in range(nb)]
        qc.append(parts[0] if nb == 1 else jnp.concatenate(parts, axis=0))

    b_col = [bcum_col[:, ML_H + h:ML_H + h + 1] for h in heads]
    li_col = [gate_col[:, h:h + 1] for h in heads]
    m_t, s, sc = [], [], []
    for h in heads:
        log_d = jnp.where(causal, b_col[h] - bcum_row[ML_H + h:ML_H + h + 1, :] + gate_row[h:h + 1, :], -jnp.inf)
        inter = b_col[h] + per_row(m_prev[h])
        mt = jnp.maximum(jnp.max(log_d, axis=-1, keepdims=True), inter)
        m_t.append(mt)
        s.append(qk[h] * jnp.exp(log_d - mt))
        sc.append(jnp.exp(inter - mt))

    for h in heads:
        num = _dot(s[h], v[h]) + sc[h] * qc[h]
        den = (jnp.sum(s[h], axis=-1, keepdims=True)
               + sc[h] * jnp.sum(q[h] * per_row(nvec[h]), axis=-1, keepdims=True))
        hh = num / jnp.maximum(jnp.abs(den), jnp.exp(-m_t[h]))
        hc = hh - jnp.mean(hh, axis=-1, keepdims=True)
        hn = hc * lax.rsqrt(jnp.mean(hc * hc, axis=-1, keepdims=True) + 1e-6) * ng_ref[:, sls[h]]
        y = jax.nn.sigmoid(zo[:, sls[h]]) * (hn + skip_ref[:, sls[h]] * ch[h])
        y_ref[:, :, sls[h]] = y.reshape(nb, L, ML_HD).astype(BF16)

    for h in heads:
        m_new = last(m_t[h])
        b_last = last(b_col[h])
        wj = jnp.exp(per_row(b_last - m_new) - b_col[h] + li_col[h])
        dec = jnp.exp(b_last + m_prev[h] - m_new)
        wv = wj * v[h]
        for j in range(nb):
            js = slice(j * L, (j + 1) * L)
            c_out[j, h] = dec[j] * cmat[(h, j)] + _dot_tn(wv[js], k[h][js])
        n_out[:, h:h + 1, :] = dec * nvec[h] + jnp.sum((wj * k[h]).reshape(nb, L, ML_HD), axis=1, keepdims=True)
        m_out[:, h:h + 1, :] = jnp.broadcast_to(m_new, (nb, 1, ML_HD))

    if carry:
        @pl.when(ci == pl.num_programs(1) - 1)
        def _():
            cout_ref[...] = c_ref[...]
            nout_ref[...] = n_ref[...]
            mout_ref[...] = m_ref[...]


def _mlstm(z3, zift, buf, c_all, n_all, m_all, p, l, nb, L):
    bsz, tlen, _ = z3.shape
    c = MIX_W
    nc = tlen // L
    rows = nb * L
    carry = nc > 1
    bcol = jnp.broadcast_to(p["mlstm_bif"][l][:, None], (8, rows))
    vec = lambda: pl.BlockSpec((None, 1, c), lambda b, i: (l, 0, 0))
    zspec = lambda blk: pl.BlockSpec((nb, L, c), lambda b, i, blk=blk: (b, i, blk))
    hmat = lambda: pl.BlockSpec((None, ML_H, ML_HD, ML_HD), lambda b, i: (l, 0, 0, 0))
    cspec = lambda: pl.BlockSpec((None, nb, ML_H, ML_HD, ML_HD), lambda b, i: (l, b, 0, 0, 0))
    nspec = lambda: pl.BlockSpec((None, nb, ML_H, ML_HD), lambda b, i: (l, b, 0, 0))
    kern = functools.partial(_mlstm_kernel, nb=nb, L=L, carry=carry)
    tiny = (1, 8, 128)
    return pl.pallas_call(
        kern,
        grid=(bsz // nb, nc),
        in_specs=[zspec(ZB_MX), zspec(ZB_MV), zspec(ZB_MO),
                  pl.BlockSpec((nb, L, 128), lambda b, i: (b, i, Z_MIF // 128)),
                  pl.BlockSpec((None, 8, rows), lambda b, i: (b * nc + i, 0, 0)),
                  pl.BlockSpec((None, nb, CONV_W - 1, c), lambda b, i: (l, b, 0, 0)),
                  cspec(), nspec(), nspec(),
                  pl.BlockSpec((None, CONV_W, c), lambda b, i: (l, 0, 0)),
                  vec(), hmat(), hmat(),
                  pl.BlockSpec((None, 1, 128), lambda b, i: (l, 0, 0)),
                  pl.BlockSpec((8, rows), lambda b, i: (0, 0)),
                  vec(), vec()],
        out_specs=[pl.BlockSpec((nb, L, c), lambda b, i: (b, i, 0)), cspec(), nspec(), nspec()],
        out_shape=[jax.ShapeDtypeStruct((bsz, tlen, c), BF16),
                   jax.ShapeDtypeStruct(c_all.shape, F32),
                   jax.ShapeDtypeStruct(n_all.shape, F32),
                   jax.ShapeDtypeStruct(m_all.shape, F32)],
        input_output_aliases={6: 1, 7: 2, 8: 3},
        scratch_shapes=[pltpu.VMEM((nb, 8 + L, c), F32),
                        pltpu.VMEM((nb, ML_H, ML_HD, ML_HD) if carry else tiny, F32),
                        pltpu.VMEM((nb, ML_H, ML_HD) if carry else tiny, F32),
                        pltpu.VMEM((nb, ML_H, ML_HD) if carry else tiny, F32)],
        compiler_params=_cparams(("parallel", "arbitrary")),
        name="mlstm",
    )(z3, z3, z3, z3, zift, buf, c_all, n_all, m_all,
      p["mlstm_conv_w"], p["mlstm_conv_b"], p["mlstm_wq"], p["mlstm_wk"], p["mlstm_brow"], bcol,
      p["mlstm_norm_g"], p["mlstm_skip"])


def _block_diag(w):
    dp, nb, d, _ = w.shape
    eye = jnp.eye(nb, dtype=w.dtype)
    return jnp.einsum("lnij,nm->lnimj", w, eye).reshape(dp, nb * d, nb * d)


SRC_TAIL = 3584
SRC_ML = SRC_TAIL + RW_TAIL
SRC_MIF = 5280
PACK_ROWS = 512


def _pack_mix_kernel(a_ref, mif_ref, o_ref):
    j = pl.program_id(1)
    last = pl.num_programs(1) - 1

    @pl.when(j < last)
    def _():
        o_ref[...] = a_ref[0].astype(BF16)

    @pl.when(j == last)
    def _():
        pad = lambda n: jnp.zeros((n, D_MODEL), F32)
        tile = jnp.concatenate([a_ref[0, 0:RW_TAIL, :], pad(RW_TAIL_PAD - RW_TAIL),
                                mif_ref[0], pad(PACK_ROWS - RW_TAIL_PAD - 8)], axis=0)
        o_ref[...] = tile.astype(BF16)


def _pack_mix(w_in_t):
    n_plain = SRC_TAIL // PACK_ROWS
    n_tiles = Z_COLS // PACK_ROWS

    def src_row(l, j):
        shifted = SRC_ML + (j - n_plain) * PACK_ROWS
        row = jnp.where(j < n_plain, j * PACK_ROWS, jnp.where(j < n_tiles - 1, shifted, SRC_TAIL))
        return (l, pl.multiple_of(row, 8), 0)

    return pl.pallas_call(
        _pack_mix_kernel,
        grid=(DEPTH, n_tiles),
        in_specs=[pl.BlockSpec((pl.Element(1), pl.Element(PACK_ROWS), pl.Element(D_MODEL)), src_row),
                  pl.BlockSpec((pl.Element(1), pl.Element(8), pl.Element(D_MODEL)), lambda l, j: (l, SRC_MIF, 0))],
        out_specs=pl.BlockSpec((None, PACK_ROWS, D_MODEL), lambda l, j: (l, j, 0)),
        out_shape=jax.ShapeDtypeStruct((DEPTH, Z_COLS, D_MODEL), BF16),
        compiler_params=_cparams(("parallel", "arbitrary")),
        name="pack_mix",
    )(w_in_t, w_in_t)


def _prepare(raw):
    p = {}
    wt = jnp.swapaxes(raw["w_in"], 1, 2)
    p["w_mix"] = _pack_mix(wt)
    p["w_in_t"] = wt
    for name in ("w_branch", "w_out", "w_ff_out"):
        p[name] = raw[name].astype(BF16)
    p["w_ff_in"] = raw["w_ff_in"]
    row = lambda a: a.reshape(DEPTH, 1, -1)
    for name in ("norm_pre_mix", "norm_post_mix", "norm_pre_ffn", "norm_post_ffn",
                 "lru_conv_b", "lru_ba", "lru_bx", "lru_lambda", "gmlp_ln_g", "gmlp_ln_b",
                 "rwkv_w0", "rwkv_a0", "rwkv_kk", "rwkv_ka", "rwkv_rk", "rwkv_ln_g", "rwkv_ln_b",
                 "mlstm_conv_b", "mlstm_norm_g", "mlstm_skip"):
        p[name] = row(raw[name])
    p["lru_conv_w"] = raw["lru_conv_w"]
    p["mlstm_conv_w"] = raw["mlstm_conv_w"]
    p["lru_wa"] = _block_diag(raw["lru_wa"]).astype(BF16)
    p["lru_wx"] = _block_diag(raw["lru_wx"]).astype(BF16)
    mu = raw["rwkv_mu"]
    p["rwkv_mu_rkv"] = row(mu[:, :RW_RKV])
    p["rwkv_mu_tail"] = row(jnp.pad(mu[:, RW_RKV:], ((0, 0), (0, RW_TAIL_PAD - RW_TAIL))))

    def lora(w, lo):
        return jnp.pad(w, ((0, 0), (lo, RW_TAIL_PAD - lo - w.shape[1]), (0, 0))).astype(BF16)

    p["rwkv_w2"] = lora(raw["rwkv_w2"], 0)
    p["rwkv_a2"] = lora(raw["rwkv_a2"], 32)
    p["rwkv_g2"] = lora(raw["rwkv_g2"], 64)
    p["mlstm_wq"] = raw["mlstm_wq"].astype(BF16)
    p["mlstm_wk"] = raw["mlstm_wk"].astype(BF16)
    bif = jnp.concatenate([raw["mlstm_bi"], raw["mlstm_bf"]], axis=-1)
    p["mlstm_bif"] = bif
    p["mlstm_brow"] = row(jnp.pad(bif, ((0, 0), (0, 128 - 8))))
    p["gmlp_ws"] = raw["gmlp_ws"]
    p["gmlp_bs"] = raw["gmlp_bs"]
    return p


def _gmlp_mix_weights(p, tlen):
    L = min(GM_CHUNK, tlen)
    rep = GM_CHUNK // L
    ws = jnp.tril(p["gmlp_ws"][:, :, :L, :L])
    eye = jnp.eye(rep, dtype=ws.dtype)
    ws_mix = jnp.einsum("lgps,ab->lgapbs", ws, eye).reshape(DEPTH, GM_GROUPS, GM_CHUNK, GM_CHUNK).astype(BF16)
    bias = jnp.swapaxes(p["gmlp_bs"][:, :, :L], 1, 2)
    bias = jnp.repeat(bias, MIX_W // GM_GROUPS, axis=2)
    bias = jnp.tile(bias, (1, rep, 1))
    return ws_mix, bias


def _group_forward(x3, states, p, is_start, depth=DEPTH):
    bsz, tlen, _ = x3.shape
    m = bsz * tlen
    x = x3.reshape(m, D_MODEL)
    lru_buf, lru_h, rw_shift, rw_s, ml_buf, ml_c, ml_n, ml_m = states
    tm = min(TM_DENSE, m)
    if is_start:
        lru_bb, lru_tt = min(LRU_SEQS_P, bsz), min(LRU_TILE_P, tlen)
        rw_nb, rw_l, ml_nb, ml_l = min(RW_SEQS_P, bsz), RW_CHUNK, 1, ML_CHUNK_P
    else:
        lru_bb, lru_tt, rw_nb, rw_l, ml_nb, ml_l = 16, tlen, RW_ROWS // tlen, tlen, ML_ROWS // tlen, tlen
    ws_mix, gm_bias = _gmlp_mix_weights(p, tlen)
    nl = lru_h.shape[0]
    h_all = lru_h.reshape(nl, bsz, 1, MIX_W)
    sh_rkv = rw_shift[:, :, :RW_RKV].reshape(nl, bsz, 1, RW_RKV)
    sh_tail = jnp.pad(rw_shift[:, :, RW_RKV:], ((0, 0), (0, 0), (0, RW_TAIL_PAD - RW_TAIL)))
    sh_tail = sh_tail.reshape(nl, bsz, 1, RW_TAIL_PAD)
    s_all, c_all, n_all = rw_s, ml_c, ml_n
    m_all = jnp.broadcast_to(ml_m[:, :, :, None], (nl, bsz, ML_H, ML_HD))
    new_states = [[] for _ in range(8)]
    gm_vs = []
    for l in range(depth):
        z, hn = _inproj(x, p["norm_pre_mix"], p["w_mix"], l, tm)
        z3 = z.reshape(bsz, tlen, Z_COLS)

        y_a, h_all = _lru(z3, lru_buf, h_all, p, l, is_start, lru_bb, lru_tt)
        y_b, gm_v = _gmlp(z, p, l, ws_mix, gm_bias)
        y_c, s_all = _rwkv(z3, sh_rkv, sh_tail, s_all, p, l, rw_nb, rw_l)
        y_c = y_c.reshape(m, MIX_W)
        ml_rows = ml_nb * ml_l
        zift = jnp.swapaxes(z[:, Z_MIF:Z_MIF + 8].reshape(m // ml_rows, ml_rows, 8), 1, 2)
        y_d, c_all, n_all, m_all = _mlstm(z3, zift, ml_buf, c_all, n_all, m_all, p, l, ml_nb, ml_l)
        y_d = y_d.reshape(m, MIX_W)

        merged = _merge(hn, (y_a.reshape(m, MIX_W), y_b, y_c, y_d), p["w_in_t"], p["w_branch"], l, tm)
        x1, hf = _outproj(merged, p["w_out"], x, p["norm_post_mix"], p["norm_pre_ffn"], l, min(TM_OUTPROJ, m))
        hmid = _ffup(hf, p["w_ff_in"], l, tm)
        x = _ffdown(hmid, p["w_ff_out"], x1, p["norm_post_ffn"], l, min(TM_FFDOWN, m))

        new_states[0].append(z3[:, tlen - (CONV_W - 1):, 0:MIX_W])
        new_states[2].append(jnp.concatenate(
            [z3[:, tlen - 1, ZB_R * MIX_W:ZB_R * MIX_W + RW_RKV], z3[:, tlen - 1, Z_TAIL:Z_TAIL + RW_TAIL]], axis=-1))
        new_states[4].append(z3[:, tlen - (CONV_W - 1):, ZB_MX * MIX_W:(ZB_MX + 1) * MIX_W])
        gm_vs.append(gm_v.reshape(bsz, tlen, MIX_W))
    out_states = [jnp.stack(new_states[0], axis=0), h_all.reshape(nl, bsz, MIX_W)[:depth],
                  jnp.stack(new_states[2], axis=0), s_all[:depth],
                  jnp.stack(new_states[4], axis=0), c_all[:depth], n_all[:depth], m_all[:depth, :, :, 0]]
    return x.reshape(bsz, tlen, D_MODEL), out_states, jnp.stack(gm_vs, axis=0)


def kernel(x_prompt, x_sample, state_lru_conv, state_lru_h, state_rwkv_shift, state_rwkv_wkv, state_mlstm_conv, state_mlstm_C, state_mlstm_n, state_mlstm_m, norm_pre_mix, norm_post_mix, norm_pre_ffn, norm_post_ffn, w_in, lru_conv_w, lru_conv_b, lru_wa, lru_ba, lru_wx, lru_bx, lru_lambda, gmlp_ln_g, gmlp_ln_b, gmlp_ws, gmlp_bs, rwkv_mu, rwkv_w0, rwkv_w2, rwkv_a0, rwkv_a2, rwkv_g2, rwkv_kk, rwkv_ka, rwkv_rk, rwkv_ln_g, rwkv_ln_b, mlstm_conv_w, mlstm_conv_b, mlstm_wq, mlstm_wk, mlstm_bi, mlstm_bf, mlstm_norm_g, mlstm_skip, w_branch, w_out, w_ff_in, w_ff_out):
    raw = dict(norm_pre_mix=norm_pre_mix, norm_post_mix=norm_post_mix, norm_pre_ffn=norm_pre_ffn,
               norm_post_ffn=norm_post_ffn, w_in=w_in, lru_conv_w=lru_conv_w, lru_conv_b=lru_conv_b,
               lru_wa=lru_wa, lru_ba=lru_ba, lru_wx=lru_wx, lru_bx=lru_bx, lru_lambda=lru_lambda,
               gmlp_ln_g=gmlp_ln_g, gmlp_ln_b=gmlp_ln_b, gmlp_ws=gmlp_ws, gmlp_bs=gmlp_bs,
               rwkv_mu=rwkv_mu, rwkv_w0=rwkv_w0, rwkv_w2=rwkv_w2, rwkv_a0=rwkv_a0, rwkv_a2=rwkv_a2,
               rwkv_g2=rwkv_g2, rwkv_kk=rwkv_kk, rwkv_ka=rwkv_ka, rwkv_rk=rwkv_rk, rwkv_ln_g=rwkv_ln_g,
               rwkv_ln_b=rwkv_ln_b, mlstm_conv_w=mlstm_conv_w, mlstm_conv_b=mlstm_conv_b, mlstm_wq=mlstm_wq,
               mlstm_wk=mlstm_wk, mlstm_bi=mlstm_bi, mlstm_bf=mlstm_bf, mlstm_norm_g=mlstm_norm_g,
               mlstm_skip=mlstm_skip, w_branch=w_branch, w_out=w_out, w_ff_in=w_ff_in, w_ff_out=w_ff_out)
    p = _prepare(raw)
    bp = x_prompt.shape[0]
    zero = lambda *s: jnp.zeros((DEPTH, bp) + s, F32)
    prompt_states = (zero(CONV_W - 1, MIX_W), zero(MIX_W), zero(RW_RKV + RW_TAIL), zero(RW_H, RW_HD, RW_HD),
                     zero(CONV_W - 1, MIX_W), zero(ML_H, ML_HD, ML_HD), zero(ML_H, ML_HD), zero(ML_H))
    sample_states = (state_lru_conv, state_lru_h, state_rwkv_shift, state_rwkv_wkv,
                     state_mlstm_conv, state_mlstm_C, state_mlstm_n, state_mlstm_m)
    yp, st_p, _ = _group_forward(x_prompt, prompt_states, p, True)
    ys, st_s, gm_v = _group_forward(x_sample, sample_states, p, False)
    return (yp, ys, *st_p, *st_s, gm_v)
```

```python
import functools

import numpy as np
import jax
import jax.numpy as jnp
from jax import lax
from jax.experimental import pallas as pl
from jax.experimental.pallas import tpu as pltpu

F32 = jnp.float32
BF16 = jnp.bfloat16

D_MODEL = 2048
DEPTH = 4
MIX_W = 512
CONV_W = 4
LRU_BLOCKS = 8
LRU_C = 8.0
GM_CHUNK = 128
GM_GROUPS = 4
RW_HD = 64
RW_H = 8
RW_RKV = 3 * MIX_W
RW_TAIL = 160
RW_TAIL_PAD = 256
RW_LN_EPS = 64e-5
ML_H = 4
ML_HD = 128
D_FF = 5632
N_GATE = 4 * D_MODEL
P_SRC = 5288
Z_COLS = 5632

ZB_LX, ZB_LG, ZB_GU, ZB_GV, ZB_R, ZB_K, ZB_V, ZB_MX, ZB_MV, ZB_MO = range(10)
Z_TAIL = 5120
Z_MIF = 5376

RW_CHUNK = 64
RW_SEQS_P = 4
ML_CHUNK_P = 256
GM_TILE_ROWS = 512
TM_DENSE = 1024
TM_OUTPROJ = 512
OUTPROJ_SUB_ROWS = 256
FFUP_SUB_ROWS = 256
FFDOWN_SUB_ROWS = 128
TM_FFDOWN = 512
TK_FFDOWN = 2816
TN_MERGE = 256
LRU_SEQS_P = 4
LRU_TILE_P = 256
VMEM_LIMIT = 56 * 1024 * 1024
HI = lax.Precision.HIGHEST


def _cparams(sem):
    return pltpu.CompilerParams(dimension_semantics=sem, vmem_limit_bytes=VMEM_LIMIT)


def _softplus(x):
    return jnp.maximum(x, 0.0) + jnp.log1p(jnp.exp(-jnp.abs(x)))


def _log_sigmoid(x):
    return -_softplus(-x)


def _rms(x, g):
    return x * lax.rsqrt(jnp.mean(x * x, axis=-1, keepdims=True) + 1e-6) * g


def _dot(a, b):
    return jnp.dot(a.astype(BF16), b.astype(BF16), preferred_element_type=F32)


def _nt(a, b):
    return lax.dot_general(a, b, (((1,), (1,)), ((), ())), preferred_element_type=F32)


def _dot_nt(a, b):
    return lax.dot_general(a.astype(BF16), b.astype(BF16), (((1,), (1,)), ((), ())),
                           preferred_element_type=F32)


def _dot_tn(a, b):
    return lax.dot_general(a.astype(BF16), b.astype(BF16), (((0,), (0,)), ((), ())),
                           preferred_element_type=F32)


def _prenorm_kernel(x_ref, g_ref, hn_ref):
    hn_ref[...] = _rms(x_ref[...], g_ref[...]).astype(BF16)


def _prenorm(x, g, l, tm):
    m = x.shape[0]
    row_spec = pl.BlockSpec((tm, D_MODEL), lambda i: (i, 0))
    return pl.pallas_call(
        _prenorm_kernel,
        grid=(m // tm,),
        in_specs=[row_spec, pl.BlockSpec((None, 1, D_MODEL), lambda i: (l, 0, 0))],
        out_specs=row_spec,
        out_shape=jax.ShapeDtypeStruct((m, D_MODEL), BF16),
        compiler_params=_cparams(("parallel",)),
        name="prenorm",
    )(x, g)


def _inproj_kernel(hn_ref, w_ref, z_ref):
    z_ref[...] = _nt(hn_ref[...], w_ref[...])


def _inproj(hn, w, l, tm, tn=512):
    m = hn.shape[0]
    return pl.pallas_call(
        _inproj_kernel,
        grid=(m // tm, Z_COLS // tn),
        in_specs=[pl.BlockSpec((tm, D_MODEL), lambda i, j: (i, 0)),
                  pl.BlockSpec((None, tn, D_MODEL), lambda i, j: (l, j, 0))],
        out_specs=pl.BlockSpec((tm, tn), lambda i, j: (i, j)),
        out_shape=jax.ShapeDtypeStruct((m, Z_COLS), F32),
        compiler_params=_cparams(("parallel", "arbitrary")),
        name="inproj",
    )(hn, w)


def _merge_kernel(hn_ref, ya_ref, yb_ref, yc_ref, yd_ref, g0_ref, g1_ref, g2_ref, g3_ref, wb_ref, o_ref,
                  gb_ref, wbb_ref):
    @pl.when(pl.program_id(1) == 0)
    def _():
        for b, g_ref in enumerate((g0_ref, g1_ref, g2_ref, g3_ref)):
            gb_ref[b] = g_ref[0].astype(BF16)
        wbb_ref[...] = wb_ref[...].astype(BF16)

    hn = hn_ref[...]
    acc = None
    for b, y_ref in enumerate((ya_ref, yb_ref, yc_ref, yd_ref)):
        zg = _nt(hn, gb_ref[b])
        br = jnp.dot(y_ref[...], wbb_ref[b], preferred_element_type=F32)
        term = jax.nn.sigmoid(zg) * br
        acc = term if acc is None else acc + term
    o_ref[...] = acc.astype(BF16)


def _merge(hn, ys, w_in_t, wbranch, l, tm, tn=TN_MERGE):
    m = hn.shape[0]
    nb = D_MODEL // tn
    y_spec = pl.BlockSpec((tm, MIX_W), lambda j, i: (i, 0))
    g_specs = [pl.BlockSpec((pl.Element(1), pl.Element(tn), pl.Element(D_MODEL)),
                            lambda j, i, b=b: (l, pl.multiple_of(P_SRC + b * D_MODEL + j * tn, 8), 0))
               for b in range(4)]
    return pl.pallas_call(
        _merge_kernel,
        grid=(nb, m // tm),
        in_specs=[pl.BlockSpec((tm, D_MODEL), lambda j, i: (i, 0)), y_spec, y_spec, y_spec, y_spec,
                  *g_specs,
                  pl.BlockSpec((None, 4, MIX_W, tn), lambda j, i: (l, 0, 0, j))],
        out_specs=pl.BlockSpec((tm, tn), lambda j, i: (i, j)),
        out_shape=jax.ShapeDtypeStruct((m, D_MODEL), BF16),
        scratch_shapes=[pltpu.VMEM((4, tn, D_MODEL), BF16), pltpu.VMEM((4, MIX_W, tn), BF16)],
        compiler_params=_cparams(("parallel", "arbitrary")),
        name="merge",
    )(hn, *ys, w_in_t, w_in_t, w_in_t, w_in_t, wbranch)


def _outproj_kernel(mg_ref, w_ref, x_ref, gpost_ref, gpre_ref, x1_ref, hf_ref):
    sub = OUTPROJ_SUB_ROWS
    for r0 in range(0, mg_ref.shape[0], sub):
        rs = slice(r0, r0 + sub)
        mix = jnp.dot(mg_ref[rs, :], w_ref[...], preferred_element_type=F32)
        x1 = x_ref[rs, :] + _rms(mix, gpost_ref[...])
        x1_ref[rs, :] = x1
        hf_ref[rs, :] = _rms(x1, gpre_ref[...]).astype(BF16)


def _outproj(merged, w_out, x, g_post, g_pre_ffn, l, tm):
    m = x.shape[0]
    g_spec = pl.BlockSpec((None, 1, D_MODEL), lambda i: (l, 0, 0))
    row_spec = pl.BlockSpec((tm, D_MODEL), lambda i: (i, 0))
    return pl.pallas_call(
        _outproj_kernel,
        grid=(m // tm,),
        in_specs=[row_spec, pl.BlockSpec((None, D_MODEL, D_MODEL), lambda i: (l, 0, 0)), row_spec, g_spec, g_spec],
        out_specs=[row_spec, row_spec],
        out_shape=[jax.ShapeDtypeStruct((m, D_MODEL), F32), jax.ShapeDtypeStruct((m, D_MODEL), BF16)],
        compiler_params=_cparams(("parallel",)),
        name="outproj",
    )(merged, w_out, x, g_post, g_pre_ffn)


def _ffup_kernel(hf_ref, wg_ref, wu_ref, o_ref, wgb_ref, wub_ref):
    @pl.when(pl.program_id(1) == 0)
    def _():
        wgb_ref[...] = wg_ref[...].astype(BF16)
        wub_ref[...] = wu_ref[...].astype(BF16)

    sub = min(FFUP_SUB_ROWS, hf_ref.shape[0])
    for r0 in range(0, hf_ref.shape[0], sub):
        rs = slice(r0, r0 + sub)
        hf = hf_ref[rs, :]
        g = jnp.dot(hf, wgb_ref[...], preferred_element_type=F32)
        u = jnp.dot(hf, wub_ref[...], preferred_element_type=F32)
        o_ref[rs, :] = (jax.nn.silu(g) * u).astype(BF16)


def _ffup(hf, w_ff_in, l, tm, tn=512):
    m = hf.shape[0]
    nb = D_FF // tn
    return pl.pallas_call(
        _ffup_kernel,
        grid=(nb, m // tm),
        in_specs=[pl.BlockSpec((tm, D_MODEL), lambda j, i: (i, 0)),
                  pl.BlockSpec((None, D_MODEL, tn), lambda j, i: (l, 0, j)),
                  pl.BlockSpec((None, D_MODEL, tn), lambda j, i: (l, 0, nb + j))],
        out_specs=pl.BlockSpec((tm, tn), lambda j, i: (i, j)),
        out_shape=jax.ShapeDtypeStruct((m, D_FF), BF16),
        scratch_shapes=[pltpu.VMEM((D_MODEL, tn), BF16), pltpu.VMEM((D_MODEL, tn), BF16)],
        compiler_params=_cparams(("parallel", "arbitrary")),
        name="ffup",
    )(hf, w_ff_in, w_ff_in)


def _ffdown_kernel(h_ref, w_ref, x1_ref, g_ref, gnext_ref, o_ref, hn_ref, acc_ref):
    k = pl.program_id(1)
    last = pl.num_programs(1) - 1

    @pl.when(k == 0)
    def _():
        acc_ref[...] = jnp.dot(h_ref[...], w_ref[...], preferred_element_type=F32)

    @pl.when((k > 0) & (k < last))
    def _():
        acc_ref[...] += jnp.dot(h_ref[...], w_ref[...], preferred_element_type=F32)

    @pl.when(k == last)
    def _():
        sub = FFDOWN_SUB_ROWS
        for r0 in range(0, acc_ref.shape[0], sub):
            rs = slice(r0, r0 + sub)
            ff = acc_ref[rs, :] + jnp.dot(h_ref[rs, :], w_ref[...], preferred_element_type=F32)
            x2 = x1_ref[rs, :] + _rms(ff, g_ref[...])
            o_ref[rs, :] = x2
            hn_ref[rs, :] = _rms(x2, gnext_ref[...]).astype(BF16)


def _ffdown(h, w_ff_out, x1, g_post, g_pre_mix, l, tm, tk=TK_FFDOWN):
    m = x1.shape[0]
    l_next = min(l + 1, DEPTH - 1)
    row_spec = pl.BlockSpec((tm, D_MODEL), lambda i, k: (i, 0))
    return pl.pallas_call(
        _ffdown_kernel,
        grid=(m // tm, D_FF // tk),
        in_specs=[pl.BlockSpec((tm, tk), lambda i, k: (i, k)),
                  pl.BlockSpec((None, tk, D_MODEL), lambda i, k: (l, k, 0)),
                  row_spec,
                  pl.BlockSpec((None, 1, D_MODEL), lambda i, k: (l, 0, 0)),
                  pl.BlockSpec((None, 1, D_MODEL), lambda i, k: (l_next, 0, 0))],
        out_specs=[row_spec, row_spec],
        out_shape=[jax.ShapeDtypeStruct((m, D_MODEL), F32), jax.ShapeDtypeStruct((m, D_MODEL), BF16)],
        scratch_shapes=[pltpu.VMEM((tm, D_MODEL), F32)],
        compiler_params=_cparams(("parallel", "arbitrary")),
        name="ffdown",
    )(h, w_ff_out, x1, g_post, g_pre_mix)


def _lru_kernel(zx_ref, zg_ref, buf_ref, h0_ref, cw_ref, cb_ref, wa_ref, ba_ref, wx_ref, bx_ref, lam_ref,
                y_ref, hout_ref, xs_ref, a_ref, b_ref, h_ref, *, is_start, bb, tt):
    t = pl.program_id(1)
    c = MIX_W

    @pl.when(t == 0)
    def _():
        xs_ref[:, 5:8, :] = buf_ref[...]
        h_ref[...] = jnp.broadcast_to(h0_ref[...], (bb, 8, c))

    @pl.when(t > 0)
    def _():
        xs_ref[:, 5:8, :] = xs_ref[:, tt + 5:tt + 8, :]

    xs_ref[:, 8:8 + tt, :] = zx_ref[...]
    xc = cb_ref[...] + cw_ref[0:1, :] * xs_ref[:, 5:5 + tt, :]
    for j in range(1, CONV_W):
        xc = xc + cw_ref[j:j + 1, :] * xs_ref[:, 5 + j:5 + j + tt, :]
    xc2 = xc.reshape(bb * tt, c)
    r = jax.nn.sigmoid(_dot(xc2, wa_ref[...]) + ba_ref[...])
    i = jax.nn.sigmoid(_dot(xc2, wx_ref[...]) + bx_ref[...])
    log_a = LRU_C * r * _log_sigmoid(lam_ref[...])
    a = jnp.exp(log_a)
    mult = jnp.sqrt(1.0 - jnp.exp(2.0 * log_a))
    if is_start:
        tpos = lax.broadcasted_iota(jnp.int32, (bb, tt, c), 1).reshape(bb * tt, c) + t * tt
        mult = jnp.where(tpos == 0, 1.0, mult)
    a_ref[...] = a.reshape(bb, tt, c)
    b_ref[...] = (mult * i * xc2).reshape(bb, tt, c)

    row = lax.broadcasted_iota(jnp.int32, (bb, 8, c), 1).reshape(bb * 8, c)

    def group(gi, carry):
        off = pl.multiple_of(gi * 8, 8)
        av = a_ref[:, pl.ds(off, 8), :].reshape(bb * 8, c)
        bv = b_ref[:, pl.ds(off, 8), :].reshape(bb * 8, c)
        for s in (1, 2, 4):
            keep = row >= s
            a_sh = pltpu.roll(av, s, 0)
            b_sh = pltpu.roll(bv, s, 0)
            bv = jnp.where(keep, av * b_sh + bv, bv)
            av = jnp.where(keep, av * a_sh, av)
        hh = (av * h_ref[...].reshape(bb * 8, c) + bv).reshape(bb, 8, c)
        b_ref[:, pl.ds(off, 8), :] = hh
        h_ref[...] = jnp.broadcast_to(hh[:, 7:8, :], (bb, 8, c))
        return carry

    lax.fori_loop(0, tt // 8, group, 0)
    y_ref[...] = (b_ref[...] * jax.nn.gelu(zg_ref[...])).astype(BF16)

    @pl.when(t == pl.num_programs(1) - 1)
    def _():
        hout_ref[...] = h_ref[:, 7:8, :]


def _lru(z3, buf, h_all, p, l, is_start, bb, tt):
    bsz, tlen, _ = z3.shape
    c = MIX_W
    vec = lambda: pl.BlockSpec((None, 1, c), lambda b, t: (l, 0, 0))
    mat = lambda: pl.BlockSpec((None, c, c), lambda b, t: (l, 0, 0))
    hspec = lambda: pl.BlockSpec((None, bb, 1, c), lambda b, t: (l, b, 0, 0))
    kern = functools.partial(_lru_kernel, is_start=is_start, bb=bb, tt=tt)
    return pl.pallas_call(
        kern,
        grid=(bsz // bb, tlen // tt),
        in_specs=[pl.BlockSpec((bb, tt, c), lambda b, t: (b, t, ZB_LX)),
                  pl.BlockSpec((bb, tt, c), lambda b, t: (b, t, ZB_LG)),
                  pl.BlockSpec((None, bb, CONV_W - 1, c), lambda b, t: (l, b, 0, 0)),
                  hspec(),
                  pl.BlockSpec((None, CONV_W, c), lambda b, t: (l, 0, 0)),
                  vec(), mat(), vec(), mat(), vec(), vec()],
        out_specs=[pl.BlockSpec((bb, tt, c), lambda b, t: (b, t, 0)), hspec()],
        out_shape=[jax.ShapeDtypeStruct((bsz, tlen, c), BF16),
                   jax.ShapeDtypeStruct(h_all.shape, F32)],
        input_output_aliases={3: 1},
        scratch_shapes=[pltpu.VMEM((bb, 8 + tt, c), F32), pltpu.VMEM((bb, tt, c), F32),
                        pltpu.VMEM((bb, tt, c), F32), pltpu.VMEM((bb, 8, c), F32)],
        compiler_params=_cparams(("parallel", "arbitrary")),
        name="rglru",
    )(z3, z3, buf, h_all, p["lru_conv_w"], p["lru_conv_b"], p["lru_wa"], p["lru_ba"], p["lru_wx"], p["lru_bx"],
      p["lru_lambda"])


def _gmlp_kernel(zu_ref, zv_ref, lng_ref, lnb_ref, ws_ref, bias_ref, y_ref, v_ref):
    u = jax.nn.gelu(zu_ref[...])
    gv = jax.nn.gelu(zv_ref[...])
    vc = gv - jnp.mean(gv, axis=-1, keepdims=True)
    v = vc * lax.rsqrt(jnp.mean(vc * vc, axis=-1, keepdims=True) + 1e-5) * lng_ref[...] + lnb_ref[...]
    v_ref[...] = v
    gd = MIX_W // GM_GROUPS
    vb = v.astype(BF16)
    for ck in range(v.shape[0] // GM_CHUNK):
        rs = slice(ck * GM_CHUNK, (ck + 1) * GM_CHUNK)
        for g in range(GM_GROUPS):
            sl = slice(g * gd, (g + 1) * gd)
            s = jnp.dot(ws_ref[g], vb[rs, sl], preferred_element_type=F32) + bias_ref[:, sl]
            y_ref[rs, sl] = (u[rs, sl] * s).astype(BF16)


def _gmlp(z2, p, l, ws_mix, bias_tile):
    m = z2.shape[0]
    c = MIX_W
    rows = GM_TILE_ROWS
    vec = lambda: pl.BlockSpec((None, 1, c), lambda i: (l, 0, 0))
    return pl.pallas_call(
        _gmlp_kernel,
        grid=(m // rows,),
        in_specs=[pl.BlockSpec((rows, c), lambda i: (i, ZB_GU)),
                  pl.BlockSpec((rows, c), lambda i: (i, ZB_GV)),
                  vec(), vec(),
                  pl.BlockSpec((None, GM_GROUPS, GM_CHUNK, GM_CHUNK), lambda i: (l, 0, 0, 0)),
                  pl.BlockSpec((None, GM_CHUNK, c), lambda i: (l, 0, 0))],
        out_specs=[pl.BlockSpec((rows, c), lambda i: (i, 0)),
                   pl.BlockSpec((rows, c), lambda i: (i, 0))],
        out_shape=[jax.ShapeDtypeStruct((m, c), BF16), jax.ShapeDtypeStruct((m, c), F32)],
        compiler_params=_cparams(("parallel",)),
        name="gmlp",
    )(z2, z2, p["gmlp_ln_g"], p["gmlp_ln_b"], ws_mix, bias_tile)


RW_ROWS = 64
RW_PACK = 4


def _dot_ones(x, ones, pieces, ones_on_left=False):
    acc = None
    rem = x
    for i in range(pieces):
        part = rem.astype(BF16)
        if i + 1 < pieces:
            rem = rem - part.astype(F32)
        term = (jnp.dot(ones, part, preferred_element_type=F32) if ones_on_left
                else jnp.dot(part, ones, preferred_element_type=F32))
        acc = term if acc is None else acc + term
    return acc


def _head_sums(x, ones):
    half = MIX_W // 2
    rows = x.shape[0]
    stacked = jnp.concatenate([x[:, :half], x[:, half:]], axis=0)
    s = _dot_ones(stacked, ones, 2)
    return jnp.concatenate([s[:rows], s[rows:]], axis=1)


def _rwkv_kernel(zr_ref, zk_ref, zv_ref, zt_ref, shr_ref, sht_ref, s0_ref,
                 mur_ref, mut_ref, w0_ref, a0_ref, kkp_ref, kap_ref, rk_ref, lng_ref, lnb_ref,
                 w2_ref, a2_ref, g2_ref, hsum_ref,
                 y_ref, sout_ref, prev_ref, s_ref, *, nb, L, carry):
    ci = pl.program_id(1)
    c = MIX_W
    rows = nb * L
    bb = RW_ROWS // L
    ngrp = rows // RW_ROWS
    log_l = L.bit_length() - 1

    @pl.when(ci == 0)
    def _():
        prev_ref[:, :, 0:RW_RKV] = shr_ref[...]
        prev_ref[:, :, RW_RKV:RW_RKV + RW_TAIL_PAD] = sht_ref[...]

    first = (lax.broadcasted_iota(jnp.int32, (rows, 1), 0) & (L - 1)) == 0

    def shift(z_ref, lo, hi, mu):
        w = hi - lo
        z3 = z_ref[...]
        z = z3.reshape(rows, w)
        prev = jnp.broadcast_to(prev_ref[:, :, lo:hi], (nb, L, w)).reshape(rows, w)
        zp = jnp.where(first, prev, pltpu.roll(z, 1, 0))
        prev_ref[:, :, lo:hi] = z3[:, L - 1:L, :]
        return z + (zp - z) * mu

    r = shift(zr_ref, 0, c, mur_ref[:, 0:c])
    k = shift(zk_ref, c, 2 * c, mur_ref[:, c:2 * c])
    v = shift(zv_ref, 2 * c, 3 * c, mur_ref[:, 2 * c:3 * c])
    tl = shift(zt_ref, RW_RKV, RW_RKV + RW_TAIL_PAD, mut_ref[...])

    wlin = w0_ref[...] + _dot(jnp.tanh(tl), w2_ref[...])
    logw = -jnp.exp(-_softplus(-wlin) - 0.5)
    a = jax.nn.sigmoid(a0_ref[...] + _dot(tl, a2_ref[...]))
    gate = _dot(jax.nn.sigmoid(tl), g2_ref[...])
    kkf = k * kkp_ref[...]
    kmod = k * (1.0 + (a - 1.0) * kap_ref[...])
    hsum = hsum_ref[...]
    kkn = kkf / jnp.maximum(jnp.sqrt(_head_sums(kkf * kkf, hsum)), 1e-12)
    beta = kkn * a
    bonus = _head_sums(r * kmod * rk_ref[...], hsum) * v

    ri_f = lax.broadcasted_iota(jnp.int32, (rows, rows), 0)
    cj_f = lax.broadcasted_iota(jnp.int32, (rows, rows), 1)
    cum_op = (((ri_f >> log_l) == (cj_f >> log_l)) & (cj_f <= ri_f)).astype(BF16)
    cum = _dot_ones(logw, cum_op, 3, ones_on_left=True)
    c_last = cum.reshape(nb, L, c)[:, L - 1:L, :]
    e_last = jnp.exp(c_last)
    e_rest = jnp.exp(jnp.broadcast_to(c_last, (nb, L, c)).reshape(rows, c) - cum)
    e_mc = jnp.exp(-cum)
    a_t = -kkn * jnp.exp(cum - logw)
    r_t = r * jnp.exp(cum)
    b_t = beta * e_mc
    k_t = kmod * e_mc
    b_hat = beta * e_rest
    k_hat = kmod * e_rest

    n = RW_ROWS
    hp = RW_PACK
    w4 = hp * RW_HD
    ngq = RW_H // hp
    ti = lax.broadcasted_iota(jnp.int32, (n, w4), 0)
    jl = lax.broadcasted_iota(jnp.int32, (n, w4), 1) & (RW_HD - 1)
    same = (ti >> log_l) == (jl >> log_l)
    strict = same & (jl < ti)
    incl = same & (jl <= ti)
    eye = (jl == ti).astype(F32)
    lvl = [((ti >> (s + 1)) == (jl >> (s + 1))) & ((ti >> s) != (jl >> s)) & (jl < ti) for s in range(log_l)]
    log_hd = RW_HD.bit_length() - 1
    bmask = ((lax.broadcasted_iota(jnp.int32, (w4, w4), 0) >> log_hd)
             == (lax.broadcasted_iota(jnp.int32, (w4, w4), 1) >> log_hd))

    def bd(xb):
        return jnp.where(bmask, jnp.concatenate([xb] * hp, axis=0), jnp.zeros((), BF16))

    def pdot(a4, b4):
        return jnp.dot(a4.astype(BF16), bd(b4.astype(BF16)), preferred_element_type=F32)

    def to_bd(blocks):
        rows_ = []
        for i, blk in enumerate(blocks):
            parts = [blk if j == i else jnp.zeros((RW_HD, RW_HD), F32) for j in range(hp)]
            rows_.append(jnp.concatenate(parts, axis=1))
        return jnp.concatenate(rows_, axis=0)

    probs = [(g, q) for g in range(ngrp) for q in range(ngq)]

    def cut(x, g, q):
        return x[g * n:(g + 1) * n, q * w4:(q + 1) * w4]

    if carry:
        @pl.when(ci == 0)
        def _():
            for sq in range(nb):
                for q in range(ngq):
                    s_ref[sq, q] = to_bd([s0_ref[sq, q * hp + i] for i in range(hp)])
        s0 = {(g, q, j): s_ref[g * bb + j, q] for (g, q) in probs for j in range(bb)}
    else:
        s0 = {(g, q, j): to_bd([s0_ref[g * bb + j, q * hp + i] for i in range(hp)])
              for (g, q) in probs for j in range(bb)}

    ar, m_ba, m_ka, m_br, m_kr = {}, {}, {}, {}, {}
    for p in probs:
        ar[p] = jnp.concatenate([cut(a_t, *p), cut(r_t, *p)], axis=0).astype(BF16)
        mb = _nt(ar[p], bd(cut(b_t, *p).astype(BF16)))
        mk = _nt(ar[p], bd(cut(k_t, *p).astype(BF16)))
        m_ba[p] = jnp.where(strict, mb[0:n], 0.0)
        m_br[p] = jnp.where(incl, mb[n:2 * n], 0.0)
        m_ka[p] = jnp.where(strict, mk[0:n], 0.0)
        m_kr[p] = jnp.where(incl, mk[n:2 * n], 0.0)

    inv = {p: eye + jnp.where(lvl[0], m_ba[p], 0.0) for p in probs}
    for s in range(1, log_l):
        half = {p: pdot(inv[p], jnp.where(lvl[s], m_ba[p], 0.0)) for p in probs}
        inv = {p: inv[p] + pdot(half[p], inv[p]) for p in probs}

    xa, xr = {}, {}
    for p in probs:
        pa, pr = [], []
        for j in range(bb):
            arj = ar[p] if bb == 1 else jnp.concatenate(
                [ar[p][j * L:(j + 1) * L], ar[p][n + j * L:n + (j + 1) * L]], axis=0)
            as0 = _nt(arj, s0[p + (j,)].astype(BF16))
            pa.append(as0[0:L])
            pr.append(as0[L:2 * L])
        xa[p] = pa[0] if bb == 1 else jnp.concatenate(pa, axis=0)
        xr[p] = pr[0] if bb == 1 else jnp.concatenate(pr, axis=0)

    vv = {p: cut(v, *p) for p in probs}
    vbd = {p: bd(vv[p].astype(BF16)) for p in probs}
    u = {p: pdot(inv[p], xa[p] + jnp.dot(m_ka[p].astype(BF16), vbd[p], preferred_element_type=F32)) for p in probs}
    o = {p: xr[p] + pdot(m_br[p], u[p]) + jnp.dot(m_kr[p].astype(BF16), vbd[p], preferred_element_type=F32)
         for p in probs}

    s_new = {}
    for (g, q) in probs:
        bh, kh = cut(b_hat, g, q), cut(k_hat, g, q)
        for j in range(bb):
            js = slice(j * L, (j + 1) * L)
            uv = jnp.concatenate([u[(g, q)][js], vv[(g, q)][js]], axis=0)
            bk = jnp.concatenate([bh[js], kh[js]], axis=0)
            seq = g * bb + j
            decay = e_last[seq][:, q * w4:(q + 1) * w4]
            s_new[(g, q, j)] = jnp.where(bmask, s0[(g, q, j)] * decay + _dot_tn(uv, bk), 0.0)

    o_rows = [jnp.concatenate([o[(g, q)] for q in range(ngq)], axis=1) for g in range(ngrp)]
    o_all = o_rows[0] if ngrp == 1 else jnp.concatenate(o_rows, axis=0)
    oc = o_all - _head_sums(o_all, hsum) * (1.0 / RW_HD)
    on = oc * lax.rsqrt(_head_sums(oc * oc, hsum) * (1.0 / RW_HD) + RW_LN_EPS)
    y = (on * lng_ref[...] + lnb_ref[...] + bonus) * gate
    y_ref[...] = y.reshape(nb, L, c).astype(BF16)

    def diag_block(m, i):
        return m[i * RW_HD:(i + 1) * RW_HD, i * RW_HD:(i + 1) * RW_HD]

    if carry:
        for (g, q, j), val in s_new.items():
            s_ref[g * bb + j, q] = val

        @pl.when(ci == pl.num_programs(1) - 1)
        def _():
            for sq in range(nb):
                for q in range(ngq):
                    for i in range(hp):
                        sout_ref[sq, q * hp + i] = diag_block(s_ref[sq, q], i)
    else:
        for (g, q, j), val in s_new.items():
            for i in range(hp):
                sout_ref[g * bb + j, q * hp + i] = diag_block(val, i)


def _rwkv(z3, sh_rkv, sh_tail, s_all, p, l, nb, L):
    bsz, tlen, _ = z3.shape
    c = MIX_W
    nc = tlen // L
    carry = nc > 1
    half = c // 2
    hsum = jnp.asarray(np.kron(np.eye(half // RW_HD, dtype=np.float32), np.ones((RW_HD, RW_HD), np.float32)), BF16)
    vec = lambda w=c: pl.BlockSpec((None, 1, w), lambda b, i: (l, 0, 0))
    lora = lambda: pl.BlockSpec((None, RW_TAIL_PAD, c), lambda b, i: (l, 0, 0))
    zspec = lambda blk: pl.BlockSpec((nb, L, c), lambda b, i, blk=blk: (b, i, blk))
    sspec = lambda: pl.BlockSpec((None, nb, RW_H, RW_HD, RW_HD), lambda b, i: (l, b, 0, 0, 0))
    kern = functools.partial(_rwkv_kernel, nb=nb, L=L, carry=carry)
    return pl.pallas_call(
        kern,
        grid=(bsz // nb, nc),
        in_specs=[zspec(ZB_R), zspec(ZB_K), zspec(ZB_V),
                  pl.BlockSpec((nb, L, RW_TAIL_PAD), lambda b, i: (b, i, Z_TAIL // RW_TAIL_PAD)),
                  pl.BlockSpec((None, nb, 1, RW_RKV), lambda b, i: (l, b, 0, 0)),
                  pl.BlockSpec((None, nb, 1, RW_TAIL_PAD), lambda b, i: (l, b, 0, 0)),
                  sspec(),
                  vec(RW_RKV), vec(RW_TAIL_PAD), vec(), vec(), vec(), vec(), vec(), vec(), vec(),
                  lora(), lora(), lora(),
                  pl.BlockSpec((half, half), lambda b, i: (0, 0))],
        out_specs=[pl.BlockSpec((nb, L, c), lambda b, i: (b, i, 0)), sspec()],
        out_shape=[jax.ShapeDtypeStruct((bsz, tlen, c), BF16),
                   jax.ShapeDtypeStruct(s_all.shape, F32)],
        input_output_aliases={6: 1},
        scratch_shapes=[pltpu.VMEM((nb, 1, RW_RKV + RW_TAIL_PAD), F32),
                        pltpu.VMEM((nb, RW_H // RW_PACK, RW_PACK * RW_HD, RW_PACK * RW_HD) if carry
                                   else (1, 1, 8, 128), F32)],
        compiler_params=_cparams(("parallel", "arbitrary")),
        name="rwkv7",
    )(z3, z3, z3, z3, sh_rkv, sh_tail, s_all,
      p["rwkv_mu_rkv"], p["rwkv_mu_tail"], p["rwkv_w0"], p["rwkv_a0"], p["rwkv_kk"], p["rwkv_ka"], p["rwkv_rk"],
      p["rwkv_ln_g"], p["rwkv_ln_b"], p["rwkv_w2"], p["rwkv_a2"], p["rwkv_g2"], hsum)


ML_ROWS = 128


def _mlstm_kernel(zx_ref, zv_ref, zo_ref, zif_ref, zift_ref, buf_ref, c0_ref, n0_ref, m0_ref,
                  cw_ref, cb_ref, wq_ref, wk_ref, brow_ref, bcol_ref, ng_ref, skip_ref,
                  y_ref, cout_ref, nout_ref, mout_ref, xs_ref, c_ref, n_ref, m_ref, *, nb, L, carry):
    ci = pl.program_id(1)
    c = MIX_W
    rows = nb * L
    log_l = L.bit_length() - 1

    @pl.when(ci == 0)
    def _():
        xs_ref[:, 5:8, :] = buf_ref[...]
        if carry:
            c_ref[...] = c0_ref[...]
            n_ref[...] = n0_ref[...]
            m_ref[...] = m0_ref[...]

    if carry:
        @pl.when(ci > 0)
        def _():
            xs_ref[:, 5:8, :] = xs_ref[:, L + 5:L + 8, :]

    xs_ref[:, 8:8 + L, :] = zx_ref[...]
    conv = cb_ref[...] + cw_ref[0:1, :] * xs_ref[:, 5:5 + L, :]
    for j in range(1, CONV_W):
        conv = conv + cw_ref[j:j + 1, :] * xs_ref[:, 5 + j:5 + j + L, :]
    cc = jax.nn.silu(conv).reshape(rows, c)
    zv = zv_ref[...].reshape(rows, c)
    zo = zo_ref[...].reshape(rows, c)

    ri = lax.broadcasted_iota(jnp.int32, (rows, rows), 0)
    cj = lax.broadcasted_iota(jnp.int32, (rows, rows), 1)
    same = (ri >> log_l) == (cj >> log_l)
    causal = same & (cj <= ri)

    gate_col = zif_ref[...].reshape(rows, 128) + brow_ref[...]
    bcum_col = _dot_ones(_log_sigmoid(gate_col), causal.astype(BF16), 3, ones_on_left=True)
    gate_row = zift_ref[...] + bcol_ref[...]
    bcum_row = _dot_ones(_log_sigmoid(gate_row), (same & (ri <= cj)).astype(BF16), 3)

    def per_row(x3):
        return jnp.broadcast_to(x3, (nb, L, x3.shape[-1])).reshape(rows, x3.shape[-1])

    def last(x):
        return x.reshape(nb, L, x.shape[-1])[:, L - 1:L, :]

    c_in, n_in, m_in = (c_ref, n_ref, m_ref) if carry else (c0_ref, n0_ref, m0_ref)
    c_out, n_out, m_out = (c_ref, n_ref, m_ref) if carry else (cout_ref, nout_ref, mout_ref)
    heads = range(ML_H)
    sls = [slice(h * ML_HD, (h + 1) * ML_HD) for h in heads]
    cmat = {(h, j): c_in[j, h] for h in heads for j in range(nb)}
    nvec = [n_in[:, h:h + 1, :] for h in heads]
    m_prev = [m_in[:, h:h + 1, 0:1] for h in heads]

    ch = [cc[:, sls[h]] for h in heads]
    q = [_dot(ch[h], wq_ref[h]) for h in heads]
    k = [_dot(ch[h], wk_ref[h]) * (ML_HD ** -0.5) for h in heads]
    v = [zv[:, sls[h]] for h in heads]
    qk = [_dot_nt(q[h], k[h]) for h in heads]
    qc = []
    for h in heads:
        parts = [_dot_nt(q[h][j * L:(j + 1) * L], cmat[(h, j)]) for j in range(nb)]
        qc.append(parts[0] if nb == 1 else jnp.concatenate(parts, axis=0))

    b_col = [bcum_col[:, ML_H + h:ML_H + h + 1] for h in heads]
    li_col = [gate_col[:, h:h + 1] for h in heads]
    m_t, s, sc = [], [], []
    for h in heads:
        log_d = jnp.where(causal, b_col[h] - bcum_row[ML_H + h:ML_H + h + 1, :] + gate_row[h:h + 1, :], -jnp.inf)
        inter = b_col[h] + per_row(m_prev[h])
        mt = jnp.maximum(jnp.max(log_d, axis=-1, keepdims=True), inter)
        m_t.append(mt)
        s.append(qk[h] * jnp.exp(log_d - mt))
        sc.append(jnp.exp(inter - mt))

    for h in heads:
        num = _dot(s[h], v[h]) + sc[h] * qc[h]
        den = (jnp.sum(s[h], axis=-1, keepdims=True)
               + sc[h] * jnp.sum(q[h] * per_row(nvec[h]), axis=-1, keepdims=True))
        hh = num / jnp.maximum(jnp.abs(den), jnp.exp(-m_t[h]))
        hc = hh - jnp.mean(hh, axis=-1, keepdims=True)
        hn = hc * lax.rsqrt(jnp.mean(hc * hc, axis=-1, keepdims=True) + 1e-6) * ng_ref[:, sls[h]]
        y = jax.nn.sigmoid(zo[:, sls[h]]) * (hn + skip_ref[:, sls[h]] * ch[h])
        y_ref[:, :, sls[h]] = y.reshape(nb, L, ML_HD).astype(BF16)

    for h in heads:
        m_new = last(m_t[h])
        b_last = last(b_col[h])
        wj = jnp.exp(per_row(b_last - m_new) - b_col[h] + li_col[h])
        dec = jnp.exp(b_last + m_prev[h] - m_new)
        wv = wj * v[h]
        for j in range(nb):
            js = slice(j * L, (j + 1) * L)
            c_out[j, h] = dec[j] * cmat[(h, j)] + _dot_tn(wv[js], k[h][js])
        n_out[:, h:h + 1, :] = dec * nvec[h] + jnp.sum((wj * k[h]).reshape(nb, L, ML_HD), axis=1, keepdims=True)
        m_out[:, h:h + 1, :] = jnp.broadcast_to(m_new, (nb, 1, ML_HD))

    if carry:
        @pl.when(ci == pl.num_programs(1) - 1)
        def _():
            cout_ref[...] = c_ref[...]
            nout_ref[...] = n_ref[...]
            mout_ref[...] = m_ref[...]


def _mlstm(z3, zift, buf, c_all, n_all, m_all, p, l, nb, L):
    bsz, tlen, _ = z3.shape
    c = MIX_W
    nc = tlen // L
    rows = nb * L
    carry = nc > 1
    bcol = jnp.broadcast_to(p["mlstm_bif"][l][:, None], (8, rows))
    vec = lambda: pl.BlockSpec((None, 1, c), lambda b, i: (l, 0, 0))
    zspec = lambda blk: pl.BlockSpec((nb, L, c), lambda b, i, blk=blk: (b, i, blk))
    hmat = lambda: pl.BlockSpec((None, ML_H, ML_HD, ML_HD), lambda b, i: (l, 0, 0, 0))
    cspec = lambda: pl.BlockSpec((None, nb, ML_H, ML_HD, ML_HD), lambda b, i: (l, b, 0, 0, 0))
    nspec = lambda: pl.BlockSpec((None, nb, ML_H, ML_HD), lambda b, i: (l, b, 0, 0))
    kern = functools.partial(_mlstm_kernel, nb=nb, L=L, carry=carry)
    tiny = (1, 8, 128)
    return pl.pallas_call(
        kern,
        grid=(bsz // nb, nc),
        in_specs=[zspec(ZB_MX), zspec(ZB_MV), zspec(ZB_MO),
                  pl.BlockSpec((nb, L, 128), lambda b, i: (b, i, Z_MIF // 128)),
                  pl.BlockSpec((None, 8, rows), lambda b, i: (b * nc + i, 0, 0)),
                  pl.BlockSpec((None, nb, CONV_W - 1, c), lambda b, i: (l, b, 0, 0)),
                  cspec(), nspec(), nspec(),
                  pl.BlockSpec((None, CONV_W, c), lambda b, i: (l, 0, 0)),
                  vec(), hmat(), hmat(),
                  pl.BlockSpec((None, 1, 128), lambda b, i: (l, 0, 0)),
                  pl.BlockSpec((8, rows), lambda b, i: (0, 0)),
                  vec(), vec()],
        out_specs=[pl.BlockSpec((nb, L, c), lambda b, i: (b, i, 0)), cspec(), nspec(), nspec()],
        out_shape=[jax.ShapeDtypeStruct((bsz, tlen, c), BF16),
                   jax.ShapeDtypeStruct(c_all.shape, F32),
                   jax.ShapeDtypeStruct(n_all.shape, F32),
                   jax.ShapeDtypeStruct(m_all.shape, F32)],
        input_output_aliases={6: 1, 7: 2, 8: 3},
        scratch_shapes=[pltpu.VMEM((nb, 8 + L, c), F32),
                        pltpu.VMEM((nb, ML_H, ML_HD, ML_HD) if carry else tiny, F32),
                        pltpu.VMEM((nb, ML_H, ML_HD) if carry else tiny, F32),
                        pltpu.VMEM((nb, ML_H, ML_HD) if carry else tiny, F32)],
        compiler_params=_cparams(("parallel", "arbitrary")),
        name="mlstm",
    )(z3, z3, z3, z3, zift, buf, c_all, n_all, m_all,
      p["mlstm_conv_w"], p["mlstm_conv_b"], p["mlstm_wq"], p["mlstm_wk"], p["mlstm_brow"], bcol,
      p["mlstm_norm_g"], p["mlstm_skip"])


def _block_diag(w):
    dp, nb, d, _ = w.shape
    eye = jnp.eye(nb, dtype=w.dtype)
    return jnp.einsum("lnij,nm->lnimj", w, eye).reshape(dp, nb * d, nb * d)


SRC_TAIL = 3584
SRC_ML = SRC_TAIL + RW_TAIL
SRC_MIF = 5280
PACK_ROWS = 512


def _pack_mix_kernel(a_ref, mif_ref, o_ref):
    j = pl.program_id(1)
    last = pl.num_programs(1) - 1

    @pl.when(j < last)
    def _():
        o_ref[...] = a_ref[0].astype(BF16)

    @pl.when(j == last)
    def _():
        pad = lambda n: jnp.zeros((n, D_MODEL), F32)
        tile = jnp.concatenate([a_ref[0, 0:RW_TAIL, :], pad(RW_TAIL_PAD - RW_TAIL),
                                mif_ref[0], pad(PACK_ROWS - RW_TAIL_PAD - 8)], axis=0)
        o_ref[...] = tile.astype(BF16)


def _pack_mix(w_in_t):
    n_plain = SRC_TAIL // PACK_ROWS
    n_tiles = Z_COLS // PACK_ROWS

    def src_row(l, j):
        shifted = SRC_ML + (j - n_plain) * PACK_ROWS
        row = jnp.where(j < n_plain, j * PACK_ROWS, jnp.where(j < n_tiles - 1, shifted, SRC_TAIL))
        return (l, pl.multiple_of(row, 8), 0)

    return pl.pallas_call(
        _pack_mix_kernel,
        grid=(DEPTH, n_tiles),
        in_specs=[pl.BlockSpec((pl.Element(1), pl.Element(PACK_ROWS), pl.Element(D_MODEL)), src_row),
                  pl.BlockSpec((pl.Element(1), pl.Element(8), pl.Element(D_MODEL)), lambda l, j: (l, SRC_MIF, 0))],
        out_specs=pl.BlockSpec((None, PACK_ROWS, D_MODEL), lambda l, j: (l, j, 0)),
        out_shape=jax.ShapeDtypeStruct((DEPTH, Z_COLS, D_MODEL), BF16),
        compiler_params=_cparams(("parallel", "arbitrary")),
        name="pack_mix",
    )(w_in_t, w_in_t)


def _prepare(raw):
    p = {}
    wt = jnp.swapaxes(raw["w_in"], 1, 2)
    p["w_mix"] = _pack_mix(wt)
    p["w_in_t"] = wt
    for name in ("w_out", "w_ff_out"):
        p[name] = raw[name].astype(BF16)
    p["w_ff_in"] = raw["w_ff_in"]
    p["w_branch"] = raw["w_branch"]
    row = lambda a: a.reshape(DEPTH, 1, -1)
    for name in ("norm_pre_mix", "norm_post_mix", "norm_pre_ffn", "norm_post_ffn",
                 "lru_conv_b", "lru_ba", "lru_bx", "lru_lambda", "gmlp_ln_g", "gmlp_ln_b",
                 "rwkv_w0", "rwkv_a0", "rwkv_kk", "rwkv_ka", "rwkv_rk", "rwkv_ln_g", "rwkv_ln_b",
                 "mlstm_conv_b", "mlstm_norm_g", "mlstm_skip"):
        p[name] = row(raw[name])
    p["lru_conv_w"] = raw["lru_conv_w"]
    p["mlstm_conv_w"] = raw["mlstm_conv_w"]
    p["lru_wa"] = _block_diag(raw["lru_wa"]).astype(BF16)
    p["lru_wx"] = _block_diag(raw["lru_wx"]).astype(BF16)
    mu = raw["rwkv_mu"]
    p["rwkv_mu_rkv"] = row(mu[:, :RW_RKV])
    p["rwkv_mu_tail"] = row(jnp.pad(mu[:, RW_RKV:], ((0, 0), (0, RW_TAIL_PAD - RW_TAIL))))

    def lora(w, lo):
        return jnp.pad(w, ((0, 0), (lo, RW_TAIL_PAD - lo - w.shape[1]), (0, 0))).astype(BF16)

    p["rwkv_w2"] = lora(raw["rwkv_w2"], 0)
    p["rwkv_a2"] = lora(raw["rwkv_a2"], 32)
    p["rwkv_g2"] = lora(raw["rwkv_g2"], 64)
    p["mlstm_wq"] = raw["mlstm_wq"].astype(BF16)
    p["mlstm_wk"] = raw["mlstm_wk"].astype(BF16)
    bif = jnp.concatenate([raw["mlstm_bi"], raw["mlstm_bf"]], axis=-1)
    p["mlstm_bif"] = bif
    p["mlstm_brow"] = row(jnp.pad(bif, ((0, 0), (0, 128 - 8))))
    p["gmlp_ws"] = raw["gmlp_ws"]
    p["gmlp_bs"] = raw["gmlp_bs"]
    return p


def _gmlp_mix_weights(p, tlen):
    L = min(GM_CHUNK, tlen)
    rep = GM_CHUNK // L
    ws = jnp.tril(p["gmlp_ws"][:, :, :L, :L])
    eye = jnp.eye(rep, dtype=ws.dtype)
    ws_mix = jnp.einsum("lgps,ab->lgapbs", ws, eye).reshape(DEPTH, GM_GROUPS, GM_CHUNK, GM_CHUNK).astype(BF16)
    bias = jnp.swapaxes(p["gmlp_bs"][:, :, :L], 1, 2)
    bias = jnp.repeat(bias, MIX_W // GM_GROUPS, axis=2)
    bias = jnp.tile(bias, (1, rep, 1))
    return ws_mix, bias


def _group_forward(x3, states, p, is_start, depth=DEPTH):
    bsz, tlen, _ = x3.shape
    m = bsz * tlen
    x = x3.reshape(m, D_MODEL)
    lru_buf, lru_h, rw_shift, rw_s, ml_buf, ml_c, ml_n, ml_m = states
    tm = min(TM_DENSE, m)
    if is_start:
        lru_bb, lru_tt = min(LRU_SEQS_P, bsz), min(LRU_TILE_P, tlen)
        rw_nb, rw_l, ml_nb, ml_l = min(RW_SEQS_P, bsz), RW_CHUNK, 1, ML_CHUNK_P
    else:
        lru_bb, lru_tt, rw_nb, rw_l, ml_nb, ml_l = 16, tlen, RW_ROWS // tlen, tlen, ML_ROWS // tlen, tlen
    ws_mix, gm_bias = _gmlp_mix_weights(p, tlen)
    nl = lru_h.shape[0]
    h_all = lru_h.reshape(nl, bsz, 1, MIX_W)
    sh_rkv = rw_shift[:, :, :RW_RKV].reshape(nl, bsz, 1, RW_RKV)
    sh_tail = jnp.pad(rw_shift[:, :, RW_RKV:], ((0, 0), (0, 0), (0, RW_TAIL_PAD - RW_TAIL)))
    sh_tail = sh_tail.reshape(nl, bsz, 1, RW_TAIL_PAD)
    s_all, c_all, n_all = rw_s, ml_c, ml_n
    m_all = jnp.broadcast_to(ml_m[:, :, :, None], (nl, bsz, ML_H, ML_HD))
    new_states = [[] for _ in range(8)]
    gm_vs = []
    hn = _prenorm(x, p["norm_pre_mix"], 0, tm)
    for l in range(depth):
        z = _inproj(hn, p["w_mix"], l, tm)
        z3 = z.reshape(bsz, tlen, Z_COLS)

        y_a, h_all = _lru(z3, lru_buf, h_all, p, l, is_start, lru_bb, lru_tt)
        y_b, gm_v = _gmlp(z, p, l, ws_mix, gm_bias)
        y_c, s_all = _rwkv(z3, sh_rkv, sh_tail, s_all, p, l, rw_nb, rw_l)
        y_c = y_c.reshape(m, MIX_W)
        ml_rows = ml_nb * ml_l
        zift = jnp.swapaxes(z[:, Z_MIF:Z_MIF + 8].reshape(m // ml_rows, ml_rows, 8), 1, 2)
        y_d, c_all, n_all, m_all = _mlstm(z3, zift, ml_buf, c_all, n_all, m_all, p, l, ml_nb, ml_l)
        y_d = y_d.reshape(m, MIX_W)

        merged = _merge(hn, (y_a.reshape(m, MIX_W), y_b, y_c, y_d), p["w_in_t"], p["w_branch"], l, tm)
        x1, hf = _outproj(merged, p["w_out"], x, p["norm_post_mix"], p["norm_pre_ffn"], l, min(TM_OUTPROJ, m))
        hmid = _ffup(hf, p["w_ff_in"], l, tm)
        x, hn = _ffdown(hmid, p["w_ff_out"], x1, p["norm_post_ffn"], p["norm_pre_mix"], l, min(TM_FFDOWN, m))

        new_states[0].append(z3[:, tlen - (CONV_W - 1):, 0:MIX_W])
        new_states[2].append(jnp.concatenate(
            [z3[:, tlen - 1, ZB_R * MIX_W:ZB_R * MIX_W + RW_RKV], z3[:, tlen - 1, Z_TAIL:Z_TAIL + RW_TAIL]], axis=-1))
        new_states[4].append(z3[:, tlen - (CONV_W - 1):, ZB_MX * MIX_W:(ZB_MX + 1) * MIX_W])
        gm_vs.append(gm_v.reshape(bsz, tlen, MIX_W))
    out_states = [jnp.stack(new_states[0], axis=0), h_all.reshape(nl, bsz, MIX_W)[:depth],
                  jnp.stack(new_states[2], axis=0), s_all[:depth],
                  jnp.stack(new_states[4], axis=0), c_all[:depth], n_all[:depth], m_all[:depth, :, :, 0]]
    return x.reshape(bsz, tlen, D_MODEL), out_states, jnp.stack(gm_vs, axis=0)


def kernel(x_prompt, x_sample, state_lru_conv, state_lru_h, state_rwkv_shift, state_rwkv_wkv, state_mlstm_conv, state_mlstm_C, state_mlstm_n, state_mlstm_m, norm_pre_mix, norm_post_mix, norm_pre_ffn, norm_post_ffn, w_in, lru_conv_w, lru_conv_b, lru_wa, lru_ba, lru_wx, lru_bx, lru_lambda, gmlp_ln_g, gmlp_ln_b, gmlp_ws, gmlp_bs, rwkv_mu, rwkv_w0, rwkv_w2, rwkv_a0, rwkv_a2, rwkv_g2, rwkv_kk, rwkv_ka, rwkv_rk, rwkv_ln_g, rwkv_ln_b, mlstm_conv_w, mlstm_conv_b, mlstm_wq, mlstm_wk, mlstm_bi, mlstm_bf, mlstm_norm_g, mlstm_skip, w_branch, w_out, w_ff_in, w_ff_out):
    raw = dict(norm_pre_mix=norm_pre_mix, norm_post_mix=norm_post_mix, norm_pre_ffn=norm_pre_ffn,
               norm_post_ffn=norm_post_ffn, w_in=w_in, lru_conv_w=lru_conv_w, lru_conv_b=lru_conv_b,
               lru_wa=lru_wa, lru_ba=lru_ba, lru_wx=lru_wx, lru_bx=lru_bx, lru_lambda=lru_lambda,
               gmlp_ln_g=gmlp_ln_g, gmlp_ln_b=gmlp_ln_b, gmlp_ws=gmlp_ws, gmlp_bs=gmlp_bs,
               rwkv_mu=rwkv_mu, rwkv_w0=rwkv_w0, rwkv_w2=rwkv_w2, rwkv_a0=rwkv_a0, rwkv_a2=rwkv_a2,
               rwkv_g2=rwkv_g2, rwkv_kk=rwkv_kk, rwkv_ka=rwkv_ka, rwkv_rk=rwkv_rk, rwkv_ln_g=rwkv_ln_g,
               rwkv_ln_b=rwkv_ln_b, mlstm_conv_w=mlstm_conv_w, mlstm_conv_b=mlstm_conv_b, mlstm_wq=mlstm_wq,
               mlstm_wk=mlstm_wk, mlstm_bi=mlstm_bi, mlstm_bf=mlstm_bf, mlstm_norm_g=mlstm_norm_g,
               mlstm_skip=mlstm_skip, w_branch=w_branch, w_out=w_out, w_ff_in=w_ff_in, w_ff_out=w_ff_out)
    p = _prepare(raw)
    bp = x_prompt.shape[0]
    zero = lambda *s: jnp.zeros((DEPTH, bp) + s, F32)
    prompt_states = (zero(CONV_W - 1, MIX_W), zero(MIX_W), zero(RW_RKV + RW_TAIL), zero(RW_H, RW_HD, RW_HD),
                     zero(CONV_W - 1, MIX_W), zero(ML_H, ML_HD, ML_HD), zero(ML_H, ML_HD), zero(ML_H))
    sample_states = (state_lru_conv, state_lru_h, state_rwkv_shift, state_rwkv_wkv,
                     state_mlstm_conv, state_mlstm_C, state_mlstm_n, state_mlstm_m)
    yp, st_p, _ = _group_forward(x_prompt, prompt_states, p, True)
    ys, st_s, gm_v = _group_forward(x_sample, sample_states, p, False)
    return (yp, ys, *st_p, *st_s, gm_v)
```

```python
import functools

import numpy as np
import jax
import jax.numpy as jnp
from jax import lax
from jax.experimental import pallas as pl
from jax.experimental.pallas import tpu as pltpu

F32 = jnp.float32
BF16 = jnp.bfloat16

D_MODEL = 2048
DEPTH = 4
MIX_W = 512
CONV_W = 4
LRU_BLOCKS = 8
LRU_C = 8.0
GM_CHUNK = 128
GM_GROUPS = 4
RW_HD = 64
RW_H = 8
RW_RKV = 3 * MIX_W
RW_TAIL = 160
RW_TAIL_PAD = 256
RW_LN_EPS = 64e-5
ML_H = 4
ML_HD = 128
D_FF = 5632
N_GATE = 4 * D_MODEL
P_SRC = 5288
Z_COLS = 5632

ZB_LX, ZB_LG, ZB_GU, ZB_GV, ZB_R, ZB_K, ZB_V, ZB_MX, ZB_MV, ZB_MO = range(10)
Z_TAIL = 5120
Z_MIF = 5376

RW_CHUNK = 64
RW_SEQS_P = 4
ML_CHUNK_P = 256
GM_TILE_ROWS = 512
TM_DENSE = 1024
TM_OUTPROJ = 512
OUTPROJ_SUB_ROWS = 256
FFUP_SUB_ROWS = 256
FFDOWN_SUB_ROWS = 256
TM_FFDOWN = 512
TN_MERGE = 256
LRU_SEQS_P = 4
LRU_TILE_P = 256
VMEM_LIMIT = 60 * 1024 * 1024
HI = lax.Precision.HIGHEST


def _cparams(sem):
    return pltpu.CompilerParams(dimension_semantics=sem, vmem_limit_bytes=VMEM_LIMIT)


def _softplus(x):
    return jnp.maximum(x, 0.0) + jnp.log1p(jnp.exp(-jnp.abs(x)))


def _log_sigmoid(x):
    return -_softplus(-x)


def _rms(x, g):
    return x * lax.rsqrt(jnp.mean(x * x, axis=-1, keepdims=True) + 1e-6) * g


def _dot(a, b):
    return jnp.dot(a.astype(BF16), b.astype(BF16), preferred_element_type=F32)


def _nt(a, b):
    return lax.dot_general(a, b, (((1,), (1,)), ((), ())), preferred_element_type=F32)


def _dot_nt(a, b):
    return lax.dot_general(a.astype(BF16), b.astype(BF16), (((1,), (1,)), ((), ())),
                           preferred_element_type=F32)


def _dot_tn(a, b):
    return lax.dot_general(a.astype(BF16), b.astype(BF16), (((0,), (0,)), ((), ())),
                           preferred_element_type=F32)


def _prenorm_kernel(x_ref, g_ref, hn_ref):
    hn_ref[...] = _rms(x_ref[...], g_ref[...]).astype(BF16)


def _prenorm(x, g, l, tm):
    m = x.shape[0]
    row_spec = pl.BlockSpec((tm, D_MODEL), lambda i: (i, 0))
    return pl.pallas_call(
        _prenorm_kernel,
        grid=(m // tm,),
        in_specs=[row_spec, pl.BlockSpec((None, 1, D_MODEL), lambda i: (l, 0, 0))],
        out_specs=row_spec,
        out_shape=jax.ShapeDtypeStruct((m, D_MODEL), BF16),
        compiler_params=_cparams(("parallel",)),
        name="prenorm",
    )(x, g)


def _inproj_kernel(hn_ref, w_ref, z_ref):
    tn = z_ref.shape[1]
    rows = pl.multiple_of(pl.program_id(1) * tn, tn)
    z_ref[...] = _nt(hn_ref[...], w_ref[pl.ds(rows, tn), :])


def _inproj(hn, w, l, tm, tn=512):
    m = hn.shape[0]
    return pl.pallas_call(
        _inproj_kernel,
        grid=(m // tm, Z_COLS // tn),
        in_specs=[pl.BlockSpec((tm, D_MODEL), lambda i, j: (i, 0)),
                  pl.BlockSpec((None, Z_COLS, D_MODEL), lambda i, j: (l, 0, 0), pipeline_mode=pl.Buffered(1))],
        out_specs=pl.BlockSpec((tm, tn), lambda i, j: (i, j)),
        out_shape=jax.ShapeDtypeStruct((m, Z_COLS), F32),
        compiler_params=_cparams(("parallel", "arbitrary")),
        name="inproj",
    )(hn, w)


def _merge_kernel(hn_ref, ya_ref, yb_ref, yc_ref, yd_ref, g0_ref, g1_ref, g2_ref, g3_ref, wb_ref, o_ref,
                  gb_ref, wbb_ref):
    @pl.when(pl.program_id(1) == 0)
    def _():
        for b, g_ref in enumerate((g0_ref, g1_ref, g2_ref, g3_ref)):
            gb_ref[b] = g_ref[0].astype(BF16)
        wbb_ref[...] = wb_ref[...].astype(BF16)

    hn = hn_ref[...]
    acc = None
    for b, y_ref in enumerate((ya_ref, yb_ref, yc_ref, yd_ref)):
        zg = _nt(hn, gb_ref[b])
        br = jnp.dot(y_ref[...], wbb_ref[b], preferred_element_type=F32)
        term = jax.nn.sigmoid(zg) * br
        acc = term if acc is None else acc + term
    o_ref[...] = acc.astype(BF16)


def _merge(hn, ys, w_in_t, wbranch, l, tm, tn=TN_MERGE):
    m = hn.shape[0]
    nb = D_MODEL // tn
    y_spec = pl.BlockSpec((tm, MIX_W), lambda j, i: (i, 0))
    g_specs = [pl.BlockSpec((pl.Element(1), pl.Element(tn), pl.Element(D_MODEL)),
                            lambda j, i, b=b: (l, pl.multiple_of(P_SRC + b * D_MODEL + j * tn, 8), 0))
               for b in range(4)]
    return pl.pallas_call(
        _merge_kernel,
        grid=(nb, m // tm),
        in_specs=[pl.BlockSpec((tm, D_MODEL), lambda j, i: (i, 0)), y_spec, y_spec, y_spec, y_spec,
                  *g_specs,
                  pl.BlockSpec((None, 4, MIX_W, tn), lambda j, i: (l, 0, 0, j))],
        out_specs=pl.BlockSpec((tm, tn), lambda j, i: (i, j)),
        out_shape=jax.ShapeDtypeStruct((m, D_MODEL), BF16),
        scratch_shapes=[pltpu.VMEM((4, tn, D_MODEL), BF16), pltpu.VMEM((4, MIX_W, tn), BF16)],
        compiler_params=_cparams(("parallel", "arbitrary")),
        name="merge",
    )(hn, *ys, w_in_t, w_in_t, w_in_t, w_in_t, wbranch)


def _outproj_kernel(mg_ref, w_ref, x_ref, gpost_ref, gpre_ref, x1_ref, hf_ref):
    sub = OUTPROJ_SUB_ROWS
    for r0 in range(0, mg_ref.shape[0], sub):
        rs = slice(r0, r0 + sub)
        mix = jnp.dot(mg_ref[rs, :], w_ref[...], preferred_element_type=F32)
        x1 = x_ref[rs, :] + _rms(mix, gpost_ref[...])
        x1_ref[rs, :] = x1
        hf_ref[rs, :] = _rms(x1, gpre_ref[...]).astype(BF16)


def _outproj(merged, w_out, x, g_post, g_pre_ffn, l, tm):
    m = x.shape[0]
    g_spec = pl.BlockSpec((None, 1, D_MODEL), lambda i: (l, 0, 0))
    row_spec = pl.BlockSpec((tm, D_MODEL), lambda i: (i, 0))
    return pl.pallas_call(
        _outproj_kernel,
        grid=(m // tm,),
        in_specs=[row_spec, pl.BlockSpec((None, D_MODEL, D_MODEL), lambda i: (l, 0, 0)), row_spec, g_spec, g_spec],
        out_specs=[row_spec, row_spec],
        out_shape=[jax.ShapeDtypeStruct((m, D_MODEL), F32), jax.ShapeDtypeStruct((m, D_MODEL), BF16)],
        compiler_params=_cparams(("parallel",)),
        name="outproj",
    )(merged, w_out, x, g_post, g_pre_ffn)


def _ffup_kernel(hf_ref, wg_ref, wu_ref, o_ref, wgb_ref, wub_ref):
    @pl.when(pl.program_id(1) == 0)
    def _():
        wgb_ref[...] = wg_ref[...].astype(BF16)
        wub_ref[...] = wu_ref[...].astype(BF16)

    sub = min(FFUP_SUB_ROWS, hf_ref.shape[0])
    for r0 in range(0, hf_ref.shape[0], sub):
        rs = slice(r0, r0 + sub)
        hf = hf_ref[rs, :]
        g = jnp.dot(hf, wgb_ref[...], preferred_element_type=F32)
        u = jnp.dot(hf, wub_ref[...], preferred_element_type=F32)
        o_ref[rs, :] = (jax.nn.silu(g) * u).astype(BF16)


def _ffup(hf, w_ff_in, l, tm, tn=512):
    m = hf.shape[0]
    nb = D_FF // tn
    return pl.pallas_call(
        _ffup_kernel,
        grid=(nb, m // tm),
        in_specs=[pl.BlockSpec((tm, D_MODEL), lambda j, i: (i, 0)),
                  pl.BlockSpec((None, D_MODEL, tn), lambda j, i: (l, 0, j)),
                  pl.BlockSpec((None, D_MODEL, tn), lambda j, i: (l, 0, nb + j))],
        out_specs=pl.BlockSpec((tm, tn), lambda j, i: (i, j)),
        out_shape=jax.ShapeDtypeStruct((m, D_FF), BF16),
        scratch_shapes=[pltpu.VMEM((D_MODEL, tn), BF16), pltpu.VMEM((D_MODEL, tn), BF16)],
        compiler_params=_cparams(("parallel", "arbitrary")),
        name="ffup",
    )(hf, w_ff_in, w_ff_in)


def _ffdown_kernel(h_ref, w_ref, x1_ref, g_ref, gnext_ref, o_ref, hn_ref):
    sub = FFDOWN_SUB_ROWS
    for r0 in range(0, h_ref.shape[0], sub):
        rs = slice(r0, r0 + sub)
        ff = jnp.dot(h_ref[rs, :], w_ref[...], preferred_element_type=F32)
        x2 = x1_ref[rs, :] + _rms(ff, g_ref[...])
        o_ref[rs, :] = x2
        hn_ref[rs, :] = _rms(x2, gnext_ref[...]).astype(BF16)


def _ffdown(h, w_ff_out, x1, g_post, g_pre_mix, l, tm):
    m = x1.shape[0]
    l_next = min(l + 1, DEPTH - 1)
    row_spec = pl.BlockSpec((tm, D_MODEL), lambda i: (i, 0))
    return pl.pallas_call(
        _ffdown_kernel,
        grid=(m // tm,),
        in_specs=[pl.BlockSpec((tm, D_FF), lambda i: (i, 0)),
                  pl.BlockSpec((None, D_FF, D_MODEL), lambda i: (l, 0, 0), pipeline_mode=pl.Buffered(1)),
                  row_spec,
                  pl.BlockSpec((None, 1, D_MODEL), lambda i: (l, 0, 0)),
                  pl.BlockSpec((None, 1, D_MODEL), lambda i: (l_next, 0, 0))],
        out_specs=[row_spec, row_spec],
        out_shape=[jax.ShapeDtypeStruct((m, D_MODEL), F32), jax.ShapeDtypeStruct((m, D_MODEL), BF16)],
        compiler_params=_cparams(("parallel",)),
        name="ffdown",
    )(h, w_ff_out, x1, g_post, g_pre_mix)


def _lru_kernel(zx_ref, zg_ref, buf_ref, h0_ref, cw_ref, cb_ref, wa_ref, ba_ref, wx_ref, bx_ref, lam_ref,
                y_ref, hout_ref, xs_ref, a_ref, b_ref, h_ref, *, is_start, bb, tt):
    t = pl.program_id(1)
    c = MIX_W

    @pl.when(t == 0)
    def _():
        xs_ref[:, 5:8, :] = buf_ref[...]
        h_ref[...] = jnp.broadcast_to(h0_ref[...], (bb, 8, c))

    @pl.when(t > 0)
    def _():
        xs_ref[:, 5:8, :] = xs_ref[:, tt + 5:tt + 8, :]

    xs_ref[:, 8:8 + tt, :] = zx_ref[...]
    xc = cb_ref[...] + cw_ref[0:1, :] * xs_ref[:, 5:5 + tt, :]
    for j in range(1, CONV_W):
        xc = xc + cw_ref[j:j + 1, :] * xs_ref[:, 5 + j:5 + j + tt, :]
    xc2 = xc.reshape(bb * tt, c)
    r = jax.nn.sigmoid(_dot(xc2, wa_ref[...]) + ba_ref[...])
    i = jax.nn.sigmoid(_dot(xc2, wx_ref[...]) + bx_ref[...])
    log_a = LRU_C * r * _log_sigmoid(lam_ref[...])
    a = jnp.exp(log_a)
    mult = jnp.sqrt(1.0 - jnp.exp(2.0 * log_a))
    if is_start:
        tpos = lax.broadcasted_iota(jnp.int32, (bb, tt, c), 1).reshape(bb * tt, c) + t * tt
        mult = jnp.where(tpos == 0, 1.0, mult)
    a_ref[...] = a.reshape(bb, tt, c)
    b_ref[...] = (mult * i * xc2).reshape(bb, tt, c)

    row = lax.broadcasted_iota(jnp.int32, (bb, 8, c), 1).reshape(bb * 8, c)

    def group(gi, carry):
        off = pl.multiple_of(gi * 8, 8)
        av = a_ref[:, pl.ds(off, 8), :].reshape(bb * 8, c)
        bv = b_ref[:, pl.ds(off, 8), :].reshape(bb * 8, c)
        for s in (1, 2, 4):
            keep = row >= s
            a_sh = pltpu.roll(av, s, 0)
            b_sh = pltpu.roll(bv, s, 0)
            bv = jnp.where(keep, av * b_sh + bv, bv)
            av = jnp.where(keep, av * a_sh, av)
        hh = (av * h_ref[...].reshape(bb * 8, c) + bv).reshape(bb, 8, c)
        b_ref[:, pl.ds(off, 8), :] = hh
        h_ref[...] = jnp.broadcast_to(hh[:, 7:8, :], (bb, 8, c))
        return carry

    lax.fori_loop(0, tt // 8, group, 0)
    y_ref[...] = (b_ref[...] * jax.nn.gelu(zg_ref[...])).astype(BF16)

    @pl.when(t == pl.num_programs(1) - 1)
    def _():
        hout_ref[...] = h_ref[:, 7:8, :]


def _lru(z3, buf, h_all, p, l, is_start, bb, tt):
    bsz, tlen, _ = z3.shape
    c = MIX_W
    vec = lambda: pl.BlockSpec((None, 1, c), lambda b, t: (l, 0, 0))
    mat = lambda: pl.BlockSpec((None, c, c), lambda b, t: (l, 0, 0))
    hspec = lambda: pl.BlockSpec((None, bb, 1, c), lambda b, t: (l, b, 0, 0))
    kern = functools.partial(_lru_kernel, is_start=is_start, bb=bb, tt=tt)
    return pl.pallas_call(
        kern,
        grid=(bsz // bb, tlen // tt),
        in_specs=[pl.BlockSpec((bb, tt, c), lambda b, t: (b, t, ZB_LX)),
                  pl.BlockSpec((bb, tt, c), lambda b, t: (b, t, ZB_LG)),
                  pl.BlockSpec((None, bb, CONV_W - 1, c), lambda b, t: (l, b, 0, 0)),
                  hspec(),
                  pl.BlockSpec((None, CONV_W, c), lambda b, t: (l, 0, 0)),
                  vec(), mat(), vec(), mat(), vec(), vec()],
        out_specs=[pl.BlockSpec((bb, tt, c), lambda b, t: (b, t, 0)), hspec()],
        out_shape=[jax.ShapeDtypeStruct((bsz, tlen, c), BF16),
                   jax.ShapeDtypeStruct(h_all.shape, F32)],
        input_output_aliases={3: 1},
        scratch_shapes=[pltpu.VMEM((bb, 8 + tt, c), F32), pltpu.VMEM((bb, tt, c), F32),
                        pltpu.VMEM((bb, tt, c), F32), pltpu.VMEM((bb, 8, c), F32)],
        compiler_params=_cparams(("parallel", "arbitrary")),
        name="rglru",
    )(z3, z3, buf, h_all, p["lru_conv_w"], p["lru_conv_b"], p["lru_wa"], p["lru_ba"], p["lru_wx"], p["lru_bx"],
      p["lru_lambda"])


def _gmlp_kernel(zu_ref, zv_ref, lng_ref, lnb_ref, ws_ref, bias_ref, y_ref, v_ref):
    u = jax.nn.gelu(zu_ref[...])
    gv = jax.nn.gelu(zv_ref[...])
    vc = gv - jnp.mean(gv, axis=-1, keepdims=True)
    v = vc * lax.rsqrt(jnp.mean(vc * vc, axis=-1, keepdims=True) + 1e-5) * lng_ref[...] + lnb_ref[...]
    v_ref[...] = v
    gd = MIX_W // GM_GROUPS
    vb = v.astype(BF16)
    for ck in range(v.shape[0] // GM_CHUNK):
        rs = slice(ck * GM_CHUNK, (ck + 1) * GM_CHUNK)
        for g in range(GM_GROUPS):
            sl = slice(g * gd, (g + 1) * gd)
            s = jnp.dot(ws_ref[g], vb[rs, sl], preferred_element_type=F32) + bias_ref[:, sl]
            y_ref[rs, sl] = (u[rs, sl] * s).astype(BF16)


def _gmlp(z2, p, l, ws_mix, bias_tile):
    m = z2.shape[0]
    c = MIX_W
    rows = GM_TILE_ROWS
    vec = lambda: pl.BlockSpec((None, 1, c), lambda i: (l, 0, 0))
    return pl.pallas_call(
        _gmlp_kernel,
        grid=(m // rows,),
        in_specs=[pl.BlockSpec((rows, c), lambda i: (i, ZB_GU)),
                  pl.BlockSpec((rows, c), lambda i: (i, ZB_GV)),
                  vec(), vec(),
                  pl.BlockSpec((None, GM_GROUPS, GM_CHUNK, GM_CHUNK), lambda i: (l, 0, 0, 0)),
                  pl.BlockSpec((None, GM_CHUNK, c), lambda i: (l, 0, 0))],
        out_specs=[pl.BlockSpec((rows, c), lambda i: (i, 0)),
                   pl.BlockSpec((rows, c), lambda i: (i, 0))],
        out_shape=[jax.ShapeDtypeStruct((m, c), BF16), jax.ShapeDtypeStruct((m, c), F32)],
        compiler_params=_cparams(("parallel",)),
        name="gmlp",
    )(z2, z2, p["gmlp_ln_g"], p["gmlp_ln_b"], ws_mix, bias_tile)


RW_ROWS = 64
RW_PACK = 4


def _dot_ones(x, ones, pieces, ones_on_left=False):
    acc = None
    rem = x
    for i in range(pieces):
        part = rem.astype(BF16)
        if i + 1 < pieces:
            rem = rem - part.astype(F32)
        term = (jnp.dot(ones, part, preferred_element_type=F32) if ones_on_left
                else jnp.dot(part, ones, preferred_element_type=F32))
        acc = term if acc is None else acc + term
    return acc


def _head_sums(x, ones):
    half = MIX_W // 2
    rows = x.shape[0]
    stacked = jnp.concatenate([x[:, :half], x[:, half:]], axis=0)
    s = _dot_ones(stacked, ones, 2)
    return jnp.concatenate([s[:rows], s[rows:]], axis=1)


def _rwkv_kernel(zr_ref, zk_ref, zv_ref, zt_ref, shr_ref, sht_ref, s0_ref,
                 mur_ref, mut_ref, w0_ref, a0_ref, kkp_ref, kap_ref, rk_ref, lng_ref, lnb_ref,
                 w2_ref, a2_ref, g2_ref, hsum_ref,
                 y_ref, sout_ref, prev_ref, s_ref, *, nb, L, carry):
    ci = pl.program_id(1)
    c = MIX_W
    rows = nb * L
    bb = RW_ROWS // L
    ngrp = rows // RW_ROWS
    log_l = L.bit_length() - 1

    @pl.when(ci == 0)
    def _():
        prev_ref[:, :, 0:RW_RKV] = shr_ref[...]
        prev_ref[:, :, RW_RKV:RW_RKV + RW_TAIL_PAD] = sht_ref[...]

    first = (lax.broadcasted_iota(jnp.int32, (rows, 1), 0) & (L - 1)) == 0

    def shift(z_ref, lo, hi, mu):
        w = hi - lo
        z3 = z_ref[...]
        z = z3.reshape(rows, w)
        prev = jnp.broadcast_to(prev_ref[:, :, lo:hi], (nb, L, w)).reshape(rows, w)
        zp = jnp.where(first, prev, pltpu.roll(z, 1, 0))
        prev_ref[:, :, lo:hi] = z3[:, L - 1:L, :]
        return z + (zp - z) * mu

    r = shift(zr_ref, 0, c, mur_ref[:, 0:c])
    k = shift(zk_ref, c, 2 * c, mur_ref[:, c:2 * c])
    v = shift(zv_ref, 2 * c, 3 * c, mur_ref[:, 2 * c:3 * c])
    tl = shift(zt_ref, RW_RKV, RW_RKV + RW_TAIL_PAD, mut_ref[...])

    wlin = w0_ref[...] + _dot(jnp.tanh(tl), w2_ref[...])
    logw = -jnp.exp(-_softplus(-wlin) - 0.5)
    a = jax.nn.sigmoid(a0_ref[...] + _dot(tl, a2_ref[...]))
    gate = _dot(jax.nn.sigmoid(tl), g2_ref[...])
    kkf = k * kkp_ref[...]
    kmod = k * (1.0 + (a - 1.0) * kap_ref[...])
    hsum = hsum_ref[...]
    kkn = kkf / jnp.maximum(jnp.sqrt(_head_sums(kkf * kkf, hsum)), 1e-12)
    beta = kkn * a
    bonus = _head_sums(r * kmod * rk_ref[...], hsum) * v

    ri_f = lax.broadcasted_iota(jnp.int32, (rows, rows), 0)
    cj_f = lax.broadcasted_iota(jnp.int32, (rows, rows), 1)
    cum_op = (((ri_f >> log_l) == (cj_f >> log_l)) & (cj_f <= ri_f)).astype(BF16)
    cum = _dot_ones(logw, cum_op, 3, ones_on_left=True)
    c_last = cum.reshape(nb, L, c)[:, L - 1:L, :]
    e_last = jnp.exp(c_last)
    e_rest = jnp.exp(jnp.broadcast_to(c_last, (nb, L, c)).reshape(rows, c) - cum)
    e_mc = jnp.exp(-cum)
    a_t = -kkn * jnp.exp(cum - logw)
    r_t = r * jnp.exp(cum)
    b_t = beta * e_mc
    k_t = kmod * e_mc
    b_hat = beta * e_rest
    k_hat = kmod * e_rest

    n = RW_ROWS
    hp = RW_PACK
    w4 = hp * RW_HD
    ngq = RW_H // hp
    ti = lax.broadcasted_iota(jnp.int32, (n, w4), 0)
    jl = lax.broadcasted_iota(jnp.int32, (n, w4), 1) & (RW_HD - 1)
    same = (ti >> log_l) == (jl >> log_l)
    strict = same & (jl < ti)
    incl = same & (jl <= ti)
    eye = (jl == ti).astype(F32)
    lvl = [((ti >> (s + 1)) == (jl >> (s + 1))) & ((ti >> s) != (jl >> s)) & (jl < ti) for s in range(log_l)]
    log_hd = RW_HD.bit_length() - 1
    bmask = ((lax.broadcasted_iota(jnp.int32, (w4, w4), 0) >> log_hd)
             == (lax.broadcasted_iota(jnp.int32, (w4, w4), 1) >> log_hd))

    def bd(xb):
        return jnp.where(bmask, jnp.concatenate([xb] * hp, axis=0), jnp.zeros((), BF16))

    def pdot(a4, b4):
        return jnp.dot(a4.astype(BF16), bd(b4.astype(BF16)), preferred_element_type=F32)

    def to_bd(blocks):
        rows_ = []
        for i, blk in enumerate(blocks):
            parts = [blk if j == i else jnp.zeros((RW_HD, RW_HD), F32) for j in range(hp)]
            rows_.append(jnp.concatenate(parts, axis=1))
        return jnp.concatenate(rows_, axis=0)

    probs = [(g, q) for g in range(ngrp) for q in range(ngq)]

    def cut(x, g, q):
        return x[g * n:(g + 1) * n, q * w4:(q + 1) * w4]

    if carry:
        @pl.when(ci == 0)
        def _():
            for sq in range(nb):
                for q in range(ngq):
                    s_ref[sq, q] = to_bd([s0_ref[sq, q * hp + i] for i in range(hp)])
        s0 = {(g, q, j): s_ref[g * bb + j, q] for (g, q) in probs for j in range(bb)}
    else:
        s0 = {(g, q, j): to_bd([s0_ref[g * bb + j, q * hp + i] for i in range(hp)])
              for (g, q) in probs for j in range(bb)}

    ar, m_ba, m_ka, m_br, m_kr = {}, {}, {}, {}, {}
    for p in probs:
        ar[p] = jnp.concatenate([cut(a_t, *p), cut(r_t, *p)], axis=0).astype(BF16)
        mb = _nt(ar[p], bd(cut(b_t, *p).astype(BF16)))
        mk = _nt(ar[p], bd(cut(k_t, *p).astype(BF16)))
        m_ba[p] = jnp.where(strict, mb[0:n], 0.0)
        m_br[p] = jnp.where(incl, mb[n:2 * n], 0.0)
        m_ka[p] = jnp.where(strict, mk[0:n], 0.0)
        m_kr[p] = jnp.where(incl, mk[n:2 * n], 0.0)

    inv = {p: eye + jnp.where(lvl[0], m_ba[p], 0.0) for p in probs}
    for s in range(1, log_l):
        half = {p: pdot(inv[p], jnp.where(lvl[s], m_ba[p], 0.0)) for p in probs}
        inv = {p: inv[p] + pdot(half[p], inv[p]) for p in probs}

    xa, xr = {}, {}
    for p in probs:
        pa, pr = [], []
        for j in range(bb):
            arj = ar[p] if bb == 1 else jnp.concatenate(
                [ar[p][j * L:(j + 1) * L], ar[p][n + j * L:n + (j + 1) * L]], axis=0)
            as0 = _nt(arj, s0[p + (j,)].astype(BF16))
            pa.append(as0[0:L])
            pr.append(as0[L:2 * L])
        xa[p] = pa[0] if bb == 1 else jnp.concatenate(pa, axis=0)
        xr[p] = pr[0] if bb == 1 else jnp.concatenate(pr, axis=0)

    vv = {p: cut(v, *p) for p in probs}
    vbd = {p: bd(vv[p].astype(BF16)) for p in probs}
    u = {p: pdot(inv[p], xa[p] + jnp.dot(m_ka[p].astype(BF16), vbd[p], preferred_element_type=F32)) for p in probs}
    o = {p: xr[p] + pdot(m_br[p], u[p]) + jnp.dot(m_kr[p].astype(BF16), vbd[p], preferred_element_type=F32)
         for p in probs}

    s_new = {}
    for (g, q) in probs:
        bh, kh = cut(b_hat, g, q), cut(k_hat, g, q)
        for j in range(bb):
            js = slice(j * L, (j + 1) * L)
            uv = jnp.concatenate([u[(g, q)][js], vv[(g, q)][js]], axis=0)
            bk = jnp.concatenate([bh[js], kh[js]], axis=0)
            seq = g * bb + j
            decay = e_last[seq][:, q * w4:(q + 1) * w4]
            s_new[(g, q, j)] = jnp.where(bmask, s0[(g, q, j)] * decay + _dot_tn(uv, bk), 0.0)

    o_rows = [jnp.concatenate([o[(g, q)] for q in range(ngq)], axis=1) for g in range(ngrp)]
    o_all = o_rows[0] if ngrp == 1 else jnp.concatenate(o_rows, axis=0)
    oc = o_all - _head_sums(o_all, hsum) * (1.0 / RW_HD)
    on = oc * lax.rsqrt(_head_sums(oc * oc, hsum) * (1.0 / RW_HD) + RW_LN_EPS)
    y = (on * lng_ref[...] + lnb_ref[...] + bonus) * gate
    y_ref[...] = y.reshape(nb, L, c).astype(BF16)

    def diag_block(m, i):
        return m[i * RW_HD:(i + 1) * RW_HD, i * RW_HD:(i + 1) * RW_HD]

    if carry:
        for (g, q, j), val in s_new.items():
            s_ref[g * bb + j, q] = val

        @pl.when(ci == pl.num_programs(1) - 1)
        def _():
            for sq in range(nb):
                for q in range(ngq):
                    for i in range(hp):
                        sout_ref[sq, q * hp + i] = diag_block(s_ref[sq, q], i)
    else:
        for (g, q, j), val in s_new.items():
            for i in range(hp):
                sout_ref[g * bb + j, q * hp + i] = diag_block(val, i)


def _rwkv(z3, sh_rkv, sh_tail, s_all, p, l, nb, L):
    bsz, tlen, _ = z3.shape
    c = MIX_W
    nc = tlen // L
    carry = nc > 1
    half = c // 2
    hsum = jnp.asarray(np.kron(np.eye(half // RW_HD, dtype=np.float32), np.ones((RW_HD, RW_HD), np.float32)), BF16)
    vec = lambda w=c: pl.BlockSpec((None, 1, w), lambda b, i: (l, 0, 0))
    lora = lambda: pl.BlockSpec((None, RW_TAIL_PAD, c), lambda b, i: (l, 0, 0))
    zspec = lambda blk: pl.BlockSpec((nb, L, c), lambda b, i, blk=blk: (b, i, blk))
    sspec = lambda: pl.BlockSpec((None, nb, RW_H, RW_HD, RW_HD), lambda b, i: (l, b, 0, 0, 0))
    kern = functools.partial(_rwkv_kernel, nb=nb, L=L, carry=carry)
    return pl.pallas_call(
        kern,
        grid=(bsz // nb, nc),
        in_specs=[zspec(ZB_R), zspec(ZB_K), zspec(ZB_V),
                  pl.BlockSpec((nb, L, RW_TAIL_PAD), lambda b, i: (b, i, Z_TAIL // RW_TAIL_PAD)),
                  pl.BlockSpec((None, nb, 1, RW_RKV), lambda b, i: (l, b, 0, 0)),
                  pl.BlockSpec((None, nb, 1, RW_TAIL_PAD), lambda b, i: (l, b, 0, 0)),
                  sspec(),
                  vec(RW_RKV), vec(RW_TAIL_PAD), vec(), vec(), vec(), vec(), vec(), vec(), vec(),
                  lora(), lora(), lora(),
                  pl.BlockSpec((half, half), lambda b, i: (0, 0))],
        out_specs=[pl.BlockSpec((nb, L, c), lambda b, i: (b, i, 0)), sspec()],
        out_shape=[jax.ShapeDtypeStruct((bsz, tlen, c), BF16),
                   jax.ShapeDtypeStruct(s_all.shape, F32)],
        input_output_aliases={6: 1},
        scratch_shapes=[pltpu.VMEM((nb, 1, RW_RKV + RW_TAIL_PAD), F32),
                        pltpu.VMEM((nb, RW_H // RW_PACK, RW_PACK * RW_HD, RW_PACK * RW_HD) if carry
                                   else (1, 1, 8, 128), F32)],
        compiler_params=_cparams(("parallel", "arbitrary")),
        name="rwkv7",
    )(z3, z3, z3, z3, sh_rkv, sh_tail, s_all,
      p["rwkv_mu_rkv"], p["rwkv_mu_tail"], p["rwkv_w0"], p["rwkv_a0"], p["rwkv_kk"], p["rwkv_ka"], p["rwkv_rk"],
      p["rwkv_ln_g"], p["rwkv_ln_b"], p["rwkv_w2"], p["rwkv_a2"], p["rwkv_g2"], hsum)


ML_ROWS = 128


def _mlstm_kernel(zx_ref, zv_ref, zo_ref, zif_ref, zift_ref, buf_ref, c0_ref, n0_ref, m0_ref,
                  cw_ref, cb_ref, wq_ref, wk_ref, brow_ref, bcol_ref, ng_ref, skip_ref,
                  y_ref, cout_ref, nout_ref, mout_ref, xs_ref, c_ref, n_ref, m_ref, *, nb, L, carry):
    ci = pl.program_id(1)
    c = MIX_W
    rows = nb * L
    log_l = L.bit_length() - 1

    @pl.when(ci == 0)
    def _():
        xs_ref[:, 5:8, :] = buf_ref[...]
        if carry:
            c_ref[...] = c0_ref[...]
            n_ref[...] = n0_ref[...]
            m_ref[...] = m0_ref[...]

    if carry:
        @pl.when(ci > 0)
        def _():
            xs_ref[:, 5:8, :] = xs_ref[:, L + 5:L + 8, :]

    xs_ref[:, 8:8 + L, :] = zx_ref[...]
    conv = cb_ref[...] + cw_ref[0:1, :] * xs_ref[:, 5:5 + L, :]
    for j in range(1, CONV_W):
        conv = conv + cw_ref[j:j + 1, :] * xs_ref[:, 5 + j:5 + j + L, :]
    cc = jax.nn.silu(conv).reshape(rows, c)
    zv = zv_ref[...].reshape(rows, c)
    zo = zo_ref[...].reshape(rows, c)

    ri = lax.broadcasted_iota(jnp.int32, (rows, rows), 0)
    cj = lax.broadcasted_iota(jnp.int32, (rows, rows), 1)
    same = (ri >> log_l) == (cj >> log_l)
    causal = same & (cj <= ri)

    gate_col = zif_ref[...].reshape(rows, 128) + brow_ref[...]
    bcum_col = _dot_ones(_log_sigmoid(gate_col), causal.astype(BF16), 3, ones_on_left=True)
    gate_row = zift_ref[...] + bcol_ref[...]
    bcum_row = _dot_ones(_log_sigmoid(gate_row), (same & (ri <= cj)).astype(BF16), 3)

    def per_row(x3):
        return jnp.broadcast_to(x3, (nb, L, x3.shape[-1])).reshape(rows, x3.shape[-1])

    def last(x):
        return x.reshape(nb, L, x.shape[-1])[:, L - 1:L, :]

    c_in, n_in, m_in = (c_ref, n_ref, m_ref) if carry else (c0_ref, n0_ref, m0_ref)
    c_out, n_out, m_out = (c_ref, n_ref, m_ref) if carry else (cout_ref, nout_ref, mout_ref)
    heads = range(ML_H)
    sls = [slice(h * ML_HD, (h + 1) * ML_HD) for h in heads]
    cmat = {(h, j): c_in[j, h] for h in heads for j in range(nb)}
    nvec = [n_in[:, h:h + 1, :] for h in heads]
    m_prev = [m_in[:, h:h + 1, 0:1] for h in heads]

    ch = [cc[:, sls[h]] for h in heads]
    q = [_dot(ch[h], wq_ref[h]) for h in heads]
    k = [_dot(ch[h], wk_ref[h]) * (ML_HD ** -0.5) for h in heads]
    v = [zv[:, sls[h]] for h in heads]
    qk = [_dot_nt(q[h], k[h]) for h in heads]
    qc = []
    for h in heads:
        parts = [_dot_nt(q[h][j * L:(j + 1) * L], cmat[(h, j)]) for j in range(nb)]
        qc.append(parts[0] if nb == 1 else jnp.concatenate(parts, axis=0))

    b_col = [bcum_col[:, ML_H + h:ML_H + h + 1] for h in heads]
    li_col = [gate_col[:, h:h + 1] for h in heads]
    m_t, s, sc = [], [], []
    for h in heads:
        log_d = jnp.where(causal, b_col[h] - bcum_row[ML_H + h:ML_H + h + 1, :] + gate_row[h:h + 1, :], -jnp.inf)
        inter = b_col[h] + per_row(m_prev[h])
        mt = jnp.maximum(jnp.max(log_d, axis=-1, keepdims=True), inter)
        m_t.append(mt)
        s.append(qk[h] * jnp.exp(log_d - mt))
        sc.append(jnp.exp(inter - mt))

    for h in heads:
        num = _dot(s[h], v[h]) + sc[h] * qc[h]
        den = (jnp.sum(s[h], axis=-1, keepdims=True)
               + sc[h] * jnp.sum(q[h] * per_row(nvec[h]), axis=-1, keepdims=True))
        hh = num / jnp.maximum(jnp.abs(den), jnp.exp(-m_t[h]))
        hc = hh - jnp.mean(hh, axis=-1, keepdims=True)
        hn = hc * lax.rsqrt(jnp.mean(hc * hc, axis=-1, keepdims=True) + 1e-6) * ng_ref[:, sls[h]]
        y = jax.nn.sigmoid(zo[:, sls[h]]) * (hn + skip_ref[:, sls[h]] * ch[h])
        y_ref[:, :, sls[h]] = y.reshape(nb, L, ML_HD).astype(BF16)

    for h in heads:
        m_new = last(m_t[h])
        b_last = last(b_col[h])
        wj = jnp.exp(per_row(b_last - m_new) - b_col[h] + li_col[h])
        dec = jnp.exp(b_last + m_prev[h] - m_new)
        wv = wj * v[h]
        for j in range(nb):
            js = slice(j * L, (j + 1) * L)
            c_out[j, h] = dec[j] * cmat[(h, j)] + _dot_tn(wv[js], k[h][js])
        n_out[:, h:h + 1, :] = dec * nvec[h] + jnp.sum((wj * k[h]).reshape(nb, L, ML_HD), axis=1, keepdims=True)
        m_out[:, h:h + 1, :] = jnp.broadcast_to(m_new, (nb, 1, ML_HD))

    if carry:
        @pl.when(ci == pl.num_programs(1) - 1)
        def _():
            cout_ref[...] = c_ref[...]
            nout_ref[...] = n_ref[...]
            mout_ref[...] = m_ref[...]


def _mlstm(z3, zift, buf, c_all, n_all, m_all, p, l, nb, L):
    bsz, tlen, _ = z3.shape
    c = MIX_W
    nc = tlen // L
    rows = nb * L
    carry = nc > 1
    bcol = jnp.broadcast_to(p["mlstm_bif"][l][:, None], (8, rows))
    vec = lambda: pl.BlockSpec((None, 1, c), lambda b, i: (l, 0, 0))
    zspec = lambda blk: pl.BlockSpec((nb, L, c), lambda b, i, blk=blk: (b, i, blk))
    hmat = lambda: pl.BlockSpec((None, ML_H, ML_HD, ML_HD), lambda b, i: (l, 0, 0, 0))
    cspec = lambda: pl.BlockSpec((None, nb, ML_H, ML_HD, ML_HD), lambda b, i: (l, b, 0, 0, 0))
    nspec = lambda: pl.BlockSpec((None, nb, ML_H, ML_HD), lambda b, i: (l, b, 0, 0))
    kern = functools.partial(_mlstm_kernel, nb=nb, L=L, carry=carry)
    tiny = (1, 8, 128)
    return pl.pallas_call(
        kern,
        grid=(bsz // nb, nc),
        in_specs=[zspec(ZB_MX), zspec(ZB_MV), zspec(ZB_MO),
                  pl.BlockSpec((nb, L, 128), lambda b, i: (b, i, Z_MIF // 128)),
                  pl.BlockSpec((None, 8, rows), lambda b, i: (b * nc + i, 0, 0)),
                  pl.BlockSpec((None, nb, CONV_W - 1, c), lambda b, i: (l, b, 0, 0)),
                  cspec(), nspec(), nspec(),
                  pl.BlockSpec((None, CONV_W, c), lambda b, i: (l, 0, 0)),
                  vec(), hmat(), hmat(),
                  pl.BlockSpec((None, 1, 128), lambda b, i: (l, 0, 0)),
                  pl.BlockSpec((8, rows), lambda b, i: (0, 0)),
                  vec(), vec()],
        out_specs=[pl.BlockSpec((nb, L, c), lambda b, i: (b, i, 0)), cspec(), nspec(), nspec()],
        out_shape=[jax.ShapeDtypeStruct((bsz, tlen, c), BF16),
                   jax.ShapeDtypeStruct(c_all.shape, F32),
                   jax.ShapeDtypeStruct(n_all.shape, F32),
                   jax.ShapeDtypeStruct(m_all.shape, F32)],
        input_output_aliases={6: 1, 7: 2, 8: 3},
        scratch_shapes=[pltpu.VMEM((nb, 8 + L, c), F32),
                        pltpu.VMEM((nb, ML_H, ML_HD, ML_HD) if carry else tiny, F32),
                        pltpu.VMEM((nb, ML_H, ML_HD) if carry else tiny, F32),
                        pltpu.VMEM((nb, ML_H, ML_HD) if carry else tiny, F32)],
        compiler_params=_cparams(("parallel", "arbitrary")),
        name="mlstm",
    )(z3, z3, z3, z3, zift, buf, c_all, n_all, m_all,
      p["mlstm_conv_w"], p["mlstm_conv_b"], p["mlstm_wq"], p["mlstm_wk"], p["mlstm_brow"], bcol,
      p["mlstm_norm_g"], p["mlstm_skip"])


def _block_diag(w):
    dp, nb, d, _ = w.shape
    eye = jnp.eye(nb, dtype=w.dtype)
    return jnp.einsum("lnij,nm->lnimj", w, eye).reshape(dp, nb * d, nb * d)


SRC_TAIL = 3584
SRC_ML = SRC_TAIL + RW_TAIL
SRC_MIF = 5280
PACK_ROWS = 512


def _pack_mix_kernel(a_ref, mif_ref, o_ref):
    j = pl.program_id(1)
    last = pl.num_programs(1) - 1

    @pl.when(j < last)
    def _():
        o_ref[...] = a_ref[0].astype(BF16)

    @pl.when(j == last)
    def _():
        pad = lambda n: jnp.zeros((n, D_MODEL), F32)
        tile = jnp.concatenate([a_ref[0, 0:RW_TAIL, :], pad(RW_TAIL_PAD - RW_TAIL),
                                mif_ref[0], pad(PACK_ROWS - RW_TAIL_PAD - 8)], axis=0)
        o_ref[...] = tile.astype(BF16)


def _pack_mix(w_in_t):
    n_plain = SRC_TAIL // PACK_ROWS
    n_tiles = Z_COLS // PACK_ROWS

    def src_row(l, j):
        shifted = SRC_ML + (j - n_plain) * PACK_ROWS
        row = jnp.where(j < n_plain, j * PACK_ROWS, jnp.where(j < n_tiles - 1, shifted, SRC_TAIL))
        return (l, pl.multiple_of(row, 8), 0)

    return pl.pallas_call(
        _pack_mix_kernel,
        grid=(DEPTH, n_tiles),
        in_specs=[pl.BlockSpec((pl.Element(1), pl.Element(PACK_ROWS), pl.Element(D_MODEL)), src_row),
                  pl.BlockSpec((pl.Element(1), pl.Element(8), pl.Element(D_MODEL)), lambda l, j: (l, SRC_MIF, 0))],
        out_specs=pl.BlockSpec((None, PACK_ROWS, D_MODEL), lambda l, j: (l, j, 0)),
        out_shape=jax.ShapeDtypeStruct((DEPTH, Z_COLS, D_MODEL), BF16),
        compiler_params=_cparams(("parallel", "arbitrary")),
        name="pack_mix",
    )(w_in_t, w_in_t)


def _prepare(raw):
    p = {}
    wt = jnp.swapaxes(raw["w_in"], 1, 2)
    p["w_mix"] = _pack_mix(wt)
    p["w_in_t"] = wt
    for name in ("w_out", "w_ff_out"):
        p[name] = raw[name].astype(BF16)
    p["w_ff_in"] = raw["w_ff_in"]
    p["w_branch"] = raw["w_branch"]
    row = lambda a: a.reshape(DEPTH, 1, -1)
    for name in ("norm_pre_mix", "norm_post_mix", "norm_pre_ffn", "norm_post_ffn",
                 "lru_conv_b", "lru_ba", "lru_bx", "lru_lambda", "gmlp_ln_g", "gmlp_ln_b",
                 "rwkv_w0", "rwkv_a0", "rwkv_kk", "rwkv_ka", "rwkv_rk", "rwkv_ln_g", "rwkv_ln_b",
                 "mlstm_conv_b", "mlstm_norm_g", "mlstm_skip"):
        p[name] = row(raw[name])
    p["lru_conv_w"] = raw["lru_conv_w"]
    p["mlstm_conv_w"] = raw["mlstm_conv_w"]
    p["lru_wa"] = _block_diag(raw["lru_wa"]).astype(BF16)
    p["lru_wx"] = _block_diag(raw["lru_wx"]).astype(BF16)
    mu = raw["rwkv_mu"]
    p["rwkv_mu_rkv"] = row(mu[:, :RW_RKV])
    p["rwkv_mu_tail"] = row(jnp.pad(mu[:, RW_RKV:], ((0, 0), (0, RW_TAIL_PAD - RW_TAIL))))

    def lora(w, lo):
        return jnp.pad(w, ((0, 0), (lo, RW_TAIL_PAD - lo - w.shape[1]), (0, 0))).astype(BF16)

    p["rwkv_w2"] = lora(raw["rwkv_w2"], 0)
    p["rwkv_a2"] = lora(raw["rwkv_a2"], 32)
    p["rwkv_g2"] = lora(raw["rwkv_g2"], 64)
    p["mlstm_wq"] = raw["mlstm_wq"].astype(BF16)
    p["mlstm_wk"] = raw["mlstm_wk"].astype(BF16)
    bif = jnp.concatenate([raw["mlstm_bi"], raw["mlstm_bf"]], axis=-1)
    p["mlstm_bif"] = bif
    p["mlstm_brow"] = row(jnp.pad(bif, ((0, 0), (0, 128 - 8))))
    p["gmlp_ws"] = raw["gmlp_ws"]
    p["gmlp_bs"] = raw["gmlp_bs"]
    return p


def _gmlp_mix_weights(p, tlen):
    L = min(GM_CHUNK, tlen)
    rep = GM_CHUNK // L
    ws = jnp.tril(p["gmlp_ws"][:, :, :L, :L])
    eye = jnp.eye(rep, dtype=ws.dtype)
    ws_mix = jnp.einsum("lgps,ab->lgapbs", ws, eye).reshape(DEPTH, GM_GROUPS, GM_CHUNK, GM_CHUNK).astype(BF16)
    bias = jnp.swapaxes(p["gmlp_bs"][:, :, :L], 1, 2)
    bias = jnp.repeat(bias, MIX_W // GM_GROUPS, axis=2)
    bias = jnp.tile(bias, (1, rep, 1))
    return ws_mix, bias


def _group_forward(x3, states, p, is_start, depth=DEPTH):
    bsz, tlen, _ = x3.shape
    m = bsz * tlen
    x = x3.reshape(m, D_MODEL)
    lru_buf, lru_h, rw_shift, rw_s, ml_buf, ml_c, ml_n, ml_m = states
    tm = min(TM_DENSE, m)
    if is_start:
        lru_bb, lru_tt = min(LRU_SEQS_P, bsz), min(LRU_TILE_P, tlen)
        rw_nb, rw_l, ml_nb, ml_l = min(RW_SEQS_P, bsz), RW_CHUNK, 1, ML_CHUNK_P
    else:
        lru_bb, lru_tt, rw_nb, rw_l, ml_nb, ml_l = 16, tlen, RW_ROWS // tlen, tlen, ML_ROWS // tlen, tlen
    ws_mix, gm_bias = _gmlp_mix_weights(p, tlen)
    nl = lru_h.shape[0]
    h_all = lru_h.reshape(nl, bsz, 1, MIX_W)
    sh_rkv = rw_shift[:, :, :RW_RKV].reshape(nl, bsz, 1, RW_RKV)
    sh_tail = jnp.pad(rw_shift[:, :, RW_RKV:], ((0, 0), (0, 0), (0, RW_TAIL_PAD - RW_TAIL)))
    sh_tail = sh_tail.reshape(nl, bsz, 1, RW_TAIL_PAD)
    s_all, c_all, n_all = rw_s, ml_c, ml_n
    m_all = jnp.broadcast_to(ml_m[:, :, :, None], (nl, bsz, ML_H, ML_HD))
    new_states = [[] for _ in range(8)]
    gm_vs = []
    hn = _prenorm(x, p["norm_pre_mix"], 0, tm)
    for l in range(depth):
        z = _inproj(hn, p["w_mix"], l, tm)
        z3 = z.reshape(bsz, tlen, Z_COLS)

        y_a, h_all = _lru(z3, lru_buf, h_all, p, l, is_start, lru_bb, lru_tt)
        y_b, gm_v = _gmlp(z, p, l, ws_mix, gm_bias)
        y_c, s_all = _rwkv(z3, sh_rkv, sh_tail, s_all, p, l, rw_nb, rw_l)
        y_c = y_c.reshape(m, MIX_W)
        ml_rows = ml_nb * ml_l
        zift = jnp.swapaxes(z[:, Z_MIF:Z_MIF + 8].reshape(m // ml_rows, ml_rows, 8), 1, 2)
        y_d, c_all, n_all, m_all = _mlstm(z3, zift, ml_buf, c_all, n_all, m_all, p, l, ml_nb, ml_l)
        y_d = y_d.reshape(m, MIX_W)

        merged = _merge(hn, (y_a.reshape(m, MIX_W), y_b, y_c, y_d), p["w_in_t"], p["w_branch"], l, tm)
        x1, hf = _outproj(merged, p["w_out"], x, p["norm_post_mix"], p["norm_pre_ffn"], l, min(TM_OUTPROJ, m))
        hmid = _ffup(hf, p["w_ff_in"], l, tm)
        x, hn = _ffdown(hmid, p["w_ff_out"], x1, p["norm_post_ffn"], p["norm_pre_mix"], l, min(TM_FFDOWN, m))

        new_states[0].append(z3[:, tlen - (CONV_W - 1):, 0:MIX_W])
        new_states[2].append(jnp.concatenate(
            [z3[:, tlen - 1, ZB_R * MIX_W:ZB_R * MIX_W + RW_RKV], z3[:, tlen - 1, Z_TAIL:Z_TAIL + RW_TAIL]], axis=-1))
        new_states[4].append(z3[:, tlen - (CONV_W - 1):, ZB_MX * MIX_W:(ZB_MX + 1) * MIX_W])
        gm_vs.append(gm_v.reshape(bsz, tlen, MIX_W))
    out_states = [jnp.stack(new_states[0], axis=0), h_all.reshape(nl, bsz, MIX_W)[:depth],
                  jnp.stack(new_states[2], axis=0), s_all[:depth],
                  jnp.stack(new_states[4], axis=0), c_all[:depth], n_all[:depth], m_all[:depth, :, :, 0]]
    return x.reshape(bsz, tlen, D_MODEL), out_states, jnp.stack(gm_vs, axis=0)


def kernel(x_prompt, x_sample, state_lru_conv, state_lru_h, state_rwkv_shift, state_rwkv_wkv, state_mlstm_conv, state_mlstm_C, state_mlstm_n, state_mlstm_m, norm_pre_mix, norm_post_mix, norm_pre_ffn, norm_post_ffn, w_in, lru_conv_w, lru_conv_b, lru_wa, lru_ba, lru_wx, lru_bx, lru_lambda, gmlp_ln_g, gmlp_ln_b, gmlp_ws, gmlp_bs, rwkv_mu, rwkv_w0, rwkv_w2, rwkv_a0, rwkv_a2, rwkv_g2, rwkv_kk, rwkv_ka, rwkv_rk, rwkv_ln_g, rwkv_ln_b, mlstm_conv_w, mlstm_conv_b, mlstm_wq, mlstm_wk, mlstm_bi, mlstm_bf, mlstm_norm_g, mlstm_skip, w_branch, w_out, w_ff_in, w_ff_out):
    raw = dict(norm_pre_mix=norm_pre_mix, norm_post_mix=norm_post_mix, norm_pre_ffn=norm_pre_ffn,
               norm_post_ffn=norm_post_ffn, w_in=w_in, lru_conv_w=lru_conv_w, lru_conv_b=lru_conv_b,
               lru_wa=lru_wa, lru_ba=lru_ba, lru_wx=lru_wx, lru_bx=lru_bx, lru_lambda=lru_lambda,
               gmlp_ln_g=gmlp_ln_g, gmlp_ln_b=gmlp_ln_b, gmlp_ws=gmlp_ws, gmlp_bs=gmlp_bs,
               rwkv_mu=rwkv_mu, rwkv_w0=rwkv_w0, rwkv_w2=rwkv_w2, rwkv_a0=rwkv_a0, rwkv_a2=rwkv_a2,
               rwkv_g2=rwkv_g2, rwkv_kk=rwkv_kk, rwkv_ka=rwkv_ka, rwkv_rk=rwkv_rk, rwkv_ln_g=rwkv_ln_g,
               rwkv_ln_b=rwkv_ln_b, mlstm_conv_w=mlstm_conv_w, mlstm_conv_b=mlstm_conv_b, mlstm_wq=mlstm_wq,
               mlstm_wk=mlstm_wk, mlstm_bi=mlstm_bi, mlstm_bf=mlstm_bf, mlstm_norm_g=mlstm_norm_g,
               mlstm_skip=mlstm_skip, w_branch=w_branch, w_out=w_out, w_ff_in=w_ff_in, w_ff_out=w_ff_out)
    p = _prepare(raw)
    bp = x_prompt.shape[0]
    zero = lambda *s: jnp.zeros((DEPTH, bp) + s, F32)
    prompt_states = (zero(CONV_W - 1, MIX_W), zero(MIX_W), zero(RW_RKV + RW_TAIL), zero(RW_H, RW_HD, RW_HD),
                     zero(CONV_W - 1, MIX_W), zero(ML_H, ML_HD, ML_HD), zero(ML_H, ML_HD), zero(ML_H))
    sample_states = (state_lru_conv, state_lru_h, state_rwkv_shift, state_rwkv_wkv,
                     state_mlstm_conv, state_mlstm_C, state_mlstm_n, state_mlstm_m)
    yp, st_p, _ = _group_forward(x_prompt, prompt_states, p, True)
    ys, st_s, gm_v = _group_forward(x_sample, sample_states, p, False)
    return (yp, ys, *st_p, *st_s, gm_v)
```

```python
import functools

import numpy as np
import jax
import jax.numpy as jnp
from jax import lax
from jax.experimental import pallas as pl
from jax.experimental.pallas import tpu as pltpu

F32 = jnp.float32
BF16 = jnp.bfloat16

D_MODEL = 2048
DEPTH = 4
MIX_W = 512
CONV_W = 4
LRU_BLOCKS = 8
LRU_C = 8.0
GM_CHUNK = 128
GM_GROUPS = 4
RW_HD = 64
RW_H = 8
RW_RKV = 3 * MIX_W
RW_TAIL = 160
RW_TAIL_PAD = 256
RW_LN_EPS = 64e-5
ML_H = 4
ML_HD = 128
D_FF = 5632
N_GATE = 4 * D_MODEL
P_SRC = 5288
Z_COLS = 5632

ZB_LX, ZB_LG, ZB_GU, ZB_GV, ZB_R, ZB_K, ZB_V, ZB_MX, ZB_MV, ZB_MO = range(10)
Z_TAIL = 5120
Z_MIF = 5376

RW_CHUNK = 64
RW_SEQS_P = 4
ML_CHUNK_P = 256
GM_TILE_ROWS = 512
TM_DENSE = 1024
TN_INPROJ = Z_COLS // 2
TM_OUTPROJ = 512
OUTPROJ_SUB_ROWS = 256
FFUP_SUB_ROWS = 256
FFDOWN_SUB_ROWS = 256
TM_FFDOWN = 512
TN_MERGE = 256
LRU_SEQS_P = 4
LRU_TILE_P = 256
VMEM_LIMIT = 60 * 1024 * 1024


def _cparams(sem):
    return pltpu.CompilerParams(dimension_semantics=sem, vmem_limit_bytes=VMEM_LIMIT)


def _softplus(x):
    return jnp.maximum(x, 0.0) + jnp.log1p(jnp.exp(-jnp.abs(x)))


def _log_sigmoid(x):
    return -_softplus(-x)


def _rms(x, g):
    return x * lax.rsqrt(jnp.mean(x * x, axis=-1, keepdims=True) + 1e-6) * g


def _dot(a, b):
    return jnp.dot(a.astype(BF16), b.astype(BF16), preferred_element_type=F32)


def _nt(a, b):
    return lax.dot_general(a, b, (((1,), (1,)), ((), ())), preferred_element_type=F32)


def _dot_nt(a, b):
    return lax.dot_general(a.astype(BF16), b.astype(BF16), (((1,), (1,)), ((), ())),
                           preferred_element_type=F32)


def _dot_tn(a, b):
    return lax.dot_general(a.astype(BF16), b.astype(BF16), (((0,), (0,)), ((), ())),
                           preferred_element_type=F32)


def _prenorm_kernel(x_ref, g_ref, hn_ref):
    hn_ref[...] = _rms(x_ref[...], g_ref[...]).astype(BF16)


def _prenorm(x, g, l, tm):
    m = x.shape[0]
    row_spec = pl.BlockSpec((tm, D_MODEL), lambda i: (i, 0))
    return pl.pallas_call(
        _prenorm_kernel,
        grid=(m // tm,),
        in_specs=[row_spec, pl.BlockSpec((None, 1, D_MODEL), lambda i: (l, 0, 0))],
        out_specs=row_spec,
        out_shape=jax.ShapeDtypeStruct((m, D_MODEL), BF16),
        compiler_params=_cparams(("parallel",)),
        name="prenorm",
    )(x, g)


def _inproj_kernel(hn_ref, w_ref, z_ref):
    tn = z_ref.shape[1]
    rows = pl.multiple_of(pl.program_id(1) * tn, tn)
    z_ref[...] = _nt(hn_ref[...], w_ref[pl.ds(rows, tn), :])


def _inproj(hn, w, l, tm, tn=TN_INPROJ):
    m = hn.shape[0]
    return pl.pallas_call(
        _inproj_kernel,
        grid=(m // tm, Z_COLS // tn),
        in_specs=[pl.BlockSpec((tm, D_MODEL), lambda i, j: (i, 0)),
                  pl.BlockSpec((None, Z_COLS, D_MODEL), lambda i, j: (l, 0, 0), pipeline_mode=pl.Buffered(1))],
        out_specs=pl.BlockSpec((tm, tn), lambda i, j: (i, j)),
        out_shape=jax.ShapeDtypeStruct((m, Z_COLS), F32),
        compiler_params=_cparams(("parallel", "arbitrary")),
        name="inproj",
    )(hn, w)


def _merge_kernel(hn_ref, ya_ref, yb_ref, yc_ref, yd_ref, g0_ref, g1_ref, g2_ref, g3_ref, wb_ref, o_ref,
                  gb_ref, wbb_ref):
    @pl.when(pl.program_id(1) == 0)
    def _():
        for b, g_ref in enumerate((g0_ref, g1_ref, g2_ref, g3_ref)):
            gb_ref[b] = g_ref[0].astype(BF16)
        wbb_ref[...] = wb_ref[...].astype(BF16)

    hn = hn_ref[...]
    acc = None
    for b, y_ref in enumerate((ya_ref, yb_ref, yc_ref, yd_ref)):
        zg = _nt(hn, gb_ref[b])
        br = jnp.dot(y_ref[...], wbb_ref[b], preferred_element_type=F32)
        term = jax.nn.sigmoid(zg) * br
        acc = term if acc is None else acc + term
    o_ref[...] = acc.astype(BF16)


def _merge(hn, ys, w_in_t, wbranch, l, tm, tn=TN_MERGE):
    m = hn.shape[0]
    nb = D_MODEL // tn
    y_spec = pl.BlockSpec((tm, MIX_W), lambda j, i: (i, 0))
    g_specs = [pl.BlockSpec((pl.Element(1), pl.Element(tn), pl.Element(D_MODEL)),
                            lambda j, i, b=b: (l, pl.multiple_of(P_SRC + b * D_MODEL + j * tn, 8), 0))
               for b in range(4)]
    return pl.pallas_call(
        _merge_kernel,
        grid=(nb, m // tm),
        in_specs=[pl.BlockSpec((tm, D_MODEL), lambda j, i: (i, 0)), y_spec, y_spec, y_spec, y_spec,
                  *g_specs,
                  pl.BlockSpec((None, 4, MIX_W, tn), lambda j, i: (l, 0, 0, j))],
        out_specs=pl.BlockSpec((tm, tn), lambda j, i: (i, j)),
        out_shape=jax.ShapeDtypeStruct((m, D_MODEL), BF16),
        scratch_shapes=[pltpu.VMEM((4, tn, D_MODEL), BF16), pltpu.VMEM((4, MIX_W, tn), BF16)],
        compiler_params=_cparams(("parallel", "arbitrary")),
        name="merge",
    )(hn, *ys, w_in_t, w_in_t, w_in_t, w_in_t, wbranch)


def _outproj_kernel(mg_ref, w_ref, x_ref, gpost_ref, gpre_ref, x1_ref, hf_ref):
    sub = OUTPROJ_SUB_ROWS
    for r0 in range(0, mg_ref.shape[0], sub):
        rs = slice(r0, r0 + sub)
        mix = jnp.dot(mg_ref[rs, :], w_ref[...], preferred_element_type=F32)
        x1 = x_ref[rs, :] + _rms(mix, gpost_ref[...])
        x1_ref[rs, :] = x1
        hf_ref[rs, :] = _rms(x1, gpre_ref[...]).astype(BF16)


def _outproj(merged, w_out, x, g_post, g_pre_ffn, l, tm):
    m = x.shape[0]
    g_spec = pl.BlockSpec((None, 1, D_MODEL), lambda i: (l, 0, 0))
    row_spec = pl.BlockSpec((tm, D_MODEL), lambda i: (i, 0))
    return pl.pallas_call(
        _outproj_kernel,
        grid=(m // tm,),
        in_specs=[row_spec, pl.BlockSpec((None, D_MODEL, D_MODEL), lambda i: (l, 0, 0)), row_spec, g_spec, g_spec],
        out_specs=[row_spec, row_spec],
        out_shape=[jax.ShapeDtypeStruct((m, D_MODEL), F32), jax.ShapeDtypeStruct((m, D_MODEL), BF16)],
        compiler_params=_cparams(("parallel",)),
        name="outproj",
    )(merged, w_out, x, g_post, g_pre_ffn)


def _ffup_kernel(hf_ref, wg_ref, wu_ref, o_ref, wgb_ref, wub_ref):
    @pl.when(pl.program_id(1) == 0)
    def _():
        wgb_ref[...] = wg_ref[...].astype(BF16)
        wub_ref[...] = wu_ref[...].astype(BF16)

    sub = min(FFUP_SUB_ROWS, hf_ref.shape[0])
    for r0 in range(0, hf_ref.shape[0], sub):
        rs = slice(r0, r0 + sub)
        hf = hf_ref[rs, :]
        g = jnp.dot(hf, wgb_ref[...], preferred_element_type=F32)
        u = jnp.dot(hf, wub_ref[...], preferred_element_type=F32)
        o_ref[rs, :] = (jax.nn.silu(g) * u).astype(BF16)


def _ffup(hf, w_ff_in, l, tm, tn=512):
    m = hf.shape[0]
    nb = D_FF // tn
    return pl.pallas_call(
        _ffup_kernel,
        grid=(nb, m // tm),
        in_specs=[pl.BlockSpec((tm, D_MODEL), lambda j, i: (i, 0)),
                  pl.BlockSpec((None, D_MODEL, tn), lambda j, i: (l, 0, j)),
                  pl.BlockSpec((None, D_MODEL, tn), lambda j, i: (l, 0, nb + j))],
        out_specs=pl.BlockSpec((tm, tn), lambda j, i: (i, j)),
        out_shape=jax.ShapeDtypeStruct((m, D_FF), BF16),
        scratch_shapes=[pltpu.VMEM((D_MODEL, tn), BF16), pltpu.VMEM((D_MODEL, tn), BF16)],
        compiler_params=_cparams(("parallel", "arbitrary")),
        name="ffup",
    )(hf, w_ff_in, w_ff_in)


def _ffdown_kernel(h_ref, w_ref, x1_ref, g_ref, gnext_ref, o_ref, hn_ref):
    sub = FFDOWN_SUB_ROWS
    for r0 in range(0, h_ref.shape[0], sub):
        rs = slice(r0, r0 + sub)
        ff = jnp.dot(h_ref[rs, :], w_ref[...], preferred_element_type=F32)
        x2 = x1_ref[rs, :] + _rms(ff, g_ref[...])
        o_ref[rs, :] = x2
        hn_ref[rs, :] = _rms(x2, gnext_ref[...]).astype(BF16)


def _ffdown(h, w_ff_out, x1, g_post, g_pre_mix, l, tm):
    m = x1.shape[0]
    l_next = min(l + 1, DEPTH - 1)
    row_spec = pl.BlockSpec((tm, D_MODEL), lambda i: (i, 0))
    return pl.pallas_call(
        _ffdown_kernel,
        grid=(m // tm,),
        in_specs=[pl.BlockSpec((tm, D_FF), lambda i: (i, 0)),
                  pl.BlockSpec((None, D_FF, D_MODEL), lambda i: (l, 0, 0), pipeline_mode=pl.Buffered(1)),
                  row_spec,
                  pl.BlockSpec((None, 1, D_MODEL), lambda i: (l, 0, 0)),
                  pl.BlockSpec((None, 1, D_MODEL), lambda i: (l_next, 0, 0))],
        out_specs=[row_spec, row_spec],
        out_shape=[jax.ShapeDtypeStruct((m, D_MODEL), F32), jax.ShapeDtypeStruct((m, D_MODEL), BF16)],
        compiler_params=_cparams(("parallel",)),
        name="ffdown",
    )(h, w_ff_out, x1, g_post, g_pre_mix)


def _lru_kernel(zx_ref, zg_ref, buf_ref, h0_ref, cw_ref, cb_ref, wa_ref, ba_ref, wx_ref, bx_ref, lam_ref,
                y_ref, hout_ref, xs_ref, a_ref, b_ref, h_ref, *, is_start, bb, tt):
    t = pl.program_id(1)
    c = MIX_W

    @pl.when(t == 0)
    def _():
        xs_ref[:, 5:8, :] = buf_ref[...]
        h_ref[...] = jnp.broadcast_to(h0_ref[...], (bb, 8, c))

    @pl.when(t > 0)
    def _():
        xs_ref[:, 5:8, :] = xs_ref[:, tt + 5:tt + 8, :]

    xs_ref[:, 8:8 + tt, :] = zx_ref[...]
    xc = cb_ref[...] + cw_ref[0:1, :] * xs_ref[:, 5:5 + tt, :]
    for j in range(1, CONV_W):
        xc = xc + cw_ref[j:j + 1, :] * xs_ref[:, 5 + j:5 + j + tt, :]
    xc2 = xc.reshape(bb * tt, c)
    r = jax.nn.sigmoid(_dot(xc2, wa_ref[...]) + ba_ref[...])
    i = jax.nn.sigmoid(_dot(xc2, wx_ref[...]) + bx_ref[...])
    log_a = LRU_C * r * _log_sigmoid(lam_ref[...])
    a = jnp.exp(log_a)
    mult = jnp.sqrt(1.0 - jnp.exp(2.0 * log_a))
    if is_start:
        tpos = lax.broadcasted_iota(jnp.int32, (bb, tt, c), 1).reshape(bb * tt, c) + t * tt
        mult = jnp.where(tpos == 0, 1.0, mult)
    a_ref[...] = a.reshape(bb, tt, c)
    b_ref[...] = (mult * i * xc2).reshape(bb, tt, c)

    row = lax.broadcasted_iota(jnp.int32, (bb, 8, c), 1).reshape(bb * 8, c)

    def group(gi, carry):
        off = pl.multiple_of(gi * 8, 8)
        av = a_ref[:, pl.ds(off, 8), :].reshape(bb * 8, c)
        bv = b_ref[:, pl.ds(off, 8), :].reshape(bb * 8, c)
        for s in (1, 2, 4):
            keep = row >= s
            a_sh = pltpu.roll(av, s, 0)
            b_sh = pltpu.roll(bv, s, 0)
            bv = jnp.where(keep, av * b_sh + bv, bv)
            av = jnp.where(keep, av * a_sh, av)
        hh = (av * h_ref[...].reshape(bb * 8, c) + bv).reshape(bb, 8, c)
        b_ref[:, pl.ds(off, 8), :] = hh
        h_ref[...] = jnp.broadcast_to(hh[:, 7:8, :], (bb, 8, c))
        return carry

    lax.fori_loop(0, tt // 8, group, 0)
    y_ref[...] = (b_ref[...] * jax.nn.gelu(zg_ref[...])).astype(BF16)

    @pl.when(t == pl.num_programs(1) - 1)
    def _():
        hout_ref[...] = h_ref[:, 7:8, :]


def _lru(z3, buf, h_all, p, l, is_start, bb, tt):
    bsz, tlen, _ = z3.shape
    c = MIX_W
    vec = lambda: pl.BlockSpec((None, 1, c), lambda b, t: (l, 0, 0))
    mat = lambda: pl.BlockSpec((None, c, c), lambda b, t: (l, 0, 0))
    hspec = lambda: pl.BlockSpec((None, bb, 1, c), lambda b, t: (l, b, 0, 0))
    kern = functools.partial(_lru_kernel, is_start=is_start, bb=bb, tt=tt)
    return pl.pallas_call(
        kern,
        grid=(bsz // bb, tlen // tt),
        in_specs=[pl.BlockSpec((bb, tt, c), lambda b, t: (b, t, ZB_LX)),
                  pl.BlockSpec((bb, tt, c), lambda b, t: (b, t, ZB_LG)),
                  pl.BlockSpec((None, bb, CONV_W - 1, c), lambda b, t: (l, b, 0, 0)),
                  hspec(),
                  pl.BlockSpec((None, CONV_W, c), lambda b, t: (l, 0, 0)),
                  vec(), mat(), vec(), mat(), vec(), vec()],
        out_specs=[pl.BlockSpec((bb, tt, c), lambda b, t: (b, t, 0)), hspec()],
        out_shape=[jax.ShapeDtypeStruct((bsz, tlen, c), BF16),
                   jax.ShapeDtypeStruct(h_all.shape, F32)],
        input_output_aliases={3: 1},
        scratch_shapes=[pltpu.VMEM((bb, 8 + tt, c), F32), pltpu.VMEM((bb, tt, c), F32),
                        pltpu.VMEM((bb, tt, c), F32), pltpu.VMEM((bb, 8, c), F32)],
        compiler_params=_cparams(("parallel", "arbitrary")),
        name="rglru",
    )(z3, z3, buf, h_all, p["lru_conv_w"], p["lru_conv_b"], p["lru_wa"], p["lru_ba"], p["lru_wx"], p["lru_bx"],
      p["lru_lambda"])


def _gmlp_kernel(zu_ref, zv_ref, lng_ref, lnb_ref, ws_ref, bias_ref, y_ref, v_ref):
    u = jax.nn.gelu(zu_ref[...])
    gv = jax.nn.gelu(zv_ref[...])
    vc = gv - jnp.mean(gv, axis=-1, keepdims=True)
    v = vc * lax.rsqrt(jnp.mean(vc * vc, axis=-1, keepdims=True) + 1e-5) * lng_ref[...] + lnb_ref[...]
    v_ref[...] = v
    gd = MIX_W // GM_GROUPS
    vb = v.astype(BF16)
    for ck in range(v.shape[0] // GM_CHUNK):
        rs = slice(ck * GM_CHUNK, (ck + 1) * GM_CHUNK)
        for g in range(GM_GROUPS):
            sl = slice(g * gd, (g + 1) * gd)
            s = jnp.dot(ws_ref[g], vb[rs, sl], preferred_element_type=F32) + bias_ref[:, sl]
            y_ref[rs, sl] = (u[rs, sl] * s).astype(BF16)


def _gmlp(z2, p, l, ws_mix, bias_tile):
    m = z2.shape[0]
    c = MIX_W
    rows = GM_TILE_ROWS
    vec = lambda: pl.BlockSpec((None, 1, c), lambda i: (l, 0, 0))
    return pl.pallas_call(
        _gmlp_kernel,
        grid=(m // rows,),
        in_specs=[pl.BlockSpec((rows, c), lambda i: (i, ZB_GU)),
                  pl.BlockSpec((rows, c), lambda i: (i, ZB_GV)),
                  vec(), vec(),
                  pl.BlockSpec((None, GM_GROUPS, GM_CHUNK, GM_CHUNK), lambda i: (l, 0, 0, 0)),
                  pl.BlockSpec((None, GM_CHUNK, c), lambda i: (l, 0, 0))],
        out_specs=[pl.BlockSpec((rows, c), lambda i: (i, 0)),
                   pl.BlockSpec((rows, c), lambda i: (i, 0))],
        out_shape=[jax.ShapeDtypeStruct((m, c), BF16), jax.ShapeDtypeStruct((m, c), F32)],
        compiler_params=_cparams(("parallel",)),
        name="gmlp",
    )(z2, z2, p["gmlp_ln_g"], p["gmlp_ln_b"], ws_mix, bias_tile)


RW_ROWS = 64
RW_PACK = 4


def _dot_ones(x, ones, pieces, ones_on_left=False):
    acc = None
    rem = x
    for i in range(pieces):
        part = rem.astype(BF16)
        if i + 1 < pieces:
            rem = rem - part.astype(F32)
        term = (jnp.dot(ones, part, preferred_element_type=F32) if ones_on_left
                else jnp.dot(part, ones, preferred_element_type=F32))
        acc = term if acc is None else acc + term
    return acc


def _head_sums(x, ones):
    half = MIX_W // 2
    rows = x.shape[0]
    stacked = jnp.concatenate([x[:, :half], x[:, half:]], axis=0)
    s = _dot_ones(stacked, ones, 2)
    return jnp.concatenate([s[:rows], s[rows:]], axis=1)


def _rwkv_kernel(zr_ref, zk_ref, zv_ref, zt_ref, shr_ref, sht_ref, s0_ref,
                 mur_ref, mut_ref, w0_ref, a0_ref, kkp_ref, kap_ref, rk_ref, lng_ref, lnb_ref,
                 w2_ref, a2_ref, g2_ref, hsum_ref,
                 y_ref, sout_ref, prev_ref, s_ref, *, nb, L, carry):
    ci = pl.program_id(1)
    c = MIX_W
    rows = nb * L
    bb = RW_ROWS // L
    ngrp = rows // RW_ROWS
    log_l = L.bit_length() - 1

    @pl.when(ci == 0)
    def _():
        prev_ref[:, :, 0:RW_RKV] = shr_ref[...]
        prev_ref[:, :, RW_RKV:RW_RKV + RW_TAIL_PAD] = sht_ref[...]

    first = (lax.broadcasted_iota(jnp.int32, (rows, 1), 0) & (L - 1)) == 0

    def shift(z_ref, lo, hi, mu):
        w = hi - lo
        z3 = z_ref[...]
        z = z3.reshape(rows, w)
        prev = jnp.broadcast_to(prev_ref[:, :, lo:hi], (nb, L, w)).reshape(rows, w)
        zp = jnp.where(first, prev, pltpu.roll(z, 1, 0))
        prev_ref[:, :, lo:hi] = z3[:, L - 1:L, :]
        return z + (zp - z) * mu

    r = shift(zr_ref, 0, c, mur_ref[:, 0:c])
    k = shift(zk_ref, c, 2 * c, mur_ref[:, c:2 * c])
    v = shift(zv_ref, 2 * c, 3 * c, mur_ref[:, 2 * c:3 * c])
    tl = shift(zt_ref, RW_RKV, RW_RKV + RW_TAIL_PAD, mut_ref[...])

    wlin = w0_ref[...] + _dot(jnp.tanh(tl), w2_ref[...])
    logw = -jnp.exp(-_softplus(-wlin) - 0.5)
    a = jax.nn.sigmoid(a0_ref[...] + _dot(tl, a2_ref[...]))
    gate = _dot(jax.nn.sigmoid(tl), g2_ref[...])
    kkf = k * kkp_ref[...]
    kmod = k * (1.0 + (a - 1.0) * kap_ref[...])
    hsum = hsum_ref[...]
    kkn = kkf / jnp.maximum(jnp.sqrt(_head_sums(kkf * kkf, hsum)), 1e-12)
    beta = kkn * a
    bonus = _head_sums(r * kmod * rk_ref[...], hsum) * v

    ri_f = lax.broadcasted_iota(jnp.int32, (rows, rows), 0)
    cj_f = lax.broadcasted_iota(jnp.int32, (rows, rows), 1)
    cum_op = (((ri_f >> log_l) == (cj_f >> log_l)) & (cj_f <= ri_f)).astype(BF16)
    cum = _dot_ones(logw, cum_op, 3, ones_on_left=True)
    c_last = cum.reshape(nb, L, c)[:, L - 1:L, :]
    e_last = jnp.exp(c_last)
    e_rest = jnp.exp(jnp.broadcast_to(c_last, (nb, L, c)).reshape(rows, c) - cum)
    e_mc = jnp.exp(-cum)
    a_t = -kkn * jnp.exp(cum - logw)
    r_t = r * jnp.exp(cum)
    b_t = beta * e_mc
    k_t = kmod * e_mc
    b_hat = beta * e_rest
    k_hat = kmod * e_rest

    n = RW_ROWS
    hp = RW_PACK
    w4 = hp * RW_HD
    ngq = RW_H // hp
    ti = lax.broadcasted_iota(jnp.int32, (n, w4), 0)
    jl = lax.broadcasted_iota(jnp.int32, (n, w4), 1) & (RW_HD - 1)
    same = (ti >> log_l) == (jl >> log_l)
    strict = same & (jl < ti)
    incl = same & (jl <= ti)
    eye = (jl == ti).astype(F32)
    lvl = [((ti >> (s + 1)) == (jl >> (s + 1))) & ((ti >> s) != (jl >> s)) & (jl < ti) for s in range(log_l)]
    log_hd = RW_HD.bit_length() - 1
    bmask = ((lax.broadcasted_iota(jnp.int32, (w4, w4), 0) >> log_hd)
             == (lax.broadcasted_iota(jnp.int32, (w4, w4), 1) >> log_hd))

    def bd(xb):
        return jnp.where(bmask, jnp.concatenate([xb] * hp, axis=0), jnp.zeros((), BF16))

    def pdot(a4, b4):
        return jnp.dot(a4.astype(BF16), bd(b4.astype(BF16)), preferred_element_type=F32)

    def to_bd(blocks):
        rows_ = []
        for i, blk in enumerate(blocks):
            parts = [blk if j == i else jnp.zeros((RW_HD, RW_HD), F32) for j in range(hp)]
            rows_.append(jnp.concatenate(parts, axis=1))
        return jnp.concatenate(rows_, axis=0)

    probs = [(g, q) for g in range(ngrp) for q in range(ngq)]

    def cut(x, g, q):
        return x[g * n:(g + 1) * n, q * w4:(q + 1) * w4]

    if carry:
        @pl.when(ci == 0)
        def _():
            for sq in range(nb):
                for q in range(ngq):
                    s_ref[sq, q] = to_bd([s0_ref[sq, q * hp + i] for i in range(hp)])
        s0 = {(g, q, j): s_ref[g * bb + j, q] for (g, q) in probs for j in range(bb)}
    else:
        s0 = {(g, q, j): to_bd([s0_ref[g * bb + j, q * hp + i] for i in range(hp)])
              for (g, q) in probs for j in range(bb)}

    ar, m_ba, m_ka, m_br, m_kr = {}, {}, {}, {}, {}
    for p in probs:
        ar[p] = jnp.concatenate([cut(a_t, *p), cut(r_t, *p)], axis=0).astype(BF16)
        mb = _nt(ar[p], bd(cut(b_t, *p).astype(BF16)))
        mk = _nt(ar[p], bd(cut(k_t, *p).astype(BF16)))
        m_ba[p] = jnp.where(strict, mb[0:n], 0.0)
        m_br[p] = jnp.where(incl, mb[n:2 * n], 0.0)
        m_ka[p] = jnp.where(strict, mk[0:n], 0.0)
        m_kr[p] = jnp.where(incl, mk[n:2 * n], 0.0)

    inv = {p: eye + jnp.where(lvl[0], m_ba[p], 0.0) for p in probs}
    for s in range(1, log_l):
        half = {p: pdot(inv[p], jnp.where(lvl[s], m_ba[p], 0.0)) for p in probs}
        inv = {p: inv[p] + pdot(half[p], inv[p]) for p in probs}

    xa, xr = {}, {}
    for p in probs:
        pa, pr = [], []
        for j in range(bb):
            arj = ar[p] if bb == 1 else jnp.concatenate(
                [ar[p][j * L:(j + 1) * L], ar[p][n + j * L:n + (j + 1) * L]], axis=0)
            as0 = _nt(arj, s0[p + (j,)].astype(BF16))
            pa.append(as0[0:L])
            pr.append(as0[L:2 * L])
        xa[p] = pa[0] if bb == 1 else jnp.concatenate(pa, axis=0)
        xr[p] = pr[0] if bb == 1 else jnp.concatenate(pr, axis=0)

    vv = {p: cut(v, *p) for p in probs}
    vbd = {p: bd(vv[p].astype(BF16)) for p in probs}
    u = {p: pdot(inv[p], xa[p] + jnp.dot(m_ka[p].astype(BF16), vbd[p], preferred_element_type=F32)) for p in probs}
    o = {p: xr[p] + pdot(m_br[p], u[p]) + jnp.dot(m_kr[p].astype(BF16), vbd[p], preferred_element_type=F32)
         for p in probs}

    s_new = {}
    for (g, q) in probs:
        bh, kh = cut(b_hat, g, q), cut(k_hat, g, q)
        for j in range(bb):
            js = slice(j * L, (j + 1) * L)
            uv = jnp.concatenate([u[(g, q)][js], vv[(g, q)][js]], axis=0)
            bk = jnp.concatenate([bh[js], kh[js]], axis=0)
            seq = g * bb + j
            decay = e_last[seq][:, q * w4:(q + 1) * w4]
            s_new[(g, q, j)] = jnp.where(bmask, s0[(g, q, j)] * decay + _dot_tn(uv, bk), 0.0)

    o_rows = [jnp.concatenate([o[(g, q)] for q in range(ngq)], axis=1) for g in range(ngrp)]
    o_all = o_rows[0] if ngrp == 1 else jnp.concatenate(o_rows, axis=0)
    oc = o_all - _head_sums(o_all, hsum) * (1.0 / RW_HD)
    on = oc * lax.rsqrt(_head_sums(oc * oc, hsum) * (1.0 / RW_HD) + RW_LN_EPS)
    y = (on * lng_ref[...] + lnb_ref[...] + bonus) * gate
    y_ref[...] = y.reshape(nb, L, c).astype(BF16)

    def diag_block(m, i):
        return m[i * RW_HD:(i + 1) * RW_HD, i * RW_HD:(i + 1) * RW_HD]

    if carry:
        for (g, q, j), val in s_new.items():
            s_ref[g * bb + j, q] = val

        @pl.when(ci == pl.num_programs(1) - 1)
        def _():
            for sq in range(nb):
                for q in range(ngq):
                    for i in range(hp):
                        sout_ref[sq, q * hp + i] = diag_block(s_ref[sq, q], i)
    else:
        for (g, q, j), val in s_new.items():
            for i in range(hp):
                sout_ref[g * bb + j, q * hp + i] = diag_block(val, i)


def _rwkv(z3, sh_rkv, sh_tail, s_all, p, l, nb, L):
    bsz, tlen, _ = z3.shape
    c = MIX_W
    nc = tlen // L
    carry = nc > 1
    half = c // 2
    hsum = jnp.asarray(np.kron(np.eye(half // RW_HD, dtype=np.float32), np.ones((RW_HD, RW_HD), np.float32)), BF16)
    vec = lambda w=c: pl.BlockSpec((None, 1, w), lambda b, i: (l, 0, 0))
    lora = lambda: pl.BlockSpec((None, RW_TAIL_PAD, c), lambda b, i: (l, 0, 0))
    zspec = lambda blk: pl.BlockSpec((nb, L, c), lambda b, i, blk=blk: (b, i, blk))
    sspec = lambda: pl.BlockSpec((None, nb, RW_H, RW_HD, RW_HD), lambda b, i: (l, b, 0, 0, 0))
    kern = functools.partial(_rwkv_kernel, nb=nb, L=L, carry=carry)
    return pl.pallas_call(
        kern,
        grid=(bsz // nb, nc),
        in_specs=[zspec(ZB_R), zspec(ZB_K), zspec(ZB_V),
                  pl.BlockSpec((nb, L, RW_TAIL_PAD), lambda b, i: (b, i, Z_TAIL // RW_TAIL_PAD)),
                  pl.BlockSpec((None, nb, 1, RW_RKV), lambda b, i: (l, b, 0, 0)),
                  pl.BlockSpec((None, nb, 1, RW_TAIL_PAD), lambda b, i: (l, b, 0, 0)),
                  sspec(),
                  vec(RW_RKV), vec(RW_TAIL_PAD), vec(), vec(), vec(), vec(), vec(), vec(), vec(),
                  lora(), lora(), lora(),
                  pl.BlockSpec((half, half), lambda b, i: (0, 0))],
        out_specs=[pl.BlockSpec((nb, L, c), lambda b, i: (b, i, 0)), sspec()],
        out_shape=[jax.ShapeDtypeStruct((bsz, tlen, c), BF16),
                   jax.ShapeDtypeStruct(s_all.shape, F32)],
        input_output_aliases={6: 1},
        scratch_shapes=[pltpu.VMEM((nb, 1, RW_RKV + RW_TAIL_PAD), F32),
                        pltpu.VMEM((nb, RW_H // RW_PACK, RW_PACK * RW_HD, RW_PACK * RW_HD) if carry
                                   else (1, 1, 8, 128), F32)],
        compiler_params=_cparams(("parallel", "arbitrary")),
        name="rwkv7",
    )(z3, z3, z3, z3, sh_rkv, sh_tail, s_all,
      p["rwkv_mu_rkv"], p["rwkv_mu_tail"], p["rwkv_w0"], p["rwkv_a0"], p["rwkv_kk"], p["rwkv_ka"], p["rwkv_rk"],
      p["rwkv_ln_g"], p["rwkv_ln_b"], p["rwkv_w2"], p["rwkv_a2"], p["rwkv_g2"], hsum)


ML_ROWS = 128


def _mlstm_kernel(zx_ref, zv_ref, zo_ref, zif_ref, zift_ref, buf_ref, c0_ref, n0_ref, m0_ref,
                  cw_ref, cb_ref, wq_ref, wk_ref, brow_ref, bcol_ref, ng_ref, skip_ref,
                  y_ref, cout_ref, nout_ref, mout_ref, xs_ref, c_ref, n_ref, m_ref, *, nb, L, carry):
    ci = pl.program_id(1)
    c = MIX_W
    rows = nb * L
    log_l = L.bit_length() - 1

    @pl.when(ci == 0)
    def _():
        xs_ref[:, 5:8, :] = buf_ref[...]
        if carry:
            c_ref[...] = c0_ref[...]
            n_ref[...] = n0_ref[...]
            m_ref[...] = m0_ref[...]

    if carry:
        @pl.when(ci > 0)
        def _():
            xs_ref[:, 5:8, :] = xs_ref[:, L + 5:L + 8, :]

    xs_ref[:, 8:8 + L, :] = zx_ref[...]
    conv = cb_ref[...] + cw_ref[0:1, :] * xs_ref[:, 5:5 + L, :]
    for j in range(1, CONV_W):
        conv = conv + cw_ref[j:j + 1, :] * xs_ref[:, 5 + j:5 + j + L, :]
    cc = jax.nn.silu(conv).reshape(rows, c)
    zv = zv_ref[...].reshape(rows, c)
    zo = zo_ref[...].reshape(rows, c)

    ri = lax.broadcasted_iota(jnp.int32, (rows, rows), 0)
    cj = lax.broadcasted_iota(jnp.int32, (rows, rows), 1)
    same = (ri >> log_l) == (cj >> log_l)
    causal = same & (cj <= ri)

    gate_col = zif_ref[...].reshape(rows, 128) + brow_ref[...]
    bcum_col = _dot_ones(_log_sigmoid(gate_col), causal.astype(BF16), 3, ones_on_left=True)
    gate_row = zift_ref[...] + bcol_ref[...]
    bcum_row = _dot_ones(_log_sigmoid(gate_row), (same & (ri <= cj)).astype(BF16), 3)

    def per_row(x3):
        return jnp.broadcast_to(x3, (nb, L, x3.shape[-1])).reshape(rows, x3.shape[-1])

    def last(x):
        return x.reshape(nb, L, x.shape[-1])[:, L - 1:L, :]

    c_in, n_in, m_in = (c_ref, n_ref, m_ref) if carry else (c0_ref, n0_ref, m0_ref)
    c_out, n_out, m_out = (c_ref, n_ref, m_ref) if carry else (cout_ref, nout_ref, mout_ref)
    heads = range(ML_H)
    sls = [slice(h * ML_HD, (h + 1) * ML_HD) for h in heads]
    cmat = {(h, j): c_in[j, h] for h in heads for j in range(nb)}
    nvec = [n_in[:, h:h + 1, :] for h in heads]
    m_prev = [m_in[:, h:h + 1, 0:1] for h in heads]

    ch = [cc[:, sls[h]] for h in heads]
    q = [_dot(ch[h], wq_ref[h]) for h in heads]
    k = [_dot(ch[h], wk_ref[h]) * (ML_HD ** -0.5) for h in heads]
    v = [zv[:, sls[h]] for h in heads]
    qk = [_dot_nt(q[h], k[h]) for h in heads]
    qc = []
    for h in heads:
        parts = [_dot_nt(q[h][j * L:(j + 1) * L], cmat[(h, j)]) for j in range(nb)]
        qc.append(parts[0] if nb == 1 else jnp.concatenate(parts, axis=0))

    b_col = [bcum_col[:, ML_H + h:ML_H + h + 1] for h in heads]
    li_col = [gate_col[:, h:h + 1] for h in heads]
    m_t, s, sc = [], [], []
    for h in heads:
        log_d = jnp.where(causal, b_col[h] - bcum_row[ML_H + h:ML_H + h + 1, :] + gate_row[h:h + 1, :], -jnp.inf)
        inter = b_col[h] + per_row(m_prev[h])
        mt = jnp.maximum(jnp.max(log_d, axis=-1, keepdims=True), inter)
        m_t.append(mt)
        s.append(qk[h] * jnp.exp(log_d - mt))
        sc.append(jnp.exp(inter - mt))

    for h in heads:
        num = _dot(s[h], v[h]) + sc[h] * qc[h]
        den = (jnp.sum(s[h], axis=-1, keepdims=True)
               + sc[h] * jnp.sum(q[h] * per_row(nvec[h]), axis=-1, keepdims=True))
        hh = num / jnp.maximum(jnp.abs(den), jnp.exp(-m_t[h]))
        hc = hh - jnp.mean(hh, axis=-1, keepdims=True)
        hn = hc * lax.rsqrt(jnp.mean(hc * hc, axis=-1, keepdims=True) + 1e-6) * ng_ref[:, sls[h]]
        y = jax.nn.sigmoid(zo[:, sls[h]]) * (hn + skip_ref[:, sls[h]] * ch[h])
        y_ref[:, :, sls[h]] = y.reshape(nb, L, ML_HD).astype(BF16)

    for h in heads:
        m_new = last(m_t[h])
        b_last = last(b_col[h])
        wj = jnp.exp(per_row(b_last - m_new) - b_col[h] + li_col[h])
        dec = jnp.exp(b_last + m_prev[h] - m_new)
        wv = wj * v[h]
        for j in range(nb):
            js = slice(j * L, (j + 1) * L)
            c_out[j, h] = dec[j] * cmat[(h, j)] + _dot_tn(wv[js], k[h][js])
        n_out[:, h:h + 1, :] = dec * nvec[h] + jnp.sum((wj * k[h]).reshape(nb, L, ML_HD), axis=1, keepdims=True)
        m_out[:, h:h + 1, :] = jnp.broadcast_to(m_new, (nb, 1, ML_HD))

    if carry:
        @pl.when(ci == pl.num_programs(1) - 1)
        def _():
            cout_ref[...] = c_ref[...]
            nout_ref[...] = n_ref[...]
            mout_ref[...] = m_ref[...]


def _mlstm(z3, zift, buf, c_all, n_all, m_all, p, l, nb, L):
    bsz, tlen, _ = z3.shape
    c = MIX_W
    nc = tlen // L
    rows = nb * L
    carry = nc > 1
    bcol = jnp.broadcast_to(p["mlstm_bif"][l][:, None], (8, rows))
    vec = lambda: pl.BlockSpec((None, 1, c), lambda b, i: (l, 0, 0))
    zspec = lambda blk: pl.BlockSpec((nb, L, c), lambda b, i, blk=blk: (b, i, blk))
    hmat = lambda: pl.BlockSpec((None, ML_H, ML_HD, ML_HD), lambda b, i: (l, 0, 0, 0))
    cspec = lambda: pl.BlockSpec((None, nb, ML_H, ML_HD, ML_HD), lambda b, i: (l, b, 0, 0, 0))
    nspec = lambda: pl.BlockSpec((None, nb, ML_H, ML_HD), lambda b, i: (l, b, 0, 0))
    kern = functools.partial(_mlstm_kernel, nb=nb, L=L, carry=carry)
    tiny = (1, 8, 128)
    return pl.pallas_call(
        kern,
        grid=(bsz // nb, nc),
        in_specs=[zspec(ZB_MX), zspec(ZB_MV), zspec(ZB_MO),
                  pl.BlockSpec((nb, L, 128), lambda b, i: (b, i, Z_MIF // 128)),
                  pl.BlockSpec((None, 8, rows), lambda b, i: (b * nc + i, 0, 0)),
                  pl.BlockSpec((None, nb, CONV_W - 1, c), lambda b, i: (l, b, 0, 0)),
                  cspec(), nspec(), nspec(),
                  pl.BlockSpec((None, CONV_W, c), lambda b, i: (l, 0, 0)),
                  vec(), hmat(), hmat(),
                  pl.BlockSpec((None, 1, 128), lambda b, i: (l, 0, 0)),
                  pl.BlockSpec((8, rows), lambda b, i: (0, 0)),
                  vec(), vec()],
        out_specs=[pl.BlockSpec((nb, L, c), lambda b, i: (b, i, 0)), cspec(), nspec(), nspec()],
        out_shape=[jax.ShapeDtypeStruct((bsz, tlen, c), BF16),
                   jax.ShapeDtypeStruct(c_all.shape, F32),
                   jax.ShapeDtypeStruct(n_all.shape, F32),
                   jax.ShapeDtypeStruct(m_all.shape, F32)],
        input_output_aliases={6: 1, 7: 2, 8: 3},
        scratch_shapes=[pltpu.VMEM((nb, 8 + L, c), F32),
                        pltpu.VMEM((nb, ML_H, ML_HD, ML_HD) if carry else tiny, F32),
                        pltpu.VMEM((nb, ML_H, ML_HD) if carry else tiny, F32),
                        pltpu.VMEM((nb, ML_H, ML_HD) if carry else tiny, F32)],
        compiler_params=_cparams(("parallel", "arbitrary")),
        name="mlstm",
    )(z3, z3, z3, z3, zift, buf, c_all, n_all, m_all,
      p["mlstm_conv_w"], p["mlstm_conv_b"], p["mlstm_wq"], p["mlstm_wk"], p["mlstm_brow"], bcol,
      p["mlstm_norm_g"], p["mlstm_skip"])


def _block_diag(w):
    dp, nb, d, _ = w.shape
    eye = jnp.eye(nb, dtype=w.dtype)
    return jnp.einsum("lnij,nm->lnimj", w, eye).reshape(dp, nb * d, nb * d)


SRC_TAIL = 3584
SRC_ML = SRC_TAIL + RW_TAIL
SRC_MIF = 5280
PACK_ROWS = 512


def _pack_mix_kernel(a_ref, mif_ref, o_ref):
    j = pl.program_id(1)
    last = pl.num_programs(1) - 1

    @pl.when(j < last)
    def _():
        o_ref[...] = a_ref[0].astype(BF16)

    @pl.when(j == last)
    def _():
        pad = lambda n: jnp.zeros((n, D_MODEL), F32)
        tile = jnp.concatenate([a_ref[0, 0:RW_TAIL, :], pad(RW_TAIL_PAD - RW_TAIL),
                                mif_ref[0], pad(PACK_ROWS - RW_TAIL_PAD - 8)], axis=0)
        o_ref[...] = tile.astype(BF16)


def _pack_mix(w_in_t):
    n_plain = SRC_TAIL // PACK_ROWS
    n_tiles = Z_COLS // PACK_ROWS

    def src_row(l, j):
        shifted = SRC_ML + (j - n_plain) * PACK_ROWS
        row = jnp.where(j < n_plain, j * PACK_ROWS, jnp.where(j < n_tiles - 1, shifted, SRC_TAIL))
        return (l, pl.multiple_of(row, 8), 0)

    return pl.pallas_call(
        _pack_mix_kernel,
        grid=(DEPTH, n_tiles),
        in_specs=[pl.BlockSpec((pl.Element(1), pl.Element(PACK_ROWS), pl.Element(D_MODEL)), src_row),
                  pl.BlockSpec((pl.Element(1), pl.Element(8), pl.Element(D_MODEL)), lambda l, j: (l, SRC_MIF, 0))],
        out_specs=pl.BlockSpec((None, PACK_ROWS, D_MODEL), lambda l, j: (l, j, 0)),
        out_shape=jax.ShapeDtypeStruct((DEPTH, Z_COLS, D_MODEL), BF16),
        compiler_params=_cparams(("parallel", "arbitrary")),
        name="pack_mix",
    )(w_in_t, w_in_t)


def _prepare(raw):
    p = {}
    wt = jnp.swapaxes(raw["w_in"], 1, 2)
    p["w_mix"] = _pack_mix(wt)
    p["w_in_t"] = wt
    for name in ("w_out", "w_ff_out"):
        p[name] = raw[name].astype(BF16)
    p["w_ff_in"] = raw["w_ff_in"]
    p["w_branch"] = raw["w_branch"]
    row = lambda a: a.reshape(DEPTH, 1, -1)
    for name in ("norm_pre_mix", "norm_post_mix", "norm_pre_ffn", "norm_post_ffn",
                 "lru_conv_b", "lru_ba", "lru_bx", "lru_lambda", "gmlp_ln_g", "gmlp_ln_b",
                 "rwkv_w0", "rwkv_a0", "rwkv_kk", "rwkv_ka", "rwkv_rk", "rwkv_ln_g", "rwkv_ln_b",
                 "mlstm_conv_b", "mlstm_norm_g", "mlstm_skip"):
        p[name] = row(raw[name])
    p["lru_conv_w"] = raw["lru_conv_w"]
    p["mlstm_conv_w"] = raw["mlstm_conv_w"]
    p["lru_wa"] = _block_diag(raw["lru_wa"]).astype(BF16)
    p["lru_wx"] = _block_diag(raw["lru_wx"]).astype(BF16)
    mu = raw["rwkv_mu"]
    p["rwkv_mu_rkv"] = row(mu[:, :RW_RKV])
    p["rwkv_mu_tail"] = row(jnp.pad(mu[:, RW_RKV:], ((0, 0), (0, RW_TAIL_PAD - RW_TAIL))))

    def lora(w, lo):
        return jnp.pad(w, ((0, 0), (lo, RW_TAIL_PAD - lo - w.shape[1]), (0, 0))).astype(BF16)

    p["rwkv_w2"] = lora(raw["rwkv_w2"], 0)
    p["rwkv_a2"] = lora(raw["rwkv_a2"], 32)
    p["rwkv_g2"] = lora(raw["rwkv_g2"], 64)
    p["mlstm_wq"] = raw["mlstm_wq"].astype(BF16)
    p["mlstm_wk"] = raw["mlstm_wk"].astype(BF16)
    bif = jnp.concatenate([raw["mlstm_bi"], raw["mlstm_bf"]], axis=-1)
    p["mlstm_bif"] = bif
    p["mlstm_brow"] = row(jnp.pad(bif, ((0, 0), (0, 128 - 8))))
    p["gmlp_ws"] = raw["gmlp_ws"]
    p["gmlp_bs"] = raw["gmlp_bs"]
    return p


def _gmlp_mix_weights(p, tlen):
    L = min(GM_CHUNK, tlen)
    rep = GM_CHUNK // L
    ws = jnp.tril(p["gmlp_ws"][:, :, :L, :L])
    eye = jnp.eye(rep, dtype=ws.dtype)
    ws_mix = jnp.einsum("lgps,ab->lgapbs", ws, eye).reshape(DEPTH, GM_GROUPS, GM_CHUNK, GM_CHUNK).astype(BF16)
    bias = jnp.swapaxes(p["gmlp_bs"][:, :, :L], 1, 2)
    bias = jnp.repeat(bias, MIX_W // GM_GROUPS, axis=2)
    bias = jnp.tile(bias, (1, rep, 1))
    return ws_mix, bias


def _group_forward(x3, states, p, is_start, depth=DEPTH):
    bsz, tlen, _ = x3.shape
    m = bsz * tlen
    x = x3.reshape(m, D_MODEL)
    lru_buf, lru_h, rw_shift, rw_s, ml_buf, ml_c, ml_n, ml_m = states
    tm = min(TM_DENSE, m)
    if is_start:
        lru_bb, lru_tt = min(LRU_SEQS_P, bsz), min(LRU_TILE_P, tlen)
        rw_nb, rw_l, ml_nb, ml_l = min(RW_SEQS_P, bsz), RW_CHUNK, 1, ML_CHUNK_P
    else:
        lru_bb, lru_tt, rw_nb, rw_l, ml_nb, ml_l = 16, tlen, RW_ROWS // tlen, tlen, ML_ROWS // tlen, tlen
    ws_mix, gm_bias = _gmlp_mix_weights(p, tlen)
    nl = lru_h.shape[0]
    h_all = lru_h.reshape(nl, bsz, 1, MIX_W)
    sh_rkv = rw_shift[:, :, :RW_RKV].reshape(nl, bsz, 1, RW_RKV)
    sh_tail = jnp.pad(rw_shift[:, :, RW_RKV:], ((0, 0), (0, 0), (0, RW_TAIL_PAD - RW_TAIL)))
    sh_tail = sh_tail.reshape(nl, bsz, 1, RW_TAIL_PAD)
    s_all, c_all, n_all = rw_s, ml_c, ml_n
    m_all = jnp.broadcast_to(ml_m[:, :, :, None], (nl, bsz, ML_H, ML_HD))
    new_states = [[] for _ in range(8)]
    gm_vs = []
    hn = _prenorm(x, p["norm_pre_mix"], 0, tm)
    for l in range(depth):
        z = _inproj(hn, p["w_mix"], l, tm)
        z3 = z.reshape(bsz, tlen, Z_COLS)

        y_a, h_all = _lru(z3, lru_buf, h_all, p, l, is_start, lru_bb, lru_tt)
        y_b, gm_v = _gmlp(z, p, l, ws_mix, gm_bias)
        y_c, s_all = _rwkv(z3, sh_rkv, sh_tail, s_all, p, l, rw_nb, rw_l)
        y_c = y_c.reshape(m, MIX_W)
        ml_rows = ml_nb * ml_l
        zift = jnp.swapaxes(z[:, Z_MIF:Z_MIF + 8].reshape(m // ml_rows, ml_rows, 8), 1, 2)
        y_d, c_all, n_all, m_all = _mlstm(z3, zift, ml_buf, c_all, n_all, m_all, p, l, ml_nb, ml_l)
        y_d = y_d.reshape(m, MIX_W)

        merged = _merge(hn, (y_a.reshape(m, MIX_W), y_b, y_c, y_d), p["w_in_t"], p["w_branch"], l, tm)
        x1, hf = _outproj(merged, p["w_out"], x, p["norm_post_mix"], p["norm_pre_ffn"], l, min(TM_OUTPROJ, m))
        hmid = _ffup(hf, p["w_ff_in"], l, tm)
        x, hn = _ffdown(hmid, p["w_ff_out"], x1, p["norm_post_ffn"], p["norm_pre_mix"], l, min(TM_FFDOWN, m))

        new_states[0].append(z3[:, tlen - (CONV_W - 1):, 0:MIX_W])
        new_states[2].append(jnp.concatenate(
            [z3[:, tlen - 1, ZB_R * MIX_W:ZB_R * MIX_W + RW_RKV], z3[:, tlen - 1, Z_TAIL:Z_TAIL + RW_TAIL]], axis=-1))
        new_states[4].append(z3[:, tlen - (CONV_W - 1):, ZB_MX * MIX_W:(ZB_MX + 1) * MIX_W])
        gm_vs.append(gm_v.reshape(bsz, tlen, MIX_W))
    out_states = [jnp.stack(new_states[0], axis=0), h_all.reshape(nl, bsz, MIX_W)[:depth],
                  jnp.stack(new_states[2], axis=0), s_all[:depth],
                  jnp.stack(new_states[4], axis=0), c_all[:depth], n_all[:depth], m_all[:depth, :, :, 0]]
    return x.reshape(bsz, tlen, D_MODEL), out_states, jnp.stack(gm_vs, axis=0)


def kernel(x_prompt, x_sample, state_lru_conv, state_lru_h, state_rwkv_shift, state_rwkv_wkv, state_mlstm_conv, state_mlstm_C, state_mlstm_n, state_mlstm_m, norm_pre_mix, norm_post_mix, norm_pre_ffn, norm_post_ffn, w_in, lru_conv_w, lru_conv_b, lru_wa, lru_ba, lru_wx, lru_bx, lru_lambda, gmlp_ln_g, gmlp_ln_b, gmlp_ws, gmlp_bs, rwkv_mu, rwkv_w0, rwkv_w2, rwkv_a0, rwkv_a2, rwkv_g2, rwkv_kk, rwkv_ka, rwkv_rk, rwkv_ln_g, rwkv_ln_b, mlstm_conv_w, mlstm_conv_b, mlstm_wq, mlstm_wk, mlstm_bi, mlstm_bf, mlstm_norm_g, mlstm_skip, w_branch, w_out, w_ff_in, w_ff_out):
    raw = dict(norm_pre_mix=norm_pre_mix, norm_post_mix=norm_post_mix, norm_pre_ffn=norm_pre_ffn,
               norm_post_ffn=norm_post_ffn, w_in=w_in, lru_conv_w=lru_conv_w, lru_conv_b=lru_conv_b,
               lru_wa=lru_wa, lru_ba=lru_ba, lru_wx=lru_wx, lru_bx=lru_bx, lru_lambda=lru_lambda,
               gmlp_ln_g=gmlp_ln_g, gmlp_ln_b=gmlp_ln_b, gmlp_ws=gmlp_ws, gmlp_bs=gmlp_bs,
               rwkv_mu=rwkv_mu, rwkv_w0=rwkv_w0, rwkv_w2=rwkv_w2, rwkv_a0=rwkv_a0, rwkv_a2=rwkv_a2,
               rwkv_g2=rwkv_g2, rwkv_kk=rwkv_kk, rwkv_ka=rwkv_ka, rwkv_rk=rwkv_rk, rwkv_ln_g=rwkv_ln_g,
               rwkv_ln_b=rwkv_ln_b, mlstm_conv_w=mlstm_conv_w, mlstm_conv_b=mlstm_conv_b, mlstm_wq=mlstm_wq,
               mlstm_wk=mlstm_wk, mlstm_bi=mlstm_bi, mlstm_bf=mlstm_bf, mlstm_norm_g=mlstm_norm_g,
               mlstm_skip=mlstm_skip, w_branch=w_branch, w_out=w_out, w_ff_in=w_ff_in, w_ff_out=w_ff_out)
    p = _prepare(raw)
    bp = x_prompt.shape[0]
    zero = lambda *s: jnp.zeros((DEPTH, bp) + s, F32)
    prompt_states = (zero(CONV_W - 1, MIX_W), zero(MIX_W), zero(RW_RKV + RW_TAIL), zero(RW_H, RW_HD, RW_HD),
                     zero(CONV_W - 1, MIX_W), zero(ML_H, ML_HD, ML_HD), zero(ML_H, ML_HD), zero(ML_H))
    sample_states = (state_lru_conv, state_lru_h, state_rwkv_shift, state_rwkv_wkv,
                     state_mlstm_conv, state_mlstm_C, state_mlstm_n, state_mlstm_m)
    yp, st_p, _ = _group_forward(x_prompt, prompt_states, p, True)
    ys, st_s, gm_v = _group_forward(x_sample, sample_states, p, False)
    return (yp, ys, *st_p, *st_s, gm_v)
```

```python
import functools

import numpy as np
import jax
import jax.numpy as jnp
from jax import lax
from jax.experimental import pallas as pl
from jax.experimental.pallas import tpu as pltpu

F32 = jnp.float32
BF16 = jnp.bfloat16

D_MODEL = 2048
DEPTH = 4
MIX_W = 512
CONV_W = 4
LRU_BLOCKS = 8
LRU_C = 8.0
GM_CHUNK = 128
GM_GROUPS = 4
RW_HD = 64
RW_H = 8
RW_RKV = 3 * MIX_W
RW_TAIL = 160
RW_TAIL_PAD = 256
RW_LN_EPS = 64e-5
ML_H = 4
ML_HD = 128
D_FF = 5632
N_GATE = 4 * D_MODEL
P_SRC = 5288
Z_COLS = 5632

ZB_LX, ZB_LG, ZB_GU, ZB_GV, ZB_R, ZB_K, ZB_V, ZB_MX, ZB_MV, ZB_MO = range(10)
Z_TAIL = 5120
Z_MIF = 5376

RW_CHUNK = 64
RW_SEQS_P = 4
ML_CHUNK_P = 256
GM_TILE_ROWS = 512
TM_DENSE = 1024
TN_INPROJ = Z_COLS // 2
TM_OUTPROJ = 1024
OUTPROJ_SUB_ROWS = 256
FFUP_SUB_ROWS = 256
FFDOWN_SUB_ROWS = 256
TM_FFDOWN = 512
TN_MERGE = 256
LRU_SEQS_P = 4
LRU_TILE_P = 256
VMEM_LIMIT = 60 * 1024 * 1024


def _cparams(sem):
    return pltpu.CompilerParams(dimension_semantics=sem, vmem_limit_bytes=VMEM_LIMIT)


def _softplus(x):
    return jnp.maximum(x, 0.0) + jnp.log1p(jnp.exp(-jnp.abs(x)))


def _log_sigmoid(x):
    return -_softplus(-x)


def _rms(x, g):
    return x * lax.rsqrt(jnp.mean(x * x, axis=-1, keepdims=True) + 1e-6) * g


def _dot(a, b):
    return jnp.dot(a.astype(BF16), b.astype(BF16), preferred_element_type=F32)


def _nt(a, b):
    return lax.dot_general(a, b, (((1,), (1,)), ((), ())), preferred_element_type=F32)


def _dot_nt(a, b):
    return lax.dot_general(a.astype(BF16), b.astype(BF16), (((1,), (1,)), ((), ())),
                           preferred_element_type=F32)


def _dot_tn(a, b):
    return lax.dot_general(a.astype(BF16), b.astype(BF16), (((0,), (0,)), ((), ())),
                           preferred_element_type=F32)


def _prenorm_kernel(x_ref, g_ref, hn_ref):
    hn_ref[...] = _rms(x_ref[...], g_ref[...]).astype(BF16)


def _prenorm(x, g, l, tm):
    m = x.shape[0]
    row_spec = pl.BlockSpec((tm, D_MODEL), lambda i: (i, 0))
    return pl.pallas_call(
        _prenorm_kernel,
        grid=(m // tm,),
        in_specs=[row_spec, pl.BlockSpec((None, 1, D_MODEL), lambda i: (l, 0, 0))],
        out_specs=row_spec,
        out_shape=jax.ShapeDtypeStruct((m, D_MODEL), BF16),
        compiler_params=_cparams(("parallel",)),
        name="prenorm",
    )(x, g)


def _inproj_kernel(hn_ref, w_ref, z_ref):
    tn = z_ref.shape[1]
    rows = pl.multiple_of(pl.program_id(1) * tn, tn)
    z_ref[...] = _nt(hn_ref[...], w_ref[pl.ds(rows, tn), :])


def _inproj(hn, w, l, tm, tn=TN_INPROJ):
    m = hn.shape[0]
    return pl.pallas_call(
        _inproj_kernel,
        grid=(m // tm, Z_COLS // tn),
        in_specs=[pl.BlockSpec((tm, D_MODEL), lambda i, j: (i, 0)),
                  pl.BlockSpec((None, Z_COLS, D_MODEL), lambda i, j: (l, 0, 0), pipeline_mode=pl.Buffered(1))],
        out_specs=pl.BlockSpec((tm, tn), lambda i, j: (i, j)),
        out_shape=jax.ShapeDtypeStruct((m, Z_COLS), F32),
        compiler_params=_cparams(("parallel", "arbitrary")),
        name="inproj",
    )(hn, w)


def _merge_kernel(hn_ref, ya_ref, yb_ref, yc_ref, yd_ref, g0_ref, g1_ref, g2_ref, g3_ref, wb_ref, o_ref,
                  gb_ref, wbb_ref):
    @pl.when(pl.program_id(1) == 0)
    def _():
        for b, g_ref in enumerate((g0_ref, g1_ref, g2_ref, g3_ref)):
            gb_ref[b] = g_ref[0].astype(BF16)
        wbb_ref[...] = wb_ref[...].astype(BF16)

    hn = hn_ref[...]
    acc = None
    for b, y_ref in enumerate((ya_ref, yb_ref, yc_ref, yd_ref)):
        zg = _nt(hn, gb_ref[b])
        br = jnp.dot(y_ref[...], wbb_ref[b], preferred_element_type=F32)
        term = jax.nn.sigmoid(zg) * br
        acc = term if acc is None else acc + term
    o_ref[...] = acc.astype(BF16)


def _merge(hn, ys, w_in_t, wbranch, l, tm, tn=TN_MERGE):
    m = hn.shape[0]
    nb = D_MODEL // tn
    y_spec = pl.BlockSpec((tm, MIX_W), lambda j, i: (i, 0))
    g_specs = [pl.BlockSpec((pl.Element(1), pl.Element(tn), pl.Element(D_MODEL)),
                            lambda j, i, b=b: (l, pl.multiple_of(P_SRC + b * D_MODEL + j * tn, 8), 0))
               for b in range(4)]
    return pl.pallas_call(
        _merge_kernel,
        grid=(nb, m // tm),
        in_specs=[pl.BlockSpec((tm, D_MODEL), lambda j, i: (i, 0)), y_spec, y_spec, y_spec, y_spec,
                  *g_specs,
                  pl.BlockSpec((None, 4, MIX_W, tn), lambda j, i: (l, 0, 0, j))],
        out_specs=pl.BlockSpec((tm, tn), lambda j, i: (i, j)),
        out_shape=jax.ShapeDtypeStruct((m, D_MODEL), BF16),
        scratch_shapes=[pltpu.VMEM((4, tn, D_MODEL), BF16), pltpu.VMEM((4, MIX_W, tn), BF16)],
        compiler_params=_cparams(("parallel", "arbitrary")),
        name="merge",
    )(hn, *ys, w_in_t, w_in_t, w_in_t, w_in_t, wbranch)


def _outproj_kernel(mg_ref, w_ref, x_ref, gpost_ref, gpre_ref, x1_ref, hf_ref):
    sub = OUTPROJ_SUB_ROWS
    for r0 in range(0, mg_ref.shape[0], sub):
        rs = slice(r0, r0 + sub)
        mix = jnp.dot(mg_ref[rs, :], w_ref[...], preferred_element_type=F32)
        x1 = x_ref[rs, :] + _rms(mix, gpost_ref[...])
        x1_ref[rs, :] = x1
        hf_ref[rs, :] = _rms(x1, gpre_ref[...]).astype(BF16)


def _outproj(merged, w_out, x, g_post, g_pre_ffn, l, tm):
    m = x.shape[0]
    g_spec = pl.BlockSpec((None, 1, D_MODEL), lambda i: (l, 0, 0))
    row_spec = pl.BlockSpec((tm, D_MODEL), lambda i: (i, 0))
    return pl.pallas_call(
        _outproj_kernel,
        grid=(m // tm,),
        in_specs=[row_spec,
                  pl.BlockSpec((None, D_MODEL, D_MODEL), lambda i: (l, 0, 0), pipeline_mode=pl.Buffered(1)),
                  row_spec, g_spec, g_spec],
        out_specs=[row_spec, row_spec],
        out_shape=[jax.ShapeDtypeStruct((m, D_MODEL), F32), jax.ShapeDtypeStruct((m, D_MODEL), BF16)],
        compiler_params=_cparams(("parallel",)),
        name="outproj",
    )(merged, w_out, x, g_post, g_pre_ffn)


def _ffup_kernel(hf_ref, wg_ref, wu_ref, o_ref, wgb_ref, wub_ref):
    @pl.when(pl.program_id(1) == 0)
    def _():
        wgb_ref[...] = wg_ref[...].astype(BF16)
        wub_ref[...] = wu_ref[...].astype(BF16)

    sub = min(FFUP_SUB_ROWS, hf_ref.shape[0])
    for r0 in range(0, hf_ref.shape[0], sub):
        rs = slice(r0, r0 + sub)
        hf = hf_ref[rs, :]
        g = jnp.dot(hf, wgb_ref[...], preferred_element_type=F32)
        u = jnp.dot(hf, wub_ref[...], preferred_element_type=F32)
        o_ref[rs, :] = (jax.nn.silu(g) * u).astype(BF16)


def _ffup(hf, w_ff_in, l, tm, tn=512):
    m = hf.shape[0]
    nb = D_FF // tn
    return pl.pallas_call(
        _ffup_kernel,
        grid=(nb, m // tm),
        in_specs=[pl.BlockSpec((tm, D_MODEL), lambda j, i: (i, 0)),
                  pl.BlockSpec((None, D_MODEL, tn), lambda j, i: (l, 0, j)),
                  pl.BlockSpec((None, D_MODEL, tn), lambda j, i: (l, 0, nb + j))],
        out_specs=pl.BlockSpec((tm, tn), lambda j, i: (i, j)),
        out_shape=jax.ShapeDtypeStruct((m, D_FF), BF16),
        scratch_shapes=[pltpu.VMEM((D_MODEL, tn), BF16), pltpu.VMEM((D_MODEL, tn), BF16)],
        compiler_params=_cparams(("parallel", "arbitrary")),
        name="ffup",
    )(hf, w_ff_in, w_ff_in)


def _ffdown_kernel(h_ref, w_ref, x1_ref, g_ref, gnext_ref, o_ref, hn_ref):
    sub = FFDOWN_SUB_ROWS
    for r0 in range(0, h_ref.shape[0], sub):
        rs = slice(r0, r0 + sub)
        ff = jnp.dot(h_ref[rs, :], w_ref[...], preferred_element_type=F32)
        x2 = x1_ref[rs, :] + _rms(ff, g_ref[...])
        o_ref[rs, :] = x2
        hn_ref[rs, :] = _rms(x2, gnext_ref[...]).astype(BF16)


def _ffdown(h, w_ff_out, x1, g_post, g_pre_mix, l, tm):
    m = x1.shape[0]
    l_next = min(l + 1, DEPTH - 1)
    row_spec = pl.BlockSpec((tm, D_MODEL), lambda i: (i, 0))
    return pl.pallas_call(
        _ffdown_kernel,
        grid=(m // tm,),
        in_specs=[pl.BlockSpec((tm, D_FF), lambda i: (i, 0)),
                  pl.BlockSpec((None, D_FF, D_MODEL), lambda i: (l, 0, 0), pipeline_mode=pl.Buffered(1)),
                  row_spec,
                  pl.BlockSpec((None, 1, D_MODEL), lambda i: (l, 0, 0)),
                  pl.BlockSpec((None, 1, D_MODEL), lambda i: (l_next, 0, 0))],
        out_specs=[row_spec, row_spec],
        out_shape=[jax.ShapeDtypeStruct((m, D_MODEL), F32), jax.ShapeDtypeStruct((m, D_MODEL), BF16)],
        compiler_params=_cparams(("parallel",)),
        name="ffdown",
    )(h, w_ff_out, x1, g_post, g_pre_mix)


def _lru_kernel(zx_ref, zg_ref, buf_ref, h0_ref, cw_ref, cb_ref, wa_ref, ba_ref, wx_ref, bx_ref, lam_ref,
                y_ref, hout_ref, xs_ref, a_ref, b_ref, h_ref, *, is_start, bb, tt):
    t = pl.program_id(1)
    c = MIX_W

    @pl.when(t == 0)
    def _():
        xs_ref[:, 5:8, :] = buf_ref[...]
        h_ref[...] = jnp.broadcast_to(h0_ref[...], (bb, 8, c))

    @pl.when(t > 0)
    def _():
        xs_ref[:, 5:8, :] = xs_ref[:, tt + 5:tt + 8, :]

    xs_ref[:, 8:8 + tt, :] = zx_ref[...]
    xc = cb_ref[...] + cw_ref[0:1, :] * xs_ref[:, 5:5 + tt, :]
    for j in range(1, CONV_W):
        xc = xc + cw_ref[j:j + 1, :] * xs_ref[:, 5 + j:5 + j + tt, :]
    xc2 = xc.reshape(bb * tt, c)
    r = jax.nn.sigmoid(_dot(xc2, wa_ref[...]) + ba_ref[...])
    i = jax.nn.sigmoid(_dot(xc2, wx_ref[...]) + bx_ref[...])
    log_a = LRU_C * r * _log_sigmoid(lam_ref[...])
    a = jnp.exp(log_a)
    mult = jnp.sqrt(1.0 - jnp.exp(2.0 * log_a))
    if is_start:
        tpos = lax.broadcasted_iota(jnp.int32, (bb, tt, c), 1).reshape(bb * tt, c) + t * tt
        mult = jnp.where(tpos == 0, 1.0, mult)
    a_ref[...] = a.reshape(bb, tt, c)
    b_ref[...] = (mult * i * xc2).reshape(bb, tt, c)

    row = lax.broadcasted_iota(jnp.int32, (bb, 8, c), 1).reshape(bb * 8, c)

    def group(gi, carry):
        off = pl.multiple_of(gi * 8, 8)
        av = a_ref[:, pl.ds(off, 8), :].reshape(bb * 8, c)
        bv = b_ref[:, pl.ds(off, 8), :].reshape(bb * 8, c)
        for s in (1, 2, 4):
            keep = row >= s
            a_sh = pltpu.roll(av, s, 0)
            b_sh = pltpu.roll(bv, s, 0)
            bv = jnp.where(keep, av * b_sh + bv, bv)
            av = jnp.where(keep, av * a_sh, av)
        hh = (av * h_ref[...].reshape(bb * 8, c) + bv).reshape(bb, 8, c)
        b_ref[:, pl.ds(off, 8), :] = hh
        h_ref[...] = jnp.broadcast_to(hh[:, 7:8, :], (bb, 8, c))
        return carry

    lax.fori_loop(0, tt // 8, group, 0)
    y_ref[...] = (b_ref[...] * jax.nn.gelu(zg_ref[...])).astype(BF16)

    @pl.when(t == pl.num_programs(1) - 1)
    def _():
        hout_ref[...] = h_ref[:, 7:8, :]


def _lru(z3, buf, h_all, p, l, is_start, bb, tt):
    bsz, tlen, _ = z3.shape
    c = MIX_W
    vec = lambda: pl.BlockSpec((None, 1, c), lambda b, t: (l, 0, 0))
    mat = lambda: pl.BlockSpec((None, c, c), lambda b, t: (l, 0, 0))
    hspec = lambda: pl.BlockSpec((None, bb, 1, c), lambda b, t: (l, b, 0, 0))
    kern = functools.partial(_lru_kernel, is_start=is_start, bb=bb, tt=tt)
    return pl.pallas_call(
        kern,
        grid=(bsz // bb, tlen // tt),
        in_specs=[pl.BlockSpec((bb, tt, c), lambda b, t: (b, t, ZB_LX)),
                  pl.BlockSpec((bb, tt, c), lambda b, t: (b, t, ZB_LG)),
                  pl.BlockSpec((None, bb, CONV_W - 1, c), lambda b, t: (l, b, 0, 0)),
                  hspec(),
                  pl.BlockSpec((None, CONV_W, c), lambda b, t: (l, 0, 0)),
                  vec(), mat(), vec(), mat(), vec(), vec()],
        out_specs=[pl.BlockSpec((bb, tt, c), lambda b, t: (b, t, 0)), hspec()],
        out_shape=[jax.ShapeDtypeStruct((bsz, tlen, c), BF16),
                   jax.ShapeDtypeStruct(h_all.shape, F32)],
        input_output_aliases={3: 1},
        scratch_shapes=[pltpu.VMEM((bb, 8 + tt, c), F32), pltpu.VMEM((bb, tt, c), F32),
                        pltpu.VMEM((bb, tt, c), F32), pltpu.VMEM((bb, 8, c), F32)],
        compiler_params=_cparams(("parallel", "arbitrary")),
        name="rglru",
    )(z3, z3, buf, h_all, p["lru_conv_w"], p["lru_conv_b"], p["lru_wa"], p["lru_ba"], p["lru_wx"], p["lru_bx"],
      p["lru_lambda"])


def _gmlp_kernel(zu_ref, zv_ref, lng_ref, lnb_ref, ws_ref, bias_ref, y_ref, v_ref):
    u = jax.nn.gelu(zu_ref[...])
    gv = jax.nn.gelu(zv_ref[...])
    vc = gv - jnp.mean(gv, axis=-1, keepdims=True)
    v = vc * lax.rsqrt(jnp.mean(vc * vc, axis=-1, keepdims=True) + 1e-5) * lng_ref[...] + lnb_ref[...]
    v_ref[...] = v
    gd = MIX_W // GM_GROUPS
    vb = v.astype(BF16)
    for ck in range(v.shape[0] // GM_CHUNK):
        rs = slice(ck * GM_CHUNK, (ck + 1) * GM_CHUNK)
        for g in range(GM_GROUPS):
            sl = slice(g * gd, (g + 1) * gd)
            s = jnp.dot(ws_ref[g], vb[rs, sl], preferred_element_type=F32) + bias_ref[:, sl]
            y_ref[rs, sl] = (u[rs, sl] * s).astype(BF16)


def _gmlp(z2, p, l, ws_mix, bias_tile):
    m = z2.shape[0]
    c = MIX_W
    rows = GM_TILE_ROWS
    vec = lambda: pl.BlockSpec((None, 1, c), lambda i: (l, 0, 0))
    return pl.pallas_call(
        _gmlp_kernel,
        grid=(m // rows,),
        in_specs=[pl.BlockSpec((rows, c), lambda i: (i, ZB_GU)),
                  pl.BlockSpec((rows, c), lambda i: (i, ZB_GV)),
                  vec(), vec(),
                  pl.BlockSpec((None, GM_GROUPS, GM_CHUNK, GM_CHUNK), lambda i: (l, 0, 0, 0)),
                  pl.BlockSpec((None, GM_CHUNK, c), lambda i: (l, 0, 0))],
        out_specs=[pl.BlockSpec((rows, c), lambda i: (i, 0)),
                   pl.BlockSpec((rows, c), lambda i: (i, 0))],
        out_shape=[jax.ShapeDtypeStruct((m, c), BF16), jax.ShapeDtypeStruct((m, c), F32)],
        compiler_params=_cparams(("parallel",)),
        name="gmlp",
    )(z2, z2, p["gmlp_ln_g"], p["gmlp_ln_b"], ws_mix, bias_tile)


RW_ROWS = 64
RW_PACK = 4


def _dot_ones(x, ones, pieces, ones_on_left=False):
    acc = None
    rem = x
    for i in range(pieces):
        part = rem.astype(BF16)
        if i + 1 < pieces:
            rem = rem - part.astype(F32)
        term = (jnp.dot(ones, part, preferred_element_type=F32) if ones_on_left
                else jnp.dot(part, ones, preferred_element_type=F32))
        acc = term if acc is None else acc + term
    return acc


def _head_sums(x, ones):
    half = MIX_W // 2
    rows = x.shape[0]
    stacked = jnp.concatenate([x[:, :half], x[:, half:]], axis=0)
    s = _dot_ones(stacked, ones, 2)
    return jnp.concatenate([s[:rows], s[rows:]], axis=1)


def _rwkv_kernel(zr_ref, zk_ref, zv_ref, zt_ref, shr_ref, sht_ref, s0_ref,
                 mur_ref, mut_ref, w0_ref, a0_ref, kkp_ref, kap_ref, rk_ref, lng_ref, lnb_ref,
                 w2_ref, a2_ref, g2_ref, hsum_ref,
                 y_ref, sout_ref, prev_ref, s_ref, *, nb, L, carry):
    ci = pl.program_id(1)
    c = MIX_W
    rows = nb * L
    bb = RW_ROWS // L
    ngrp = rows // RW_ROWS
    log_l = L.bit_length() - 1

    @pl.when(ci == 0)
    def _():
        prev_ref[:, :, 0:RW_RKV] = shr_ref[...]
        prev_ref[:, :, RW_RKV:RW_RKV + RW_TAIL_PAD] = sht_ref[...]

    first = (lax.broadcasted_iota(jnp.int32, (rows, 1), 0) & (L - 1)) == 0

    def shift(z_ref, lo, hi, mu):
        w = hi - lo
        z3 = z_ref[...]
        z = z3.reshape(rows, w)
        prev = jnp.broadcast_to(prev_ref[:, :, lo:hi], (nb, L, w)).reshape(rows, w)
        zp = jnp.where(first, prev, pltpu.roll(z, 1, 0))
        prev_ref[:, :, lo:hi] = z3[:, L - 1:L, :]
        return z + (zp - z) * mu

    r = shift(zr_ref, 0, c, mur_ref[:, 0:c])
    k = shift(zk_ref, c, 2 * c, mur_ref[:, c:2 * c])
    v = shift(zv_ref, 2 * c, 3 * c, mur_ref[:, 2 * c:3 * c])
    tl = shift(zt_ref, RW_RKV, RW_RKV + RW_TAIL_PAD, mut_ref[...])

    wlin = w0_ref[...] + _dot(jnp.tanh(tl), w2_ref[...])
    logw = -jnp.exp(-_softplus(-wlin) - 0.5)
    a = jax.nn.sigmoid(a0_ref[...] + _dot(tl, a2_ref[...]))
    gate = _dot(jax.nn.sigmoid(tl), g2_ref[...])
    kkf = k * kkp_ref[...]
    kmod = k * (1.0 + (a - 1.0) * kap_ref[...])
    hsum = hsum_ref[...]
    kkn = kkf / jnp.maximum(jnp.sqrt(_head_sums(kkf * kkf, hsum)), 1e-12)
    beta = kkn * a
    bonus = _head_sums(r * kmod * rk_ref[...], hsum) * v

    ri_f = lax.broadcasted_iota(jnp.int32, (rows, rows), 0)
    cj_f = lax.broadcasted_iota(jnp.int32, (rows, rows), 1)
    cum_op = (((ri_f >> log_l) == (cj_f >> log_l)) & (cj_f <= ri_f)).astype(BF16)
    cum = _dot_ones(logw, cum_op, 3, ones_on_left=True)
    c_last = cum.reshape(nb, L, c)[:, L - 1:L, :]
    e_last = jnp.exp(c_last)
    e_rest = jnp.exp(jnp.broadcast_to(c_last, (nb, L, c)).reshape(rows, c) - cum)
    e_mc = jnp.exp(-cum)
    a_t = -kkn * jnp.exp(cum - logw)
    r_t = r * jnp.exp(cum)
    b_t = beta * e_mc
    k_t = kmod * e_mc
    b_hat = beta * e_rest
    k_hat = kmod * e_rest

    n = RW_ROWS
    hp = RW_PACK
    w4 = hp * RW_HD
    ngq = RW_H // hp
    ti = lax.broadcasted_iota(jnp.int32, (n, w4), 0)
    jl = lax.broadcasted_iota(jnp.int32, (n, w4), 1) & (RW_HD - 1)
    same = (ti >> log_l) == (jl >> log_l)
    strict = same & (jl < ti)
    incl = same & (jl <= ti)
    eye = (jl == ti).astype(F32)
    lvl = [((ti >> (s + 1)) == (jl >> (s + 1))) & ((ti >> s) != (jl >> s)) & (jl < ti) for s in range(log_l)]
    log_hd = RW_HD.bit_length() - 1
    bmask = ((lax.broadcasted_iota(jnp.int32, (w4, w4), 0) >> log_hd)
             == (lax.broadcasted_iota(jnp.int32, (w4, w4), 1) >> log_hd))

    def bd(xb):
        return jnp.where(bmask, jnp.concatenate([xb] * hp, axis=0), jnp.zeros((), BF16))

    def pdot(a4, b4):
        return jnp.dot(a4.astype(BF16), bd(b4.astype(BF16)), preferred_element_type=F32)

    def to_bd(blocks):
        rows_ = []
        for i, blk in enumerate(blocks):
            parts = [blk if j == i else jnp.zeros((RW_HD, RW_HD), F32) for j in range(hp)]
            rows_.append(jnp.concatenate(parts, axis=1))
        return jnp.concatenate(rows_, axis=0)

    probs = [(g, q) for g in range(ngrp) for q in range(ngq)]

    def cut(x, g, q):
        return x[g * n:(g + 1) * n, q * w4:(q + 1) * w4]

    if carry:
        @pl.when(ci == 0)
        def _():
            for sq in range(nb):
                for q in range(ngq):
                    s_ref[sq, q] = to_bd([s0_ref[sq, q * hp + i] for i in range(hp)])
        s0 = {(g, q, j): s_ref[g * bb + j, q] for (g, q) in probs for j in range(bb)}
    else:
        s0 = {(g, q, j): to_bd([s0_ref[g * bb + j, q * hp + i] for i in range(hp)])
              for (g, q) in probs for j in range(bb)}

    ar, m_ba, m_ka, m_br, m_kr = {}, {}, {}, {}, {}
    for p in probs:
        ar[p] = jnp.concatenate([cut(a_t, *p), cut(r_t, *p)], axis=0).astype(BF16)
        mb = _nt(ar[p], bd(cut(b_t, *p).astype(BF16)))
        mk = _nt(ar[p], bd(cut(k_t, *p).astype(BF16)))
        m_ba[p] = jnp.where(strict, mb[0:n], 0.0)
        m_br[p] = jnp.where(incl, mb[n:2 * n], 0.0)
        m_ka[p] = jnp.where(strict, mk[0:n], 0.0)
        m_kr[p] = jnp.where(incl, mk[n:2 * n], 0.0)

    inv = {p: eye + jnp.where(lvl[0], m_ba[p], 0.0) for p in probs}
    for s in range(1, log_l):
        half = {p: pdot(inv[p], jnp.where(lvl[s], m_ba[p], 0.0)) for p in probs}
        inv = {p: inv[p] + pdot(half[p], inv[p]) for p in probs}

    xa, xr = {}, {}
    for p in probs:
        pa, pr = [], []
        for j in range(bb):
            arj = ar[p] if bb == 1 else jnp.concatenate(
                [ar[p][j * L:(j + 1) * L], ar[p][n + j * L:n + (j + 1) * L]], axis=0)
            as0 = _nt(arj, s0[p + (j,)].astype(BF16))
            pa.append(as0[0:L])
            pr.append(as0[L:2 * L])
        xa[p] = pa[0] if bb == 1 else jnp.concatenate(pa, axis=0)
        xr[p] = pr[0] if bb == 1 else jnp.concatenate(pr, axis=0)

    vv = {p: cut(v, *p) for p in probs}
    vbd = {p: bd(vv[p].astype(BF16)) for p in probs}
    u = {p: pdot(inv[p], xa[p] + jnp.dot(m_ka[p].astype(BF16), vbd[p], preferred_element_type=F32)) for p in probs}
    o = {p: xr[p] + pdot(m_br[p], u[p]) + jnp.dot(m_kr[p].astype(BF16), vbd[p], preferred_element_type=F32)
         for p in probs}

    s_new = {}
    for (g, q) in probs:
        bh, kh = cut(b_hat, g, q), cut(k_hat, g, q)
        for j in range(bb):
            js = slice(j * L, (j + 1) * L)
            uv = jnp.concatenate([u[(g, q)][js], vv[(g, q)][js]], axis=0)
            bk = jnp.concatenate([bh[js], kh[js]], axis=0)
            seq = g * bb + j
            decay = e_last[seq][:, q * w4:(q + 1) * w4]
            s_new[(g, q, j)] = jnp.where(bmask, s0[(g, q, j)] * decay + _dot_tn(uv, bk), 0.0)

    o_rows = [jnp.concatenate([o[(g, q)] for q in range(ngq)], axis=1) for g in range(ngrp)]
    o_all = o_rows[0] if ngrp == 1 else jnp.concatenate(o_rows, axis=0)
    oc = o_all - _head_sums(o_all, hsum) * (1.0 / RW_HD)
    on = oc * lax.rsqrt(_head_sums(oc * oc, hsum) * (1.0 / RW_HD) + RW_LN_EPS)
    y = (on * lng_ref[...] + lnb_ref[...] + bonus) * gate
    y_ref[...] = y.reshape(nb, L, c).astype(BF16)

    def diag_block(m, i):
        return m[i * RW_HD:(i + 1) * RW_HD, i * RW_HD:(i + 1) * RW_HD]

    if carry:
        for (g, q, j), val in s_new.items():
            s_ref[g * bb + j, q] = val

        @pl.when(ci == pl.num_programs(1) - 1)
        def _():
            for sq in range(nb):
                for q in range(ngq):
                    for i in range(hp):
                        sout_ref[sq, q * hp + i] = diag_block(s_ref[sq, q], i)
    else:
        for (g, q, j), val in s_new.items():
            for i in range(hp):
                sout_ref[g * bb + j, q * hp + i] = diag_block(val, i)


def _rwkv(z3, sh_rkv, sh_tail, s_all, p, l, nb, L):
    bsz, tlen, _ = z3.shape
    c = MIX_W
    nc = tlen // L
    carry = nc > 1
    half = c // 2
    hsum = jnp.asarray(np.kron(np.eye(half // RW_HD, dtype=np.float32), np.ones((RW_HD, RW_HD), np.float32)), BF16)
    vec = lambda w=c: pl.BlockSpec((None, 1, w), lambda b, i: (l, 0, 0))
    lora = lambda: pl.BlockSpec((None, RW_TAIL_PAD, c), lambda b, i: (l, 0, 0))
    zspec = lambda blk: pl.BlockSpec((nb, L, c), lambda b, i, blk=blk: (b, i, blk))
    sspec = lambda: pl.BlockSpec((None, nb, RW_H, RW_HD, RW_HD), lambda b, i: (l, b, 0, 0, 0))
    kern = functools.partial(_rwkv_kernel, nb=nb, L=L, carry=carry)
    return pl.pallas_call(
        kern,
        grid=(bsz // nb, nc),
        in_specs=[zspec(ZB_R), zspec(ZB_K), zspec(ZB_V),
                  pl.BlockSpec((nb, L, RW_TAIL_PAD), lambda b, i: (b, i, Z_TAIL // RW_TAIL_PAD)),
                  pl.BlockSpec((None, nb, 1, RW_RKV), lambda b, i: (l, b, 0, 0)),
                  pl.BlockSpec((None, nb, 1, RW_TAIL_PAD), lambda b, i: (l, b, 0, 0)),
                  sspec(),
                  vec(RW_RKV), vec(RW_TAIL_PAD), vec(), vec(), vec(), vec(), vec(), vec(), vec(),
                  lora(), lora(), lora(),
                  pl.BlockSpec((half, half), lambda b, i: (0, 0))],
        out_specs=[pl.BlockSpec((nb, L, c), lambda b, i: (b, i, 0)), sspec()],
        out_shape=[jax.ShapeDtypeStruct((bsz, tlen, c), BF16),
                   jax.ShapeDtypeStruct(s_all.shape, F32)],
        input_output_aliases={6: 1},
        scratch_shapes=[pltpu.VMEM((nb, 1, RW_RKV + RW_TAIL_PAD), F32),
                        pltpu.VMEM((nb, RW_H // RW_PACK, RW_PACK * RW_HD, RW_PACK * RW_HD) if carry
                                   else (1, 1, 8, 128), F32)],
        compiler_params=_cparams(("parallel", "arbitrary")),
        name="rwkv7",
    )(z3, z3, z3, z3, sh_rkv, sh_tail, s_all,
      p["rwkv_mu_rkv"], p["rwkv_mu_tail"], p["rwkv_w0"], p["rwkv_a0"], p["rwkv_kk"], p["rwkv_ka"], p["rwkv_rk"],
      p["rwkv_ln_g"], p["rwkv_ln_b"], p["rwkv_w2"], p["rwkv_a2"], p["rwkv_g2"], hsum)


ML_ROWS = 128


def _mlstm_kernel(zx_ref, zv_ref, zo_ref, zif_ref, zift_ref, buf_ref, c0_ref, n0_ref, m0_ref,
                  cw_ref, cb_ref, wq_ref, wk_ref, brow_ref, bcol_ref, ng_ref, skip_ref,
                  y_ref, cout_ref, nout_ref, mout_ref, xs_ref, c_ref, n_ref, m_ref, *, nb, L, carry):
    ci = pl.program_id(1)
    c = MIX_W
    rows = nb * L
    log_l = L.bit_length() - 1

    @pl.when(ci == 0)
    def _():
        xs_ref[:, 5:8, :] = buf_ref[...]
        if carry:
            c_ref[...] = c0_ref[...]
            n_ref[...] = n0_ref[...]
            m_ref[...] = m0_ref[...]

    if carry:
        @pl.when(ci > 0)
        def _():
            xs_ref[:, 5:8, :] = xs_ref[:, L + 5:L + 8, :]

    xs_ref[:, 8:8 + L, :] = zx_ref[...]
    conv = cb_ref[...] + cw_ref[0:1, :] * xs_ref[:, 5:5 + L, :]
    for j in range(1, CONV_W):
        conv = conv + cw_ref[j:j + 1, :] * xs_ref[:, 5 + j:5 + j + L, :]
    cc = jax.nn.silu(conv).reshape(rows, c)
    zv = zv_ref[...].reshape(rows, c)
    zo = zo_ref[...].reshape(rows, c)

    ri = lax.broadcasted_iota(jnp.int32, (rows, rows), 0)
    cj = lax.broadcasted_iota(jnp.int32, (rows, rows), 1)
    same = (ri >> log_l) == (cj >> log_l)
    causal = same & (cj <= ri)

    gate_col = zif_ref[...].reshape(rows, 128) + brow_ref[...]
    bcum_col = _dot_ones(_log_sigmoid(gate_col), causal.astype(BF16), 3, ones_on_left=True)
    gate_row = zift_ref[...] + bcol_ref[...]
    bcum_row = _dot_ones(_log_sigmoid(gate_row), (same & (ri <= cj)).astype(BF16), 3)

    def per_row(x3):
        return jnp.broadcast_to(x3, (nb, L, x3.shape[-1])).reshape(rows, x3.shape[-1])

    def last(x):
        return x.reshape(nb, L, x.shape[-1])[:, L - 1:L, :]

    c_in, n_in, m_in = (c_ref, n_ref, m_ref) if carry else (c0_ref, n0_ref, m0_ref)
    c_out, n_out, m_out = (c_ref, n_ref, m_ref) if carry else (cout_ref, nout_ref, mout_ref)
    heads = range(ML_H)
    sls = [slice(h * ML_HD, (h + 1) * ML_HD) for h in heads]
    cmat = {(h, j): c_in[j, h] for h in heads for j in range(nb)}
    nvec = [n_in[:, h:h + 1, :] for h in heads]
    m_prev = [m_in[:, h:h + 1, 0:1] for h in heads]

    ch = [cc[:, sls[h]] for h in heads]
    q = [_dot(ch[h], wq_ref[h]) for h in heads]
    k = [_dot(ch[h], wk_ref[h]) * (ML_HD ** -0.5) for h in heads]
    v = [zv[:, sls[h]] for h in heads]
    qk = [_dot_nt(q[h], k[h]) for h in heads]
    qc = []
    for h in heads:
        parts = [_dot_nt(q[h][j * L:(j + 1) * L], cmat[(h, j)]) for j in range(nb)]
        qc.append(parts[0] if nb == 1 else jnp.concatenate(parts, axis=0))

    b_col = [bcum_col[:, ML_H + h:ML_H + h + 1] for h in heads]
    li_col = [gate_col[:, h:h + 1] for h in heads]
    m_t, s, sc = [], [], []
    for h in heads:
        log_d = jnp.where(causal, b_col[h] - bcum_row[ML_H + h:ML_H + h + 1, :] + gate_row[h:h + 1, :], -jnp.inf)
        inter = b_col[h] + per_row(m_prev[h])
        mt = jnp.maximum(jnp.max(log_d, axis=-1, keepdims=True), inter)
        m_t.append(mt)
        s.append(qk[h] * jnp.exp(log_d - mt))
        sc.append(jnp.exp(inter - mt))

    for h in heads:
        num = _dot(s[h], v[h]) + sc[h] * qc[h]
        den = (jnp.sum(s[h], axis=-1, keepdims=True)
               + sc[h] * jnp.sum(q[h] * per_row(nvec[h]), axis=-1, keepdims=True))
        hh = num / jnp.maximum(jnp.abs(den), jnp.exp(-m_t[h]))
        hc = hh - jnp.mean(hh, axis=-1, keepdims=True)
        hn = hc * lax.rsqrt(jnp.mean(hc * hc, axis=-1, keepdims=True) + 1e-6) * ng_ref[:, sls[h]]
        y = jax.nn.sigmoid(zo[:, sls[h]]) * (hn + skip_ref[:, sls[h]] * ch[h])
        y_ref[:, :, sls[h]] = y.reshape(nb, L, ML_HD).astype(BF16)

    for h in heads:
        m_new = last(m_t[h])
        b_last = last(b_col[h])
        wj = jnp.exp(per_row(b_last - m_new) - b_col[h] + li_col[h])
        dec = jnp.exp(b_last + m_prev[h] - m_new)
        wv = wj * v[h]
        for j in range(nb):
            js = slice(j * L, (j + 1) * L)
            c_out[j, h] = dec[j] * cmat[(h, j)] + _dot_tn(wv[js], k[h][js])
        n_out[:, h:h + 1, :] = dec * nvec[h] + jnp.sum((wj * k[h]).reshape(nb, L, ML_HD), axis=1, keepdims=True)
        m_out[:, h:h + 1, :] = jnp.broadcast_to(m_new, (nb, 1, ML_HD))

    if carry:
        @pl.when(ci == pl.num_programs(1) - 1)
        def _():
            cout_ref[...] = c_ref[...]
            nout_ref[...] = n_ref[...]
            mout_ref[...] = m_ref[...]


def _mlstm(z3, zift, buf, c_all, n_all, m_all, p, l, nb, L):
    bsz, tlen, _ = z3.shape
    c = MIX_W
    nc = tlen // L
    rows = nb * L
    carry = nc > 1
    bcol = jnp.broadcast_to(p["mlstm_bif"][l][:, None], (8, rows))
    vec = lambda: pl.BlockSpec((None, 1, c), lambda b, i: (l, 0, 0))
    zspec = lambda blk: pl.BlockSpec((nb, L, c), lambda b, i, blk=blk: (b, i, blk))
    hmat = lambda: pl.BlockSpec((None, ML_H, ML_HD, ML_HD), lambda b, i: (l, 0, 0, 0))
    cspec = lambda: pl.BlockSpec((None, nb, ML_H, ML_HD, ML_HD), lambda b, i: (l, b, 0, 0, 0))
    nspec = lambda: pl.BlockSpec((None, nb, ML_H, ML_HD), lambda b, i: (l, b, 0, 0))
    kern = functools.partial(_mlstm_kernel, nb=nb, L=L, carry=carry)
    tiny = (1, 8, 128)
    return pl.pallas_call(
        kern,
        grid=(bsz // nb, nc),
        in_specs=[zspec(ZB_MX), zspec(ZB_MV), zspec(ZB_MO),
                  pl.BlockSpec((nb, L, 128), lambda b, i: (b, i, Z_MIF // 128)),
                  pl.BlockSpec((None, 8, rows), lambda b, i: (b * nc + i, 0, 0)),
                  pl.BlockSpec((None, nb, CONV_W - 1, c), lambda b, i: (l, b, 0, 0)),
                  cspec(), nspec(), nspec(),
                  pl.BlockSpec((None, CONV_W, c), lambda b, i: (l, 0, 0)),
                  vec(), hmat(), hmat(),
                  pl.BlockSpec((None, 1, 128), lambda b, i: (l, 0, 0)),
                  pl.BlockSpec((8, rows), lambda b, i: (0, 0)),
                  vec(), vec()],
        out_specs=[pl.BlockSpec((nb, L, c), lambda b, i: (b, i, 0)), cspec(), nspec(), nspec()],
        out_shape=[jax.ShapeDtypeStruct((bsz, tlen, c), BF16),
                   jax.ShapeDtypeStruct(c_all.shape, F32),
                   jax.ShapeDtypeStruct(n_all.shape, F32),
                   jax.ShapeDtypeStruct(m_all.shape, F32)],
        input_output_aliases={6: 1, 7: 2, 8: 3},
        scratch_shapes=[pltpu.VMEM((nb, 8 + L, c), F32),
                        pltpu.VMEM((nb, ML_H, ML_HD, ML_HD) if carry else tiny, F32),
                        pltpu.VMEM((nb, ML_H, ML_HD) if carry else tiny, F32),
                        pltpu.VMEM((nb, ML_H, ML_HD) if carry else tiny, F32)],
        compiler_params=_cparams(("parallel", "arbitrary")),
        name="mlstm",
    )(z3, z3, z3, z3, zift, buf, c_all, n_all, m_all,
      p["mlstm_conv_w"], p["mlstm_conv_b"], p["mlstm_wq"], p["mlstm_wk"], p["mlstm_brow"], bcol,
      p["mlstm_norm_g"], p["mlstm_skip"])


def _block_diag(w):
    dp, nb, d, _ = w.shape
    eye = jnp.eye(nb, dtype=w.dtype)
    return jnp.einsum("lnij,nm->lnimj", w, eye).reshape(dp, nb * d, nb * d)


SRC_TAIL = 3584
SRC_ML = SRC_TAIL + RW_TAIL
SRC_MIF = 5280
PACK_ROWS = 512


def _pack_mix_kernel(a_ref, mif_ref, o_ref):
    j = pl.program_id(1)
    last = pl.num_programs(1) - 1

    @pl.when(j < last)
    def _():
        o_ref[...] = a_ref[0].astype(BF16)

    @pl.when(j == last)
    def _():
        pad = lambda n: jnp.zeros((n, D_MODEL), F32)
        tile = jnp.concatenate([a_ref[0, 0:RW_TAIL, :], pad(RW_TAIL_PAD - RW_TAIL),
                                mif_ref[0], pad(PACK_ROWS - RW_TAIL_PAD - 8)], axis=0)
        o_ref[...] = tile.astype(BF16)


def _pack_mix(w_in_t):
    n_plain = SRC_TAIL // PACK_ROWS
    n_tiles = Z_COLS // PACK_ROWS

    def src_row(l, j):
        shifted = SRC_ML + (j - n_plain) * PACK_ROWS
        row = jnp.where(j < n_plain, j * PACK_ROWS, jnp.where(j < n_tiles - 1, shifted, SRC_TAIL))
        return (l, pl.multiple_of(row, 8), 0)

    return pl.pallas_call(
        _pack_mix_kernel,
        grid=(DEPTH, n_tiles),
        in_specs=[pl.BlockSpec((pl.Element(1), pl.Element(PACK_ROWS), pl.Element(D_MODEL)), src_row),
                  pl.BlockSpec((pl.Element(1), pl.Element(8), pl.Element(D_MODEL)), lambda l, j: (l, SRC_MIF, 0))],
        out_specs=pl.BlockSpec((None, PACK_ROWS, D_MODEL), lambda l, j: (l, j, 0)),
        out_shape=jax.ShapeDtypeStruct((DEPTH, Z_COLS, D_MODEL), BF16),
        compiler_params=_cparams(("parallel", "arbitrary")),
        name="pack_mix",
    )(w_in_t, w_in_t)


def _prepare(raw):
    p = {}
    wt = jnp.swapaxes(raw["w_in"], 1, 2)
    p["w_mix"] = _pack_mix(wt)
    p["w_in_t"] = wt
    for name in ("w_out", "w_ff_out"):
        p[name] = raw[name].astype(BF16)
    p["w_ff_in"] = raw["w_ff_in"]
    p["w_branch"] = raw["w_branch"]
    row = lambda a: a.reshape(DEPTH, 1, -1)
    for name in ("norm_pre_mix", "norm_post_mix", "norm_pre_ffn", "norm_post_ffn",
                 "lru_conv_b", "lru_ba", "lru_bx", "lru_lambda", "gmlp_ln_g", "gmlp_ln_b",
                 "rwkv_w0", "rwkv_a0", "rwkv_kk", "rwkv_ka", "rwkv_rk", "rwkv_ln_g", "rwkv_ln_b",
                 "mlstm_conv_b", "mlstm_norm_g", "mlstm_skip"):
        p[name] = row(raw[name])
    p["lru_conv_w"] = raw["lru_conv_w"]
    p["mlstm_conv_w"] = raw["mlstm_conv_w"]
    p["lru_wa"] = _block_diag(raw["lru_wa"]).astype(BF16)
    p["lru_wx"] = _block_diag(raw["lru_wx"]).astype(BF16)
    mu = raw["rwkv_mu"]
    p["rwkv_mu_rkv"] = row(mu[:, :RW_RKV])
    p["rwkv_mu_tail"] = row(jnp.pad(mu[:, RW_RKV:], ((0, 0), (0, RW_TAIL_PAD - RW_TAIL))))

    def lora(w, lo):
        return jnp.pad(w, ((0, 0), (lo, RW_TAIL_PAD - lo - w.shape[1]), (0, 0))).astype(BF16)

    p["rwkv_w2"] = lora(raw["rwkv_w2"], 0)
    p["rwkv_a2"] = lora(raw["rwkv_a2"], 32)
    p["rwkv_g2"] = lora(raw["rwkv_g2"], 64)
    p["mlstm_wq"] = raw["mlstm_wq"].astype(BF16)
    p["mlstm_wk"] = raw["mlstm_wk"].astype(BF16)
    bif = jnp.concatenate([raw["mlstm_bi"], raw["mlstm_bf"]], axis=-1)
    p["mlstm_bif"] = bif
    p["mlstm_brow"] = row(jnp.pad(bif, ((0, 0), (0, 128 - 8))))
    p["gmlp_ws"] = raw["gmlp_ws"]
    p["gmlp_bs"] = raw["gmlp_bs"]
    return p


def _gmlp_mix_weights(p, tlen):
    L = min(GM_CHUNK, tlen)
    rep = GM_CHUNK // L
    ws = jnp.tril(p["gmlp_ws"][:, :, :L, :L])
    eye = jnp.eye(rep, dtype=ws.dtype)
    ws_mix = jnp.einsum("lgps,ab->lgapbs", ws, eye).reshape(DEPTH, GM_GROUPS, GM_CHUNK, GM_CHUNK).astype(BF16)
    bias = jnp.swapaxes(p["gmlp_bs"][:, :, :L], 1, 2)
    bias = jnp.repeat(bias, MIX_W // GM_GROUPS, axis=2)
    bias = jnp.tile(bias, (1, rep, 1))
    return ws_mix, bias


def _group_forward(x3, states, p, is_start, depth=DEPTH):
    bsz, tlen, _ = x3.shape
    m = bsz * tlen
    x = x3.reshape(m, D_MODEL)
    lru_buf, lru_h, rw_shift, rw_s, ml_buf, ml_c, ml_n, ml_m = states
    tm = min(TM_DENSE, m)
    if is_start:
        lru_bb, lru_tt = min(LRU_SEQS_P, bsz), min(LRU_TILE_P, tlen)
        rw_nb, rw_l, ml_nb, ml_l = min(RW_SEQS_P, bsz), RW_CHUNK, 1, ML_CHUNK_P
    else:
        lru_bb, lru_tt, rw_nb, rw_l, ml_nb, ml_l = 16, tlen, RW_ROWS // tlen, tlen, ML_ROWS // tlen, tlen
    ws_mix, gm_bias = _gmlp_mix_weights(p, tlen)
    nl = lru_h.shape[0]
    h_all = lru_h.reshape(nl, bsz, 1, MIX_W)
    sh_rkv = rw_shift[:, :, :RW_RKV].reshape(nl, bsz, 1, RW_RKV)
    sh_tail = jnp.pad(rw_shift[:, :, RW_RKV:], ((0, 0), (0, 0), (0, RW_TAIL_PAD - RW_TAIL)))
    sh_tail = sh_tail.reshape(nl, bsz, 1, RW_TAIL_PAD)
    s_all, c_all, n_all = rw_s, ml_c, ml_n
    m_all = jnp.broadcast_to(ml_m[:, :, :, None], (nl, bsz, ML_H, ML_HD))
    new_states = [[] for _ in range(8)]
    gm_vs = []
    hn = _prenorm(x, p["norm_pre_mix"], 0, tm)
    for l in range(depth):
        z = _inproj(hn, p["w_mix"], l, tm)
        z3 = z.reshape(bsz, tlen, Z_COLS)

        y_a, h_all = _lru(z3, lru_buf, h_all, p, l, is_start, lru_bb, lru_tt)
        y_b, gm_v = _gmlp(z, p, l, ws_mix, gm_bias)
        y_c, s_all = _rwkv(z3, sh_rkv, sh_tail, s_all, p, l, rw_nb, rw_l)
        y_c = y_c.reshape(m, MIX_W)
        ml_rows = ml_nb * ml_l
        zift = jnp.swapaxes(z[:, Z_MIF:Z_MIF + 8].reshape(m // ml_rows, ml_rows, 8), 1, 2)
        y_d, c_all, n_all, m_all = _mlstm(z3, zift, ml_buf, c_all, n_all, m_all, p, l, ml_nb, ml_l)
        y_d = y_d.reshape(m, MIX_W)

        merged = _merge(hn, (y_a.reshape(m, MIX_W), y_b, y_c, y_d), p["w_in_t"], p["w_branch"], l, tm)
        x1, hf = _outproj(merged, p["w_out"], x, p["norm_post_mix"], p["norm_pre_ffn"], l, min(TM_OUTPROJ, m))
        hmid = _ffup(hf, p["w_ff_in"], l, tm)
        x, hn = _ffdown(hmid, p["w_ff_out"], x1, p["norm_post_ffn"], p["norm_pre_mix"], l, min(TM_FFDOWN, m))

        new_states[0].append(z3[:, tlen - (CONV_W - 1):, 0:MIX_W])
        new_states[2].append(jnp.concatenate(
            [z3[:, tlen - 1, ZB_R * MIX_W:ZB_R * MIX_W + RW_RKV], z3[:, tlen - 1, Z_TAIL:Z_TAIL + RW_TAIL]], axis=-1))
        new_states[4].append(z3[:, tlen - (CONV_W - 1):, ZB_MX * MIX_W:(ZB_MX + 1) * MIX_W])
        gm_vs.append(gm_v.reshape(bsz, tlen, MIX_W))
    out_states = [jnp.stack(new_states[0], axis=0), h_all.reshape(nl, bsz, MIX_W)[:depth],
                  jnp.stack(new_states[2], axis=0), s_all[:depth],
                  jnp.stack(new_states[4], axis=0), c_all[:depth], n_all[:depth], m_all[:depth, :, :, 0]]
    return x.reshape(bsz, tlen, D_MODEL), out_states, jnp.stack(gm_vs, axis=0)


def kernel(x_prompt, x_sample, state_lru_conv, state_lru_h, state_rwkv_shift, state_rwkv_wkv, state_mlstm_conv, state_mlstm_C, state_mlstm_n, state_mlstm_m, norm_pre_mix, norm_post_mix, norm_pre_ffn, norm_post_ffn, w_in, lru_conv_w, lru_conv_b, lru_wa, lru_ba, lru_wx, lru_bx, lru_lambda, gmlp_ln_g, gmlp_ln_b, gmlp_ws, gmlp_bs, rwkv_mu, rwkv_w0, rwkv_w2, rwkv_a0, rwkv_a2, rwkv_g2, rwkv_kk, rwkv_ka, rwkv_rk, rwkv_ln_g, rwkv_ln_b, mlstm_conv_w, mlstm_conv_b, mlstm_wq, mlstm_wk, mlstm_bi, mlstm_bf, mlstm_norm_g, mlstm_skip, w_branch, w_out, w_ff_in, w_ff_out):
    raw = dict(norm_pre_mix=norm_pre_mix, norm_post_mix=norm_post_mix, norm_pre_ffn=norm_pre_ffn,
               norm_post_ffn=norm_post_ffn, w_in=w_in, lru_conv_w=lru_conv_w, lru_conv_b=lru_conv_b,
               lru_wa=lru_wa, lru_ba=lru_ba, lru_wx=lru_wx, lru_bx=lru_bx, lru_lambda=lru_lambda,
               gmlp_ln_g=gmlp_ln_g, gmlp_ln_b=gmlp_ln_b, gmlp_ws=gmlp_ws, gmlp_bs=gmlp_bs,
               rwkv_mu=rwkv_mu, rwkv_w0=rwkv_w0, rwkv_w2=rwkv_w2, rwkv_a0=rwkv_a0, rwkv_a2=rwkv_a2,
               rwkv_g2=rwkv_g2, rwkv_kk=rwkv_kk, rwkv_ka=rwkv_ka, rwkv_rk=rwkv_rk, rwkv_ln_g=rwkv_ln_g,
               rwkv_ln_b=rwkv_ln_b, mlstm_conv_w=mlstm_conv_w, mlstm_conv_b=mlstm_conv_b, mlstm_wq=mlstm_wq,
               mlstm_wk=mlstm_wk, mlstm_bi=mlstm_bi, mlstm_bf=mlstm_bf, mlstm_norm_g=mlstm_norm_g,
               mlstm_skip=mlstm_skip, w_branch=w_branch, w_out=w_out, w_ff_in=w_ff_in, w_ff_out=w_ff_out)
    p = _prepare(raw)
    bp = x_prompt.shape[0]
    zero = lambda *s: jnp.zeros((DEPTH, bp) + s, F32)
    prompt_states = (zero(CONV_W - 1, MIX_W), zero(MIX_W), zero(RW_RKV + RW_TAIL), zero(RW_H, RW_HD, RW_HD),
                     zero(CONV_W - 1, MIX_W), zero(ML_H, ML_HD, ML_HD), zero(ML_H, ML_HD), zero(ML_H))
    sample_states = (state_lru_conv, state_lru_h, state_rwkv_shift, state_rwkv_wkv,
                     state_mlstm_conv, state_mlstm_C, state_mlstm_n, state_mlstm_m)
    yp, st_p, _ = _group_forward(x_prompt, prompt_states, p, True)
    ys, st_s, gm_v = _group_forward(x_sample, sample_states, p, False)
    return (yp, ys, *st_p, *st_s, gm_v)
```
